```python
import jax, jax.numpy as jnp
from jax import lax
import numpy as np

D_MODEL = 2048
BATCH = 4
SEQ = 2048
DEPTH = 4

ATT_HEADS = 12
ATT_KV_HEADS = 4
HEAD_DIM = 64
WINDOW = 128
ATT_BLOCK = 128
ROPE_THETA = 10000.0
ATT_Q_W = ATT_HEADS * HEAD_DIM
ATT_KV_W = ATT_KV_HEADS * HEAD_DIM
POOL_WINDOWS = (2, 4, 8, 16)
POOL_GROUPS = 4
POOL_GROUP_DIM = 192
POOL_WIDTH = POOL_GROUPS * POOL_GROUP_DIM
GLA_HEADS = 4
GLA_DK = 96
GLA_DV = 192
GLA_QK_W = GLA_HEADS * GLA_DK
GLA_V_W = GLA_HEADS * GLA_DV
GLA_LOWRANK = 16
GLA_TAU = 16.0
GLA_CHUNK = 64
N_BRANCH = 3
D_FF = 5632
EPS = 1e-6
W_IN_COLS = ATT_Q_W + 2 * ATT_KV_W + POOL_WIDTH + 2 * GLA_QK_W + 2 * GLA_V_W + GLA_LOWRANK + N_BRANCH * D_MODEL

kernel_name = "hybrid_gated_swa_pool_gla_macaron"


def rms_norm(x, g):
    xf = x.astype(jnp.float32)
    y = xf * lax.rsqrt(jnp.mean(xf * xf, axis=-1, keepdims=True) + EPS)
    return (y * g.astype(jnp.float32)).astype(x.dtype)


def swiglu(h, wi, wo):
    a, b = jnp.split(h @ wi, 2, axis=-1)
    return (jax.nn.silu(a) * b) @ wo


def rope(x, positions):
    half = HEAD_DIM // 2
    inv_freq = ROPE_THETA ** (-jnp.arange(half, dtype=jnp.float32) / half)
    ang = positions.astype(jnp.float32)[..., None] * inv_freq
    cos = jnp.cos(ang)[:, :, None, :]
    sin = jnp.sin(ang)[:, :, None, :]
    xf = x.astype(jnp.float32)
    x1, x2 = xf[..., :half], xf[..., half:]
    out = jnp.concatenate([x1 * cos - x2 * sin, x2 * cos + x1 * sin], axis=-1)
    return out.astype(x.dtype)


def sliding_window_sink_attention(q, k, v, sinks):
    B, S, H, Dh = q.shape
    nb = S // ATT_BLOCK
    G = H // ATT_KV_HEADS
    qb = q.reshape(B, nb, ATT_BLOCK, ATT_KV_HEADS, G, Dh)
    kb = k.reshape(B, nb, ATT_BLOCK, ATT_KV_HEADS, Dh)
    vb = v.reshape(B, nb, ATT_BLOCK, ATT_KV_HEADS, Dh)
    pad = ((0, 0), (1, 0), (0, 0), (0, 0), (0, 0))
    kw = jnp.concatenate([jnp.pad(kb, pad)[:, :-1], kb], axis=2)
    vw = jnp.concatenate([jnp.pad(vb, pad)[:, :-1], vb], axis=2)
    s = jnp.einsum('bnqkgd,bnskd->bnkgqs', qb, kw, preferred_element_type=jnp.float32) * (HEAD_DIM ** -0.5)
    qi = jnp.arange(ATT_BLOCK)[:, None]
    si = jnp.arange(2 * ATT_BLOCK)[None, :]
    rel = ATT_BLOCK + qi - si
    band = (rel >= 0) & (rel < WINDOW)
    has_prev = (jnp.arange(nb) > 0)[:, None, None] | (si >= ATT_BLOCK)[None]
    mask = band[None] & has_prev
    s = jnp.where(mask[None, :, None, None], s, -jnp.inf)
    sink = sinks.astype(jnp.float32).reshape(ATT_KV_HEADS, G)[None, None, :, :, None, None]
    sink_col = jnp.broadcast_to(sink, s.shape[:-1] + (1,))
    p = jax.nn.softmax(jnp.concatenate([s, sink_col], axis=-1), axis=-1)[..., :-1]
    o = jnp.einsum('bnkgqs,bnskd->bnqkgd', p.astype(v.dtype), vw)
    return o.reshape(B, S, H * Dh)


def pool_mixer(u, w_pool, pool_scale):
    B, S, _ = u.shape
    uf = u.astype(jnp.float32)
    c = jnp.pad(jnp.cumsum(uf, axis=1), ((0, 0), (1, 0), (0, 0)))
    outs = []
    for gi, w in enumerate(POOL_WINDOWS):
        sl = slice(gi * POOL_GROUP_DIM, (gi + 1) * POOL_GROUP_DIM)
        cg = c[..., sl]
        upper = cg[:, 1:]
        lower = jnp.pad(cg, ((0, 0), (w - 1, 0), (0, 0)))[:, :S]
        cnt = jnp.minimum(jnp.arange(1, S + 1), w).astype(jnp.float32)[None, :, None]
        outs.append((upper - lower) / cnt - uf[..., sl])
    d = jnp.stack(outs, axis=2)
    y = jnp.einsum('bsgc,gcd->bsgd', d, w_pool.astype(jnp.float32)).reshape(B, S, POOL_WIDTH)
    return (y * pool_scale.astype(jnp.float32)).astype(u.dtype)


def gla_chunked(q, k, v, gk):
    B, S, H, dk = q.shape
    dv = v.shape[-1]
    nc = S // GLA_CHUNK

    def to_chunks(a):
        return a.reshape(B, nc, GLA_CHUNK, H, a.shape[-1]).transpose(1, 0, 3, 2, 4)

    q, k, v, gk = to_chunks(q * (GLA_DK ** -0.5)), to_chunks(k), to_chunks(v), to_chunks(gk)
    b = jnp.cumsum(gk, axis=3)
    b_last = b[..., -1:, :]
    q_t = q * jnp.exp(b)
    k_t = k * jnp.exp(-b)
    k_s = k * jnp.exp(b_last - b)
    causal = jnp.tril(jnp.ones((GLA_CHUNK, GLA_CHUNK), dtype=bool))
    a = jnp.where(causal, jnp.einsum('nbhid,nbhjd->nbhij', q_t, k_t), 0.0)
    o_intra = jnp.einsum('nbhij,nbhjv->nbhiv', a, v)

    def step(state, inp):
        qn, kn, vn, decay = inp
        o = jnp.einsum('bhid,bhdv->bhiv', qn, state)
        state = state * decay[:, :, 0, :, None] + jnp.einsum('bhjd,bhjv->bhdv', kn, vn)
        return state, o

    s0 = jnp.zeros((B, H, dk, dv), jnp.float32)
    _, o_inter = lax.scan(step, s0, (q_t, k_s, v, jnp.exp(b_last)))
    o = o_intra + o_inter
    return o.transpose(1, 0, 3, 2, 4).reshape(B, S, H, dv)


def hybrid_mixer(h, positions, w_in, b_gate, att_sinks, w_pool, pool_scale, w_gla_a2, b_gla_a,
                 gla_norm, w_br_att, w_br_pool, w_br_gla, w_out):
    B, S, _ = h.shape
    proj = h @ w_in
    sizes = [ATT_Q_W, ATT_KV_W, ATT_KV_W, POOL_WIDTH, GLA_QK_W, GLA_QK_W, GLA_V_W, GLA_V_W, GLA_LOWRANK]
    points, acc = [], 0
    for sz in sizes:
        acc += sz
        points.append(acc)
    qa, ka, va, pu, qg, kg, vg, og, lr, gate_logits = jnp.split(proj, points, axis=-1)

    qa = rope(qa.reshape(B, S, ATT_HEADS, HEAD_DIM), positions)
    ka = rope(ka.reshape(B, S, ATT_KV_HEADS, HEAD_DIM), positions)
    va = va.reshape(B, S, ATT_KV_HEADS, HEAD_DIM)
    y_att = sliding_window_sink_attention(qa, ka, va, att_sinks)

    y_pool = pool_mixer(pu, w_pool, pool_scale)

    f32 = jnp.float32
    gk = jax.nn.log_sigmoid((lr @ w_gla_a2).astype(f32) + b_gla_a.astype(f32)) / GLA_TAU
    o = gla_chunked(qg.astype(f32).reshape(B, S, GLA_HEADS, GLA_DK),
                    kg.astype(f32).reshape(B, S, GLA_HEADS, GLA_DK),
                    vg.astype(f32).reshape(B, S, GLA_HEADS, GLA_DV),
                    gk.reshape(B, S, GLA_HEADS, GLA_DK))
    o = o * lax.rsqrt(jnp.mean(o * o, axis=-1, keepdims=True) + EPS)
    o = o.reshape(B, S, GLA_V_W) * gla_norm.astype(f32) * jax.nn.silu(og.astype(f32))
    y_gla = o.astype(h.dtype)

    gates = jax.nn.sigmoid(gate_logits.astype(f32) + b_gate.astype(f32)).reshape(B, S, N_BRANCH, D_MODEL)
    merged = (gates[:, :, 0] * (y_att @ w_br_att).astype(f32)
              + gates[:, :, 1] * (y_pool @ w_br_pool).astype(f32)
              + gates[:, :, 2] * (y_gla @ w_br_gla).astype(f32))
    return merged.astype(h.dtype) @ w_out


def setup_inputs(seed: int = 0) -> dict:
    key = jax.random.key(seed)
    ks = jax.random.split(key, 24)
    L, D, F = DEPTH, D_MODEL, D_FF

    def nrm(k, shape, scale):
        return jax.random.normal(k, shape, jnp.float32) * scale

    def gain(k, shape):
        return 1.0 + 0.02 * jax.random.normal(k, shape, jnp.float32)

    offsets = jax.random.randint(ks[1], (BATCH, 1), 0, 4096, dtype=jnp.int32)
    positions = offsets + jnp.arange(SEQ, dtype=jnp.int32)[None, :]
    return {
        "x": jax.random.normal(ks[0], (BATCH, SEQ, D), jnp.float32),
        "positions": positions,
        "norm_ffn1": gain(ks[2], (L, D)),
        "ffn1_wi": nrm(ks[3], (L, D, 2 * F), D ** -0.5),
        "ffn1_wo": nrm(ks[4], (L, F, D), F ** -0.5),
        "norm_mix": gain(ks[5], (L, D)),
        "w_in": nrm(ks[6], (L, D, W_IN_COLS), D ** -0.5),
        "b_gate": nrm(ks[7], (L, N_BRANCH * D), 0.02),
        "att_sinks": nrm(ks[8], (L, ATT_HEADS), 0.5),
        "w_pool": nrm(ks[9], (L, POOL_GROUPS, POOL_GROUP_DIM, POOL_GROUP_DIM), POOL_GROUP_DIM ** -0.5),
        "pool_scale": gain(ks[10], (L, POOL_WIDTH)),
        "w_gla_a2": nrm(ks[11], (L, GLA_LOWRANK, GLA_QK_W), GLA_LOWRANK ** -0.5),
        "b_gla_a": nrm(ks[12], (L, GLA_QK_W), 0.1),
        "gla_norm": gain(ks[13], (L, GLA_V_W)),
        "w_br_att": nrm(ks[14], (L, ATT_Q_W, D), ATT_Q_W ** -0.5),
        "w_br_pool": nrm(ks[15], (L, POOL_WIDTH, D), POOL_WIDTH ** -0.5),
        "w_br_gla": nrm(ks[16], (L, GLA_V_W, D), GLA_V_W ** -0.5),
        "w_out": nrm(ks[17], (L, D, D), D ** -0.5),
        "norm_ffn2": gain(ks[18], (L, D)),
        "ffn2_wi": nrm(ks[19], (L, D, 2 * F), D ** -0.5),
        "ffn2_wo": nrm(ks[20], (L, F, D), F ** -0.5),
        "norm_final": gain(ks[21], (D,)),
    }


def reference(x, positions, norm_ffn1, ffn1_wi, ffn1_wo, norm_mix, w_in, b_gate, att_sinks, w_pool,
              pool_scale, w_gla_a2, b_gla_a, gla_norm, w_br_att, w_br_pool, w_br_gla, w_out,
              norm_ffn2, ffn2_wi, ffn2_wo, norm_final):
    for l in range(DEPTH):
        x = x + 0.5 * swiglu(rms_norm(x, norm_ffn1[l]), ffn1_wi[l], ffn1_wo[l])
        h = rms_norm(x, norm_mix[l])
        x = x + hybrid_mixer(h, positions, w_in[l], b_gate[l], att_sinks[l], w_pool[l], pool_scale[l],
                             w_gla_a2[l], b_gla_a[l], gla_norm[l], w_br_att[l], w_br_pool[l],
                             w_br_gla[l], w_out[l])
        x = x + 0.5 * swiglu(rms_norm(x, norm_ffn2[l]), ffn2_wi[l], ffn2_wo[l])
    return rms_norm(x, norm_final)
```

```python
import functools

import numpy as np
import jax
import jax.numpy as jnp
from jax import lax
from jax.experimental import pallas as pl
from jax.experimental.pallas import tpu as pltpu

F32 = jnp.float32
BF16 = jnp.bfloat16

D_MODEL = 2048
D_FF = 5632
ATT_HEADS = 12
ATT_KV_HEADS = 4
HEAD_DIM = 64
ATT_BLOCK = 128
ROPE_THETA = 10000.0
ATT_Q_W = ATT_HEADS * HEAD_DIM
ATT_KV_W = ATT_KV_HEADS * HEAD_DIM
POOL_WINDOWS = (2, 4, 8, 16)
POOL_GROUP_DIM = 192
POOL_WIDTH = 768
GLA_HEADS = 4
GLA_DK = 96
GLA_DV = 192
GLA_QK_W = GLA_HEADS * GLA_DK
GLA_V_W = GLA_HEADS * GLA_DV
GLA_LOWRANK = 16
GLA_TAU = 16.0
GLA_CHUNK = 64
N_BRANCH = 3
EPS = 1e-6

_IN_QA, _IN_KA, _IN_VA, _IN_PU = 0, 768, 1024, 1280
_IN_QG, _IN_KG, _IN_VG, _IN_OG = 2048, 2432, 2816, 3584
_IN_LR = 4352
_IN_GATE = 4368

PROJ_W = 4352
_BLK_GQK, _BLK_AQ, _BLK_PU, _BLK_GV, _BLK_GO = 0, 1, 2, 3, 4
_BLK_AK, _BLK_AV = 15, 16

LANES = 128
POOL_HALO = 16
VMEM_LIMIT = 56 * 1024 * 1024

FFN_TM, FFN_TF = 512, 512
INPROJ_TM = 512
INPROJ_NCH = 1024
POOL_TS = 512
GLA_TS = 256
MERGE_TM, MERGE_TN = 1024, 512
OUT_TM, OUT_TN = 1024, 1024
ROPE_TM = 1024


def _cparams(*sem):
    return pltpu.CompilerParams(dimension_semantics=sem, vmem_limit_bytes=VMEM_LIMIT)


def _rms(x, g):
    return x * lax.rsqrt(jnp.mean(x * x, axis=-1, keepdims=True) + EPS) * g


def _ffn_body(x_ref, g_ref, wa_ref, wb_ref, wo_ref, gf_ref, o_ref, h_ref, *, nj, final_norm):
    j = pl.program_id(1)

    @pl.when(j == 0)
    def _():
        h_ref[...] = _rms(x_ref[...], g_ref[...]).astype(BF16)

    h = h_ref[...]
    a = jnp.dot(h, wa_ref[...], preferred_element_type=F32)
    b = jnp.dot(h, wb_ref[...], preferred_element_type=F32)
    u = (a * jax.nn.sigmoid(a) * b).astype(BF16)
    p = jnp.dot(u, wo_ref[...], preferred_element_type=F32)

    @pl.when(j == 0)
    def _():
        o_ref[...] = p

    @pl.when(j > 0)
    def _():
        o_ref[...] += p

    @pl.when(j == nj - 1)
    def _():
        y = x_ref[...] + 0.5 * o_ref[...]
        if final_norm:
            y = _rms(y, gf_ref[...])
        o_ref[...] = y


def _ffn(x, g, wi, wo, gf, final_norm):
    t = x.shape[0]
    tm, tf = FFN_TM, FFN_TF
    nj = D_FF // tf
    return pl.pallas_call(
        functools.partial(_ffn_body, nj=nj, final_norm=final_norm),
        out_shape=jax.ShapeDtypeStruct((t, D_MODEL), F32),
        grid=(t // tm, nj),
        in_specs=[
            pl.BlockSpec((tm, D_MODEL), lambda i, j: (i, 0)),
            pl.BlockSpec((1, D_MODEL), lambda i, j: (0, 0)),
            pl.BlockSpec((D_MODEL, tf), lambda i, j: (0, j)),
            pl.BlockSpec((D_MODEL, tf), lambda i, j: (0, j + nj)),
            pl.BlockSpec((tf, D_MODEL), lambda i, j: (j, 0)),
            pl.BlockSpec((1, D_MODEL), lambda i, j: (0, 0)),
        ],
        out_specs=pl.BlockSpec((tm, D_MODEL), lambda i, j: (i, 0)),
        scratch_shapes=[pltpu.VMEM((tm, D_MODEL), BF16)],
        compiler_params=_cparams("parallel", "arbitrary"),
        name="ffn_final" if final_norm else "ffn",
    )(x, g, wi, wi, wo, gf)


def _rope_table_body(pos_ref, freq_ref, sign_ref, cos_ref, sin_ref):
    ang = pos_ref[...].astype(F32) * freq_ref[...]
    cos_ref[...] = jnp.cos(ang)
    sin_ref[...] = jnp.sin(ang) * sign_ref[...]


def _rope_tables(positions):
    t = positions.size
    half = HEAD_DIM // 2
    inv_freq = ROPE_THETA ** (-jnp.arange(half, dtype=F32) / half)
    freq = jnp.tile(inv_freq, LANES // half)[None, :]
    sign = jnp.tile(jnp.concatenate([-jnp.ones((half,), F32), jnp.ones((half,), F32)]),
                    LANES // HEAD_DIM)[None, :]
    tm = ROPE_TM
    return pl.pallas_call(
        _rope_table_body,
        out_shape=(jax.ShapeDtypeStruct((t, LANES), F32), jax.ShapeDtypeStruct((t, LANES), F32)),
        grid=(t // tm,),
        in_specs=[
            pl.BlockSpec((tm, 1), lambda i: (i, 0)),
            pl.BlockSpec((1, LANES), lambda i: (0, 0)),
            pl.BlockSpec((1, LANES), lambda i: (0, 0)),
        ],
        out_specs=(pl.BlockSpec((tm, LANES), lambda i: (i, 0)),
                   pl.BlockSpec((tm, LANES), lambda i: (i, 0))),
        compiler_params=_cparams("parallel"),
        name="rope_tables",
    )(positions.reshape(t, 1), freq, sign)


def _inproj_body(x_ref, g_ref, w_ref, wlr_ref, wa2_ref, ba_ref, proj_ref, gk_ref, h_ref):
    h = _rms(x_ref[...], g_ref[...]).astype(BF16)
    h_ref[...] = h
    for c0 in range(0, PROJ_W, INPROJ_NCH):
        c1 = min(c0 + INPROJ_NCH, PROJ_W)
        proj_ref[:, c0:c1] = jnp.dot(h, w_ref[:, c0:c1],
                                     preferred_element_type=F32).astype(proj_ref.dtype)
    lr = jnp.dot(h, wlr_ref[...], preferred_element_type=F32)
    z = jnp.dot(lr.astype(BF16), wa2_ref[...], preferred_element_type=F32) + ba_ref[...]
    gk_ref[...] = (jnp.minimum(z, 0.0) - jnp.log1p(jnp.exp(-jnp.abs(z)))) * (1.0 / GLA_TAU)


def _inproj(x, g, w_main, w_lr, w_a2, b_a):
    t = x.shape[0]
    tm = INPROJ_TM
    const = lambda i: (0, 0)
    return pl.pallas_call(
        _inproj_body,
        out_shape=(jax.ShapeDtypeStruct((t, PROJ_W), BF16),
                   jax.ShapeDtypeStruct((t, GLA_QK_W), F32),
                   jax.ShapeDtypeStruct((t, D_MODEL), BF16)),
        grid=(t // tm,),
        in_specs=[
            pl.BlockSpec((tm, D_MODEL), lambda i: (i, 0)),
            pl.BlockSpec((1, D_MODEL), const),
            pl.BlockSpec((D_MODEL, PROJ_W), const, pipeline_mode=pl.Buffered(1)),
            pl.BlockSpec((D_MODEL, LANES), const),
            pl.BlockSpec((LANES, GLA_QK_W), const),
            pl.BlockSpec((1, GLA_QK_W), const),
        ],
        out_specs=(pl.BlockSpec((tm, PROJ_W), lambda i: (i, 0)),
                   pl.BlockSpec((tm, GLA_QK_W), lambda i: (i, 0)),
                   pl.BlockSpec((tm, D_MODEL), lambda i: (i, 0))),
        compiler_params=_cparams("parallel"),
        name="inproj",
    )(x, g, w_main, w_lr, w_a2, b_a)


def _swap_halves(x):
    lane = lax.broadcasted_iota(jnp.int32, x.shape, 1)
    first = (lane % HEAD_DIM) < (HEAD_DIM // 2)
    return jnp.where(first, pltpu.roll(x, LANES - HEAD_DIM // 2, 1), pltpu.roll(x, HEAD_DIM // 2, 1))


def _rope(x, cos, sin):
    cols = []
    for c in range(x.shape[1] // LANES):
        xc = x[:, c * LANES:(c + 1) * LANES]
        cols.append(xc * cos + _swap_halves(xc) * sin)
    return jnp.concatenate(cols, axis=1)


def _attn_body(sink_ref, q_ref, kc_ref, kp_ref, vc_ref, vp_ref, cc_ref, sc_ref, cp_ref, sp_ref, o_ref):
    n = pl.program_id(1)
    blk = ATT_BLOCK
    q = _rope(q_ref[...].astype(F32), cc_ref[...], sc_ref[...]).astype(BF16)
    k = jnp.concatenate([_rope(kp_ref[...].astype(F32), cp_ref[...], sp_ref[...]),
                         _rope(kc_ref[...].astype(F32), cc_ref[...], sc_ref[...])], axis=0).astype(BF16)
    v = jnp.concatenate([vp_ref[...], vc_ref[...]], axis=0)

    qi = lax.broadcasted_iota(jnp.int32, (blk, 2 * blk), 0)
    si = lax.broadcasted_iota(jnp.int32, (blk, 2 * blk), 1)
    rel = blk + qi - si
    mask = (rel >= 0) & (rel < blk) & ((si >= blk) | (n > 0))

    group = ATT_HEADS // ATT_KV_HEADS
    outs = []
    for hd in range(ATT_HEADS):
        g = hd // group
        qh = q[:, hd * HEAD_DIM:(hd + 1) * HEAD_DIM]
        kg = k[:, g * HEAD_DIM:(g + 1) * HEAD_DIM]
        vg = v[:, g * HEAD_DIM:(g + 1) * HEAD_DIM]
        s = lax.dot_general(qh, kg, (((1,), (1,)), ((), ())),
                            preferred_element_type=F32) * (HEAD_DIM ** -0.5)
        s = jnp.where(mask, s, -1e30)
        sink = sink_ref[hd]
        m = jnp.maximum(jnp.max(s, axis=1, keepdims=True), sink)
        e = jnp.exp(s - m)
        denom = jnp.sum(e, axis=1, keepdims=True) + jnp.exp(sink - m)
        o = jnp.dot(e.astype(BF16), vg, preferred_element_type=F32)
        outs.append(o / denom)
    o_ref[...] = jnp.concatenate(outs, axis=1).astype(o_ref.dtype)


def _attention(proj, cos_t, sin_t, sinks, batch, seq):
    t = proj.shape[0]
    blk = ATT_BLOCK
    nb = seq // blk
    cur = lambda b, n: b * nb + n
    prev = lambda b, n: b * nb + jnp.maximum(n - 1, 0)
    return pl.pallas_call(
        _attn_body,
        out_shape=jax.ShapeDtypeStruct((t, ATT_Q_W), BF16),
        grid=(batch, nb),
        in_specs=[
            pl.BlockSpec(memory_space=pltpu.SMEM),
            pl.BlockSpec((blk, ATT_Q_W), lambda b, n: (cur(b, n), _BLK_AQ)),
            pl.BlockSpec((blk, ATT_KV_W), lambda b, n: (cur(b, n), _BLK_AK)),
            pl.BlockSpec((blk, ATT_KV_W), lambda b, n: (prev(b, n), _BLK_AK)),
            pl.BlockSpec((blk, ATT_KV_W), lambda b, n: (cur(b, n), _BLK_AV)),
            pl.BlockSpec((blk, ATT_KV_W), lambda b, n: (prev(b, n), _BLK_AV)),
            pl.BlockSpec((blk, LANES), lambda b, n: (cur(b, n), 0)),
            pl.BlockSpec((blk, LANES), lambda b, n: (cur(b, n), 0)),
            pl.BlockSpec((blk, LANES), lambda b, n: (prev(b, n), 0)),
            pl.BlockSpec((blk, LANES), lambda b, n: (prev(b, n), 0)),
        ],
        out_specs=pl.BlockSpec((blk, ATT_Q_W), lambda b, n: (cur(b, n), 0)),
        compiler_params=_cparams("parallel", "arbitrary"),
        name="swa_attention",
    )(sinks, proj, proj, proj, proj, proj, cos_t, sin_t, cos_t, sin_t)


def _pool_body(u_ref, halo_ref, w_ref, scale_ref, o_ref):
    s_idx = pl.program_id(1)
    ts = u_ref.shape[0]
    u = u_ref[...].astype(F32)
    halo = jnp.where(s_idx > 0, halo_ref[...].astype(F32), 0.0)
    a1 = jnp.concatenate([halo, u], axis=0)
    length = ts + POOL_HALO
    a2 = a1[1:] + a1[:length - 1]
    a4 = a2[2:] + a2[:length - 3]
    a8 = a4[4:] + a4[:length - 7]
    a16 = a8[8:] + a8[:length - 15]
    sums = {2: a2[POOL_HALO - 1:], 4: a4[POOL_HALO - 3:], 8: a8[POOL_HALO - 7:], 16: a16[POOL_HALO - 15:]}

    tok = s_idx * ts + lax.broadcasted_iota(jnp.int32, (ts, 1), 0)
    chan = lax.broadcasted_iota(jnp.int32, (1, POOL_WIDTH), 1)
    d = jnp.zeros((ts, POOL_WIDTH), F32)
    for gi, w in enumerate(POOL_WINDOWS):
        inv_cnt = 1.0 / jnp.minimum(tok + 1, w).astype(F32)
        in_group = (chan >= gi * POOL_GROUP_DIM) & (chan < (gi + 1) * POOL_GROUP_DIM)
        d = jnp.where(in_group, sums[w] * inv_cnt, d)
    d = d - u
    y = jnp.dot(d.astype(BF16), w_ref[...], preferred_element_type=F32)
    o_ref[...] = (y * scale_ref[...]).astype(o_ref.dtype)


def _pool(proj, w_bd, scale, batch, seq):
    t = proj.shape[0]
    ts = POOL_TS
    ns = seq // ts
    return pl.pallas_call(
        _pool_body,
        out_shape=jax.ShapeDtypeStruct((t, POOL_WIDTH), BF16),
        grid=(batch, ns),
        in_specs=[
            pl.BlockSpec((ts, POOL_WIDTH), lambda b, s: (b * ns + s, _BLK_PU)),
            pl.BlockSpec((POOL_HALO, POOL_WIDTH),
                         lambda b, s: (jnp.maximum((b * ns + s) * (ts // POOL_HALO) - 1, 0), _BLK_PU)),
            pl.BlockSpec((POOL_WIDTH, POOL_WIDTH), lambda b, s: (0, 0)),
            pl.BlockSpec((1, POOL_WIDTH), lambda b, s: (0, 0)),
        ],
        out_specs=pl.BlockSpec((ts, POOL_WIDTH), lambda b, s: (b * ns + s, 0)),
        compiler_params=_cparams("parallel", "arbitrary"),
        name="pool_mixer",
    )(proj, proj, w_bd, scale)


def _gla_body(qk_ref, v_ref, og_ref, gk_ref, norm_ref, tril_ref, bd_ref, o_ref, st_ref, oacc_ref):
    s_idx = pl.program_id(1)
    ts = qk_ref.shape[0]
    c = GLA_CHUNK

    @pl.when(s_idx == 0)
    def _():
        st_ref[...] = jnp.zeros_like(st_ref)

    qk = qk_ref[...].astype(F32)
    q = qk[:, :GLA_QK_W] * (GLA_DK ** -0.5)
    k = qk[:, GLA_QK_W:]
    v = v_ref[...]
    b = jnp.dot(tril_ref[...], gk_ref[...], preferred_element_type=F32,
                precision=lax.Precision.HIGHEST)
    q_t = (q * jnp.exp(b)).astype(BF16)
    k_t = (k * jnp.exp(-b)).astype(BF16)

    kcol = lax.broadcasted_iota(jnp.int32, (1, GLA_QK_W), 1)
    vcol = lax.broadcasted_iota(jnp.int32, (1, GLA_V_W), 1)
    ri = lax.broadcasted_iota(jnp.int32, (c, c), 0)
    ci = lax.broadcasted_iota(jnp.int32, (c, c), 1)
    causal = ri >= ci
    bd = bd_ref[...]

    for n in range(ts // c):
        rows = slice(n * c, (n + 1) * c)
        b_n = b[rows]
        b_last = b_n[c - 1:c, :]
        q_n = q_t[rows]
        k_n = k_t[rows]
        v_n = v[rows]
        k_s = (k[rows] * jnp.exp(b_last - b_n)).astype(BF16)
        q4 = jnp.concatenate(
            [jnp.where((kcol >= h * GLA_DK) & (kcol < (h + 1) * GLA_DK), q_n, jnp.zeros_like(q_n))
             for h in range(GLA_HEADS)], axis=0)
        a4 = lax.dot_general(q4, k_n, (((1,), (1,)), ((), ())), preferred_element_type=F32)
        o_n = jnp.zeros((c, GLA_V_W), F32)
        for h in range(GLA_HEADS):
            a_h = jnp.where(causal, a4[h * c:(h + 1) * c], 0.0).astype(BF16)
            o_h = jnp.dot(a_h, v_n, preferred_element_type=F32)
            o_n = jnp.where((vcol >= h * GLA_DV) & (vcol < (h + 1) * GLA_DV), o_h, o_n)
        st = st_ref[...]
        o_n = o_n + lax.dot_general(q_n, st.astype(BF16), (((1,), (1,)), ((), ())),
                                    preferred_element_type=F32)
        kv = lax.dot_general(v_n, k_s, (((0,), (0,)), ((), ())), preferred_element_type=F32)
        st_ref[...] = st * jnp.exp(b_last) + kv * bd
        oacc_ref[rows, :] = o_n

    o = oacc_ref[...]
    o2 = o * o
    inv = jnp.zeros_like(o)
    for h in range(GLA_HEADS):
        in_head = (vcol >= h * GLA_DV) & (vcol < (h + 1) * GLA_DV)
        ms = jnp.sum(jnp.where(in_head, o2, 0.0), axis=1, keepdims=True) * (1.0 / GLA_DV)
        inv = jnp.where(in_head, lax.rsqrt(ms + EPS), inv)
    og = og_ref[...].astype(F32)
    o_ref[...] = (o * inv * norm_ref[...] * (og * jax.nn.sigmoid(og))).astype(o_ref.dtype)


def _gla(proj, gk, norm, batch, seq):
    t = proj.shape[0]
    ts = GLA_TS
    ns = seq // ts
    r = np.arange(ts)
    tril = ((r[:, None] // GLA_CHUNK == r[None, :] // GLA_CHUNK) & (r[:, None] >= r[None, :]))
    tril = jnp.asarray(tril, F32)
    bd = (np.arange(GLA_V_W)[:, None] // GLA_DV) == (np.arange(GLA_QK_W)[None, :] // GLA_DK)
    bd = jnp.asarray(bd, F32)
    row = lambda b, s: b * ns + s
    return pl.pallas_call(
        _gla_body,
        out_shape=jax.ShapeDtypeStruct((t, GLA_V_W), BF16),
        grid=(batch, ns),
        in_specs=[
            pl.BlockSpec((ts, 2 * GLA_QK_W), lambda b, s: (row(b, s), _BLK_GQK)),
            pl.BlockSpec((ts, GLA_V_W), lambda b, s: (row(b, s), _BLK_GV)),
            pl.BlockSpec((ts, GLA_V_W), lambda b, s: (row(b, s), _BLK_GO)),
            pl.BlockSpec((ts, GLA_QK_W), lambda b, s: (row(b, s), 0)),
            pl.BlockSpec((1, GLA_V_W), lambda b, s: (0, 0)),
            pl.BlockSpec((ts, ts), lambda b, s: (0, 0)),
            pl.BlockSpec((GLA_V_W, GLA_QK_W), lambda b, s: (0, 0)),
        ],
        out_specs=pl.BlockSpec((ts, GLA_V_W), lambda b, s: (row(b, s), 0)),
        scratch_shapes=[pltpu.VMEM((GLA_V_W, GLA_QK_W), F32), pltpu.VMEM((ts, GLA_V_W), F32)],
        compiler_params=_cparams("parallel", "arbitrary"),
        name="gla",
    )(proj, proj, proj, gk, norm, tril, bd)


def _merge_body(h_ref, ya_ref, yp_ref, yg_ref, wga_ref, wgp_ref, wgg_ref, bga_ref, bgp_ref, bgg_ref,
                wa_ref, wp_ref, wg_ref, o_ref):
    h = h_ref[...]
    acc = None
    for y_ref, wgate_ref, bgate_ref, wbr_ref in ((ya_ref, wga_ref, bga_ref, wa_ref),
                                                 (yp_ref, wgp_ref, bgp_ref, wp_ref),
                                                 (yg_ref, wgg_ref, bgg_ref, wg_ref)):
        gate = jax.nn.sigmoid(jnp.dot(h, wgate_ref[...], preferred_element_type=F32) + bgate_ref[...])
        term = gate * jnp.dot(y_ref[...], wbr_ref[...], preferred_element_type=F32)
        acc = term if acc is None else acc + term
    o_ref[...] = acc.astype(o_ref.dtype)


def _merge(h, y_att, y_pool, y_gla, w_gate, b_gate, w_br_att, w_br_pool, w_br_gla):
    t = h.shape[0]
    tm, tn = MERGE_TM, MERGE_TN
    nn = D_MODEL // tn
    row = lambda i, j: (i, 0)
    y_spec = pl.BlockSpec((tm, ATT_Q_W), row)
    gate_specs = [pl.BlockSpec((D_MODEL, tn), functools.partial(lambda i, j, br: (0, br * nn + j), br=br))
                  for br in range(N_BRANCH)]
    bias_specs = [pl.BlockSpec((1, tn), functools.partial(lambda i, j, br: (0, br * nn + j), br=br))
                  for br in range(N_BRANCH)]
    br_spec = pl.BlockSpec((ATT_Q_W, tn), lambda i, j: (0, j))
    return pl.pallas_call(
        _merge_body,
        out_shape=jax.ShapeDtypeStruct((t, D_MODEL), BF16),
        grid=(t // tm, nn),
        in_specs=[pl.BlockSpec((tm, D_MODEL), row), y_spec, y_spec, y_spec,
                  *gate_specs, *bias_specs, br_spec, br_spec, br_spec],
        out_specs=pl.BlockSpec((tm, tn), lambda i, j: (i, j)),
        compiler_params=_cparams("parallel", "arbitrary"),
        name="gated_merge",
    )(h, y_att, y_pool, y_gla, w_gate, w_gate, w_gate, b_gate, b_gate, b_gate,
      w_br_att, w_br_pool, w_br_gla)


def _outproj_body(m_ref, w_ref, x_ref, o_ref):
    o_ref[...] = x_ref[...] + jnp.dot(m_ref[...], w_ref[...], preferred_element_type=F32)


def _outproj(merged, w_out, x):
    t = x.shape[0]
    tm, tn = OUT_TM, OUT_TN
    return pl.pallas_call(
        _outproj_body,
        out_shape=jax.ShapeDtypeStruct((t, D_MODEL), F32),
        grid=(t // tm, D_MODEL // tn),
        in_specs=[
            pl.BlockSpec((tm, D_MODEL), lambda i, j: (i, 0)),
            pl.BlockSpec((D_MODEL, tn), lambda i, j: (0, j)),
            pl.BlockSpec((tm, tn), lambda i, j: (i, j)),
        ],
        out_specs=pl.BlockSpec((tm, tn), lambda i, j: (i, j)),
        compiler_params=_cparams("parallel", "arbitrary"),
        name="outproj",
    )(merged, w_out, x)


def _mixer(x, cos_t, sin_t, batch, seq, norm_mix, w_in, b_gate, att_sinks, w_pool, pool_scale,
           w_gla_a2, b_gla_a, gla_norm, w_br_att, w_br_pool, w_br_gla, w_out):
    w_main = jnp.concatenate(
        [w_in[:, _IN_QG:_IN_VG], w_in[:, _IN_QA:_IN_KA], w_in[:, _IN_PU:_IN_QG],
         w_in[:, _IN_VG:_IN_LR], w_in[:, _IN_KA:_IN_PU]], axis=1).astype(BF16)
    w_lr = jnp.pad(w_in[:, _IN_LR:_IN_GATE], ((0, 0), (0, LANES - GLA_LOWRANK))).astype(BF16)
    w_a2 = jnp.pad(w_gla_a2, ((0, LANES - GLA_LOWRANK), (0, 0))).astype(BF16)
    w_gate = w_in[:, _IN_GATE:].astype(BF16)
    w_bd = jnp.zeros((POOL_WIDTH, POOL_WIDTH), F32)
    for gi in range(len(POOL_WINDOWS)):
        sl = slice(gi * POOL_GROUP_DIM, (gi + 1) * POOL_GROUP_DIM)
        w_bd = w_bd.at[sl, sl].set(w_pool[gi])
    w_bd = w_bd.astype(BF16)

    proj, gk, h = _inproj(x, norm_mix[None, :], w_main, w_lr, w_a2, b_gla_a[None, :])
    y_att = _attention(proj, cos_t, sin_t, att_sinks, batch, seq)
    y_pool = _pool(proj, w_bd, pool_scale[None, :], batch, seq)
    y_gla = _gla(proj, gk, gla_norm[None, :], batch, seq)
    merged = _merge(h, y_att, y_pool, y_gla, w_gate, b_gate[None, :],
                    w_br_att.astype(BF16), w_br_pool.astype(BF16), w_br_gla.astype(BF16))
    return _outproj(merged, w_out.astype(BF16), x)


def kernel(x, positions, norm_ffn1, ffn1_wi, ffn1_wo, norm_mix, w_in, b_gate, att_sinks, w_pool, pool_scale, w_gla_a2, b_gla_a, gla_norm, w_br_att, w_br_pool, w_br_gla, w_out, norm_ffn2, ffn2_wi, ffn2_wo, norm_final):
    batch, seq, d = x.shape
    depth = norm_ffn1.shape[0]
    assert d == D_MODEL and seq % max(POOL_TS, GLA_TS, ATT_BLOCK) == 0
    t = batch * seq
    assert t % max(FFN_TM, MERGE_TM, OUT_TM, ROPE_TM, INPROJ_TM) == 0
    xt = x.reshape(t, d)
    cos_t, sin_t = _rope_tables(positions)
    gf = norm_final[None, :]
    for l in range(depth):
        xt = _ffn(xt, norm_ffn1[l][None, :], ffn1_wi[l].astype(BF16), ffn1_wo[l].astype(BF16), gf, False)
        xt = _mixer(xt, cos_t, sin_t, batch, seq, norm_mix[l], w_in[l], b_gate[l], att_sinks[l],
                    w_pool[l], pool_scale[l], w_gla_a2[l], b_gla_a[l], gla_norm[l],
                    w_br_att[l], w_br_pool[l], w_br_gla[l], w_out[l])
        xt = _ffn(xt, norm_ffn2[l][None, :], ffn2_wi[l].astype(BF16), ffn2_wo[l].astype(BF16), gf,
                  l == depth - 1)
    return xt.reshape(batch, seq, d)
```

```python
import functools

import numpy as np
import jax
import jax.numpy as jnp
from jax import lax
from jax.experimental import pallas as pl
from jax.experimental.pallas import tpu as pltpu

F32 = jnp.float32
BF16 = jnp.bfloat16

D_MODEL = 2048
D_FF = 5632
ATT_HEADS = 12
ATT_KV_HEADS = 4
HEAD_DIM = 64
ATT_BLOCK = 128
ROPE_THETA = 10000.0
ATT_Q_W = ATT_HEADS * HEAD_DIM
ATT_KV_W = ATT_KV_HEADS * HEAD_DIM
POOL_WINDOWS = (2, 4, 8, 16)
POOL_GROUP_DIM = 192
POOL_WIDTH = 768
GLA_HEADS = 4
GLA_DK = 96
GLA_DV = 192
GLA_QK_W = GLA_HEADS * GLA_DK
GLA_V_W = GLA_HEADS * GLA_DV
GLA_LOWRANK = 16
GLA_TAU = 16.0
GLA_CHUNK = 64
N_BRANCH = 3
EPS = 1e-6

_IN_QA, _IN_KA, _IN_VA, _IN_PU = 0, 768, 1024, 1280
_IN_QG, _IN_KG, _IN_VG, _IN_OG = 2048, 2432, 2816, 3584
_IN_LR = 4352
_IN_GATE = 4368

PROJ_W = 4352
_BLK_GQK, _BLK_AQ, _BLK_PU, _BLK_GV, _BLK_GO = 0, 1, 2, 3, 4
_BLK_AK, _BLK_AV = 15, 16

LANES = 128
POOL_HALO = 16
VMEM_LIMIT = 56 * 1024 * 1024

FFN_TM, FFN_TF = 512, 512
INPROJ_TM = 512
INPROJ_NCH = 1024
POOL_TS = 512
GLA_TS = 256
MERGE_TM, MERGE_TN = 1024, 512
OUT_TM, OUT_TN = 1024, 1024
ROPE_TM = 1024


def _cparams(*sem):
    return pltpu.CompilerParams(dimension_semantics=sem, vmem_limit_bytes=VMEM_LIMIT)


def _rms(x, g):
    return x * lax.rsqrt(jnp.mean(x * x, axis=-1, keepdims=True) + EPS) * g


def _ffn_body(x_ref, g_ref, wa_ref, wb_ref, wo_ref, gf_ref, o_ref, h_ref, u_ref, *, nj, final_norm):
    j = pl.program_id(1)

    def up(slot):
        h = h_ref[...]
        a = jnp.dot(h, wa_ref[...], preferred_element_type=F32)
        b = jnp.dot(h, wb_ref[...], preferred_element_type=F32)
        u_ref[slot] = (a * jax.nn.sigmoid(a) * (0.5 * b)).astype(BF16)

    def down(slot):
        o_ref[...] += jnp.dot(u_ref[slot], wo_ref[...], preferred_element_type=F32)

    @pl.when(j == 0)
    def _():
        x = x_ref[...]
        h_ref[...] = _rms(x, g_ref[...]).astype(BF16)
        o_ref[...] = x
        up(0)

    @pl.when((j > 0) & (j < nj))
    def _():
        down((j - 1) % 2)
        up(j % 2)

    @pl.when(j == nj)
    def _():
        down((nj - 1) % 2)
        if final_norm:
            o_ref[...] = _rms(o_ref[...], gf_ref[...])


def _ffn(x, g, wi, wo, gf, layer, final_norm):
    t = x.shape[0]
    tm, tf = FFN_TM, FFN_TF
    nj = D_FF // tf
    up_blk = lambda j: jnp.minimum(j, nj - 1)
    return pl.pallas_call(
        functools.partial(_ffn_body, nj=nj, final_norm=final_norm),
        out_shape=jax.ShapeDtypeStruct((t, D_MODEL), F32),
        grid=(t // tm, nj + 1),
        in_specs=[
            pl.BlockSpec((tm, D_MODEL), lambda i, j: (i, 0)),
            pl.BlockSpec((1, D_MODEL), lambda i, j: (0, 0)),
            pl.BlockSpec((None, D_MODEL, tf), lambda i, j: (layer, 0, up_blk(j))),
            pl.BlockSpec((None, D_MODEL, tf), lambda i, j: (layer, 0, up_blk(j) + nj)),
            pl.BlockSpec((None, tf, D_MODEL), lambda i, j: (layer, jnp.maximum(j - 1, 0), 0)),
            pl.BlockSpec((1, D_MODEL), lambda i, j: (0, 0)),
        ],
        out_specs=pl.BlockSpec((tm, D_MODEL), lambda i, j: (i, 0)),
        scratch_shapes=[pltpu.VMEM((tm, D_MODEL), BF16), pltpu.VMEM((2, tm, tf), BF16)],
        compiler_params=_cparams("parallel", "arbitrary"),
        name="ffn_final" if final_norm else "ffn",
    )(x, g, wi, wi, wo, gf)


def _rope_table_body(pos_ref, freq_ref, sign_ref, cos_ref, sin_ref):
    ang = pos_ref[...].astype(F32) * freq_ref[...]
    cos_ref[...] = jnp.cos(ang)
    sin_ref[...] = jnp.sin(ang) * sign_ref[...]


def _rope_tables(positions):
    t = positions.size
    half = HEAD_DIM // 2
    inv_freq = ROPE_THETA ** (-jnp.arange(half, dtype=F32) / half)
    freq = jnp.tile(inv_freq, LANES // half)[None, :]
    sign = jnp.tile(jnp.concatenate([-jnp.ones((half,), F32), jnp.ones((half,), F32)]),
                    LANES // HEAD_DIM)[None, :]
    tm = ROPE_TM
    return pl.pallas_call(
        _rope_table_body,
        out_shape=(jax.ShapeDtypeStruct((t, LANES), F32), jax.ShapeDtypeStruct((t, LANES), F32)),
        grid=(t // tm,),
        in_specs=[
            pl.BlockSpec((tm, 1), lambda i: (i, 0)),
            pl.BlockSpec((1, LANES), lambda i: (0, 0)),
            pl.BlockSpec((1, LANES), lambda i: (0, 0)),
        ],
        out_specs=(pl.BlockSpec((tm, LANES), lambda i: (i, 0)),
                   pl.BlockSpec((tm, LANES), lambda i: (i, 0))),
        compiler_params=_cparams("parallel"),
        name="rope_tables",
    )(positions.reshape(t, 1), freq, sign)


def _inproj_body(x_ref, g_ref, w_ref, wlr_ref, wa2_ref, ba_ref, proj_ref, gk_ref, h_ref):
    h = _rms(x_ref[...], g_ref[...]).astype(BF16)
    h_ref[...] = h
    for c0 in range(0, PROJ_W, INPROJ_NCH):
        c1 = min(c0 + INPROJ_NCH, PROJ_W)
        proj_ref[:, c0:c1] = jnp.dot(h, w_ref[:, c0:c1],
                                     preferred_element_type=F32).astype(proj_ref.dtype)
    lr = jnp.dot(h, wlr_ref[...], preferred_element_type=F32)
    z = jnp.dot(lr.astype(BF16), wa2_ref[...], preferred_element_type=F32) + ba_ref[...]
    gk_ref[...] = (jnp.minimum(z, 0.0) - jnp.log1p(jnp.exp(-jnp.abs(z)))) * (1.0 / GLA_TAU)


def _inproj(x, g, w_main, w_lr, w_a2, b_a, layer):
    t = x.shape[0]
    tm = INPROJ_TM
    const = lambda i: (0, 0)
    lconst = lambda i: (layer, 0, 0)
    return pl.pallas_call(
        _inproj_body,
        out_shape=(jax.ShapeDtypeStruct((t, PROJ_W), BF16),
                   jax.ShapeDtypeStruct((t, GLA_QK_W), F32),
                   jax.ShapeDtypeStruct((t, D_MODEL), BF16)),
        grid=(t // tm,),
        in_specs=[
            pl.BlockSpec((tm, D_MODEL), lambda i: (i, 0)),
            pl.BlockSpec((1, D_MODEL), const),
            pl.BlockSpec((None, D_MODEL, PROJ_W), lconst, pipeline_mode=pl.Buffered(1)),
            pl.BlockSpec((None, D_MODEL, LANES), lconst),
            pl.BlockSpec((None, LANES, GLA_QK_W), lconst),
            pl.BlockSpec((1, GLA_QK_W), const),
        ],
        out_specs=(pl.BlockSpec((tm, PROJ_W), lambda i: (i, 0)),
                   pl.BlockSpec((tm, GLA_QK_W), lambda i: (i, 0)),
                   pl.BlockSpec((tm, D_MODEL), lambda i: (i, 0))),
        compiler_params=_cparams("parallel"),
        name="inproj",
    )(x, g, w_main, w_lr, w_a2, b_a)


def _swap_halves(x):
    lane = lax.broadcasted_iota(jnp.int32, x.shape, 1)
    first = (lane % HEAD_DIM) < (HEAD_DIM // 2)
    return jnp.where(first, pltpu.roll(x, LANES - HEAD_DIM // 2, 1), pltpu.roll(x, HEAD_DIM // 2, 1))


def _rope(x, cos, sin):
    cols = []
    for c in range(x.shape[1] // LANES):
        xc = x[:, c * LANES:(c + 1) * LANES]
        cols.append(xc * cos + _swap_halves(xc) * sin)
    return jnp.concatenate(cols, axis=1)


def _attn_body(sink_ref, q_ref, kc_ref, kp_ref, vc_ref, vp_ref, cc_ref, sc_ref, cp_ref, sp_ref, o_ref):
    n = pl.program_id(1)
    blk = ATT_BLOCK
    q = _rope(q_ref[...].astype(F32), cc_ref[...], sc_ref[...]).astype(BF16)
    k = jnp.concatenate([_rope(kp_ref[...].astype(F32), cp_ref[...], sp_ref[...]),
                         _rope(kc_ref[...].astype(F32), cc_ref[...], sc_ref[...])], axis=0).astype(BF16)
    v = jnp.concatenate([vp_ref[...], vc_ref[...]], axis=0)

    qi = lax.broadcasted_iota(jnp.int32, (blk, 2 * blk), 0)
    si = lax.broadcasted_iota(jnp.int32, (blk, 2 * blk), 1)
    rel = blk + qi - si
    mask = (rel >= 0) & (rel < blk) & ((si >= blk) | (n > 0))

    group = ATT_HEADS // ATT_KV_HEADS
    outs = []
    for hd in range(ATT_HEADS):
        g = hd // group
        qh = q[:, hd * HEAD_DIM:(hd + 1) * HEAD_DIM]
        kg = k[:, g * HEAD_DIM:(g + 1) * HEAD_DIM]
        vg = v[:, g * HEAD_DIM:(g + 1) * HEAD_DIM]
        s = lax.dot_general(qh, kg, (((1,), (1,)), ((), ())),
                            preferred_element_type=F32) * (HEAD_DIM ** -0.5)
        s = jnp.where(mask, s, -1e30)
        sink = sink_ref[hd]
        m = jnp.maximum(jnp.max(s, axis=1, keepdims=True), sink)
        e = jnp.exp(s - m)
        denom = jnp.sum(e, axis=1, keepdims=True) + jnp.exp(sink - m)
        o = jnp.dot(e.astype(BF16), vg, preferred_element_type=F32)
        outs.append(o / denom)
    o_ref[...] = jnp.concatenate(outs, axis=1).astype(o_ref.dtype)


def _attention(proj, cos_t, sin_t, sinks, batch, seq):
    t = proj.shape[0]
    blk = ATT_BLOCK
    nb = seq // blk
    cur = lambda b, n: b * nb + n
    prev = lambda b, n: b * nb + jnp.maximum(n - 1, 0)
    return pl.pallas_call(
        _attn_body,
        out_shape=jax.ShapeDtypeStruct((t, ATT_Q_W), BF16),
        grid=(batch, nb),
        in_specs=[
            pl.BlockSpec(memory_space=pltpu.SMEM),
            pl.BlockSpec((blk, ATT_Q_W), lambda b, n: (cur(b, n), _BLK_AQ)),
            pl.BlockSpec((blk, ATT_KV_W), lambda b, n: (cur(b, n), _BLK_AK)),
            pl.BlockSpec((blk, ATT_KV_W), lambda b, n: (prev(b, n), _BLK_AK)),
            pl.BlockSpec((blk, ATT_KV_W), lambda b, n: (cur(b, n), _BLK_AV)),
            pl.BlockSpec((blk, ATT_KV_W), lambda b, n: (prev(b, n), _BLK_AV)),
            pl.BlockSpec((blk, LANES), lambda b, n: (cur(b, n), 0)),
            pl.BlockSpec((blk, LANES), lambda b, n: (cur(b, n), 0)),
            pl.BlockSpec((blk, LANES), lambda b, n: (prev(b, n), 0)),
            pl.BlockSpec((blk, LANES), lambda b, n: (prev(b, n), 0)),
        ],
        out_specs=pl.BlockSpec((blk, ATT_Q_W), lambda b, n: (cur(b, n), 0)),
        compiler_params=_cparams("parallel", "arbitrary"),
        name="swa_attention",
    )(sinks, proj, proj, proj, proj, proj, cos_t, sin_t, cos_t, sin_t)


def _pool_body(u_ref, halo_ref, w_ref, scale_ref, o_ref):
    s_idx = pl.program_id(1)
    ts = u_ref.shape[0]
    u = u_ref[...].astype(F32)
    halo = jnp.where(s_idx > 0, halo_ref[...].astype(F32), 0.0)
    a1 = jnp.concatenate([halo, u], axis=0)
    length = ts + POOL_HALO
    a2 = a1[1:] + a1[:length - 1]
    a4 = a2[2:] + a2[:length - 3]
    a8 = a4[4:] + a4[:length - 7]
    a16 = a8[8:] + a8[:length - 15]
    sums = {2: a2[POOL_HALO - 1:], 4: a4[POOL_HALO - 3:], 8: a8[POOL_HALO - 7:], 16: a16[POOL_HALO - 15:]}

    tok = s_idx * ts + lax.broadcasted_iota(jnp.int32, (ts, 1), 0)
    chan = lax.broadcasted_iota(jnp.int32, (1, POOL_WIDTH), 1)
    d = jnp.zeros((ts, POOL_WIDTH), F32)
    for gi, w in enumerate(POOL_WINDOWS):
        inv_cnt = 1.0 / jnp.minimum(tok + 1, w).astype(F32)
        in_group = (chan >= gi * POOL_GROUP_DIM) & (chan < (gi + 1) * POOL_GROUP_DIM)
        d = jnp.where(in_group, sums[w] * inv_cnt, d)
    d = d - u
    y = jnp.dot(d.astype(BF16), w_ref[...], preferred_element_type=F32)
    o_ref[...] = (y * scale_ref[...]).astype(o_ref.dtype)


def _pool(proj, w_bd, scale, batch, seq, layer):
    t = proj.shape[0]
    ts = POOL_TS
    ns = seq // ts
    return pl.pallas_call(
        _pool_body,
        out_shape=jax.ShapeDtypeStruct((t, POOL_WIDTH), BF16),
        grid=(batch, ns),
        in_specs=[
            pl.BlockSpec((ts, POOL_WIDTH), lambda b, s: (b * ns + s, _BLK_PU)),
            pl.BlockSpec((POOL_HALO, POOL_WIDTH),
                         lambda b, s: (jnp.maximum((b * ns + s) * (ts // POOL_HALO) - 1, 0), _BLK_PU)),
            pl.BlockSpec((None, POOL_WIDTH, POOL_WIDTH), lambda b, s: (layer, 0, 0)),
            pl.BlockSpec((1, POOL_WIDTH), lambda b, s: (0, 0)),
        ],
        out_specs=pl.BlockSpec((ts, POOL_WIDTH), lambda b, s: (b * ns + s, 0)),
        compiler_params=_cparams("parallel", "arbitrary"),
        name="pool_mixer",
    )(proj, proj, w_bd, scale)


def _gla_body(qk_ref, v_ref, og_ref, gk_ref, norm_ref, tril_ref, bd_ref, o_ref, st_ref, oacc_ref):
    s_idx = pl.program_id(1)
    ts = qk_ref.shape[0]
    c = GLA_CHUNK

    @pl.when(s_idx == 0)
    def _():
        st_ref[...] = jnp.zeros_like(st_ref)

    qk = qk_ref[...].astype(F32)
    q = qk[:, :GLA_QK_W] * (GLA_DK ** -0.5)
    k = qk[:, GLA_QK_W:]
    v = v_ref[...]
    b = jnp.dot(tril_ref[...], gk_ref[...], preferred_element_type=F32,
                precision=lax.Precision.HIGHEST)
    q_t = (q * jnp.exp(b)).astype(BF16)
    k_t = (k * jnp.exp(-b)).astype(BF16)

    kcol = lax.broadcasted_iota(jnp.int32, (1, GLA_QK_W), 1)
    vcol = lax.broadcasted_iota(jnp.int32, (1, GLA_V_W), 1)
    ri = lax.broadcasted_iota(jnp.int32, (c, c), 0)
    ci = lax.broadcasted_iota(jnp.int32, (c, c), 1)
    causal = ri >= ci
    bd = bd_ref[...]

    for n in range(ts // c):
        rows = slice(n * c, (n + 1) * c)
        b_n = b[rows]
        b_last = b_n[c - 1:c, :]
        q_n = q_t[rows]
        k_n = k_t[rows]
        v_n = v[rows]
        k_s = (k[rows] * jnp.exp(b_last - b_n)).astype(BF16)
        q4 = jnp.concatenate(
            [jnp.where((kcol >= h * GLA_DK) & (kcol < (h + 1) * GLA_DK), q_n, jnp.zeros_like(q_n))
             for h in range(GLA_HEADS)], axis=0)
        a4 = lax.dot_general(q4, k_n, (((1,), (1,)), ((), ())), preferred_element_type=F32)
        o_n = jnp.zeros((c, GLA_V_W), F32)
        for h in range(GLA_HEADS):
            a_h = jnp.where(causal, a4[h * c:(h + 1) * c], 0.0).astype(BF16)
            o_h = jnp.dot(a_h, v_n, preferred_element_type=F32)
            o_n = jnp.where((vcol >= h * GLA_DV) & (vcol < (h + 1) * GLA_DV), o_h, o_n)
        st = st_ref[...]
        o_n = o_n + lax.dot_general(q_n, st.astype(BF16), (((1,), (1,)), ((), ())),
                                    preferred_element_type=F32)
        kv = lax.dot_general(v_n, k_s, (((0,), (0,)), ((), ())), preferred_element_type=F32)
        st_ref[...] = st * jnp.exp(b_last) + kv * bd
        oacc_ref[rows, :] = o_n

    o = oacc_ref[...]
    o2 = o * o
    inv = jnp.zeros_like(o)
    for h in range(GLA_HEADS):
        in_head = (vcol >= h * GLA_DV) & (vcol < (h + 1) * GLA_DV)
        ms = jnp.sum(jnp.where(in_head, o2, 0.0), axis=1, keepdims=True) * (1.0 / GLA_DV)
        inv = jnp.where(in_head, lax.rsqrt(ms + EPS), inv)
    og = og_ref[...].astype(F32)
    o_ref[...] = (o * inv * norm_ref[...] * (og * jax.nn.sigmoid(og))).astype(o_ref.dtype)


def _gla(proj, gk, norm, batch, seq):
    t = proj.shape[0]
    ts = GLA_TS
    ns = seq // ts
    r = np.arange(ts)
    tril = ((r[:, None] // GLA_CHUNK == r[None, :] // GLA_CHUNK) & (r[:, None] >= r[None, :]))
    tril = jnp.asarray(tril, F32)
    bd = (np.arange(GLA_V_W)[:, None] // GLA_DV) == (np.arange(GLA_QK_W)[None, :] // GLA_DK)
    bd = jnp.asarray(bd, F32)
    row = lambda b, s: b * ns + s
    return pl.pallas_call(
        _gla_body,
        out_shape=jax.ShapeDtypeStruct((t, GLA_V_W), BF16),
        grid=(batch, ns),
        in_specs=[
            pl.BlockSpec((ts, 2 * GLA_QK_W), lambda b, s: (row(b, s), _BLK_GQK)),
            pl.BlockSpec((ts, GLA_V_W), lambda b, s: (row(b, s), _BLK_GV)),
            pl.BlockSpec((ts, GLA_V_W), lambda b, s: (row(b, s), _BLK_GO)),
            pl.BlockSpec((ts, GLA_QK_W), lambda b, s: (row(b, s), 0)),
            pl.BlockSpec((1, GLA_V_W), lambda b, s: (0, 0)),
            pl.BlockSpec((ts, ts), lambda b, s: (0, 0)),
            pl.BlockSpec((GLA_V_W, GLA_QK_W), lambda b, s: (0, 0)),
        ],
        out_specs=pl.BlockSpec((ts, GLA_V_W), lambda b, s: (row(b, s), 0)),
        scratch_shapes=[pltpu.VMEM((GLA_V_W, GLA_QK_W), F32), pltpu.VMEM((ts, GLA_V_W), F32)],
        compiler_params=_cparams("parallel", "arbitrary"),
        name="gla",
    )(proj, proj, proj, gk, norm, tril, bd)


def _merge_body(h_ref, ya_ref, yp_ref, yg_ref, wga_ref, wgp_ref, wgg_ref, bga_ref, bgp_ref, bgg_ref,
                wa_ref, wp_ref, wg_ref, o_ref):
    h = h_ref[...]
    acc = None
    for y_ref, wgate_ref, bgate_ref, wbr_ref in ((ya_ref, wga_ref, bga_ref, wa_ref),
                                                 (yp_ref, wgp_ref, bgp_ref, wp_ref),
                                                 (yg_ref, wgg_ref, bgg_ref, wg_ref)):
        gate = jax.nn.sigmoid(jnp.dot(h, wgate_ref[...], preferred_element_type=F32) + bgate_ref[...])
        term = gate * jnp.dot(y_ref[...], wbr_ref[...], preferred_element_type=F32)
        acc = term if acc is None else acc + term
    o_ref[...] = acc.astype(o_ref.dtype)


def _merge(h, y_att, y_pool, y_gla, w_gate, b_gate, w_br_att, w_br_pool, w_br_gla, layer):
    t = h.shape[0]
    tm, tn = MERGE_TM, MERGE_TN
    nn = D_MODEL // tn
    row = lambda i, j: (i, 0)
    y_spec = pl.BlockSpec((tm, ATT_Q_W), row)
    gate_specs = [pl.BlockSpec((None, D_MODEL, tn),
                               functools.partial(lambda i, j, br: (layer, 0, br * nn + j), br=br))
                  for br in range(N_BRANCH)]
    bias_specs = [pl.BlockSpec((1, tn), functools.partial(lambda i, j, br: (0, br * nn + j), br=br))
                  for br in range(N_BRANCH)]
    br_spec = pl.BlockSpec((None, ATT_Q_W, tn), lambda i, j: (layer, 0, j))
    return pl.pallas_call(
        _merge_body,
        out_shape=jax.ShapeDtypeStruct((t, D_MODEL), BF16),
        grid=(t // tm, nn),
        in_specs=[pl.BlockSpec((tm, D_MODEL), row), y_spec, y_spec, y_spec,
                  *gate_specs, *bias_specs, br_spec, br_spec, br_spec],
        out_specs=pl.BlockSpec((tm, tn), lambda i, j: (i, j)),
        compiler_params=_cparams("parallel", "arbitrary"),
        name="gated_merge",
    )(h, y_att, y_pool, y_gla, w_gate, w_gate, w_gate, b_gate, b_gate, b_gate,
      w_br_att, w_br_pool, w_br_gla)


def _outproj_body(m_ref, w_ref, x_ref, o_ref):
    o_ref[...] = x_ref[...] + jnp.dot(m_ref[...], w_ref[...], preferred_element_type=F32)


def _outproj(merged, w_out, x, layer):
    t = x.shape[0]
    tm, tn = OUT_TM, OUT_TN
    return pl.pallas_call(
        _outproj_body,
        out_shape=jax.ShapeDtypeStruct((t, D_MODEL), F32),
        grid=(t // tm, D_MODEL // tn),
        in_specs=[
            pl.BlockSpec((tm, D_MODEL), lambda i, j: (i, 0)),
            pl.BlockSpec((None, D_MODEL, tn), lambda i, j: (layer, 0, j)),
            pl.BlockSpec((tm, tn), lambda i, j: (i, j)),
        ],
        out_specs=pl.BlockSpec((tm, tn), lambda i, j: (i, j)),
        compiler_params=_cparams("parallel", "arbitrary"),
        name="outproj",
    )(merged, w_out, x)


def _prep_mixer_weights(w_in, w_pool, w_gla_a2, w_br_att, w_br_pool, w_br_gla, w_out):
    depth = w_in.shape[0]
    w_main = jnp.concatenate(
        [w_in[..., _IN_QG:_IN_VG], w_in[..., _IN_QA:_IN_KA], w_in[..., _IN_PU:_IN_QG],
         w_in[..., _IN_VG:_IN_LR], w_in[..., _IN_KA:_IN_PU]], axis=-1).astype(BF16)
    w_lr = jnp.pad(w_in[..., _IN_LR:_IN_GATE].astype(BF16),
                   ((0, 0), (0, 0), (0, LANES - GLA_LOWRANK)))
    w_a2 = jnp.pad(w_gla_a2.astype(BF16), ((0, 0), (0, LANES - GLA_LOWRANK), (0, 0)))
    w_gate = w_in[..., _IN_GATE:].astype(BF16)
    w_bd = jnp.zeros((depth, POOL_WIDTH, POOL_WIDTH), BF16)
    for gi in range(len(POOL_WINDOWS)):
        sl = slice(gi * POOL_GROUP_DIM, (gi + 1) * POOL_GROUP_DIM)
        w_bd = w_bd.at[:, sl, sl].set(w_pool[:, gi].astype(BF16))
    return dict(w_main=w_main, w_lr=w_lr, w_a2=w_a2, w_gate=w_gate, w_bd=w_bd,
                w_br_att=w_br_att.astype(BF16), w_br_pool=w_br_pool.astype(BF16),
                w_br_gla=w_br_gla.astype(BF16), w_out=w_out.astype(BF16))


def _mixer(x, cos_t, sin_t, batch, seq, layer, mw, norm_mix, b_gate, att_sinks, pool_scale,
           b_gla_a, gla_norm):
    proj, gk, h = _inproj(x, norm_mix[None, :], mw["w_main"], mw["w_lr"], mw["w_a2"],
                          b_gla_a[None, :], layer)
    y_att = _attention(proj, cos_t, sin_t, att_sinks, batch, seq)
    y_pool = _pool(proj, mw["w_bd"], pool_scale[None, :], batch, seq, layer)
    y_gla = _gla(proj, gk, gla_norm[None, :], batch, seq)
    merged = _merge(h, y_att, y_pool, y_gla, mw["w_gate"], b_gate[None, :],
                    mw["w_br_att"], mw["w_br_pool"], mw["w_br_gla"], layer)
    return _outproj(merged, mw["w_out"], x, layer)


def kernel(x, positions, norm_ffn1, ffn1_wi, ffn1_wo, norm_mix, w_in, b_gate, att_sinks, w_pool, pool_scale, w_gla_a2, b_gla_a, gla_norm, w_br_att, w_br_pool, w_br_gla, w_out, norm_ffn2, ffn2_wi, ffn2_wo, norm_final):
    batch, seq, d = x.shape
    depth = norm_ffn1.shape[0]
    assert d == D_MODEL and seq % max(POOL_TS, GLA_TS, ATT_BLOCK) == 0
    t = batch * seq
    assert t % max(FFN_TM, MERGE_TM, OUT_TM, ROPE_TM, INPROJ_TM) == 0
    xt = x.reshape(t, d)
    cos_t, sin_t = _rope_tables(positions)
    gf = norm_final[None, :]
    wi1, wo1 = ffn1_wi.astype(BF16), ffn1_wo.astype(BF16)
    wi2, wo2 = ffn2_wi.astype(BF16), ffn2_wo.astype(BF16)
    mw = _prep_mixer_weights(w_in, w_pool, w_gla_a2, w_br_att, w_br_pool, w_br_gla, w_out)
    for l in range(depth):
        xt = _ffn(xt, norm_ffn1[l][None, :], wi1, wo1, gf, l, False)
        xt = _mixer(xt, cos_t, sin_t, batch, seq, l, mw, norm_mix[l], b_gate[l], att_sinks[l],
                    pool_scale[l], b_gla_a[l], gla_norm[l])
        xt = _ffn(xt, norm_ffn2[l][None, :], wi2, wo2, gf, l, l == depth - 1)
    return xt.reshape(batch, seq, d)
```

```python
import functools

import numpy as np
import jax
import jax.numpy as jnp
from jax import lax
from jax.experimental import pallas as pl
from jax.experimental.pallas import tpu as pltpu

F32 = jnp.float32
BF16 = jnp.bfloat16

D_MODEL = 2048
D_FF = 5632
ATT_HEADS = 12
ATT_KV_HEADS = 4
HEAD_DIM = 64
ATT_BLOCK = 128
ROPE_THETA = 10000.0
ATT_Q_W = ATT_HEADS * HEAD_DIM
ATT_KV_W = ATT_KV_HEADS * HEAD_DIM
POOL_WINDOWS = (2, 4, 8, 16)
POOL_GROUP_DIM = 192
POOL_WIDTH = 768
GLA_HEADS = 4
GLA_DK = 96
GLA_DV = 192
GLA_QK_W = GLA_HEADS * GLA_DK
GLA_V_W = GLA_HEADS * GLA_DV
GLA_LOWRANK = 16
GLA_TAU = 16.0
GLA_CHUNK = 64
N_BRANCH = 3
EPS = 1e-6

_IN_QA, _IN_KA, _IN_VA, _IN_PU = 0, 768, 1024, 1280
_IN_QG, _IN_KG, _IN_VG, _IN_OG = 2048, 2432, 2816, 3584
_IN_LR = 4352
_IN_GATE = 4368

PROJ_W = 4352
_BLK_GQK, _BLK_AQ, _BLK_PU, _BLK_GV, _BLK_GO = 0, 1, 2, 3, 4
_BLK_AK, _BLK_AV = 15, 16
_PROJ_SEGMENTS = ((0, _IN_QG, 768), (768, _IN_QA, 768), (1536, _IN_PU, 768),
                  (2304, _IN_VG, 1536), (3840, _IN_KA, 512))
W_IN_COLS = _IN_GATE + N_BRANCH * D_MODEL
GATE_LEAD = _IN_GATE % 128
W_IN_PAD = -(-W_IN_COLS // 128) * 128

LANES = 128
POOL_HALO = 16
VMEM_LIMIT = 58 * 1024 * 1024

FFN_TM, FFN_TF = 1024, 512
INPROJ_TM = 512
INPROJ_NCH = 1024
POOL_TS = 512
GLA_TS = 256
MERGE_TM, MERGE_TN = 1024, 512
OUT_TM, OUT_TN = 1024, 1024
ROPE_TM = 1024


def _cparams(*sem):
    return pltpu.CompilerParams(dimension_semantics=sem, vmem_limit_bytes=VMEM_LIMIT)


def _rms(x, g):
    return x * lax.rsqrt(jnp.mean(x * x, axis=-1, keepdims=True) + EPS) * g


def _ffn_body(x_ref, g_ref, wa_ref, wb_ref, wo_ref, gf_ref, o_ref, h_ref, u_ref, *, nj, final_norm):
    j = pl.program_id(1)

    def up(slot):
        h = h_ref[...]
        a = jnp.dot(h, wa_ref[...], preferred_element_type=F32)
        b = jnp.dot(h, wb_ref[...], preferred_element_type=F32)
        u_ref[slot] = (a * jax.nn.sigmoid(a) * (0.5 * b)).astype(BF16)

    def down(slot):
        o_ref[...] += jnp.dot(u_ref[slot], wo_ref[...], preferred_element_type=F32)

    @pl.when(j == 0)
    def _():
        x = x_ref[...]
        h_ref[...] = _rms(x, g_ref[...]).astype(BF16)
        o_ref[...] = x
        up(0)

    @pl.when((j > 0) & (j < nj))
    def _():
        down((j - 1) % 2)
        up(j % 2)

    @pl.when(j == nj)
    def _():
        down((nj - 1) % 2)
        if final_norm:
            o_ref[...] = _rms(o_ref[...], gf_ref[...])


def _ffn(x, g, wi, wo, gf, layer, final_norm):
    t = x.shape[0]
    tm, tf = FFN_TM, FFN_TF
    nj = D_FF // tf
    up_blk = lambda j: jnp.minimum(j, nj - 1)
    return pl.pallas_call(
        functools.partial(_ffn_body, nj=nj, final_norm=final_norm),
        out_shape=jax.ShapeDtypeStruct((t, D_MODEL), F32),
        grid=(t // tm, nj + 1),
        in_specs=[
            pl.BlockSpec((tm, D_MODEL), lambda i, j: (i, 0)),
            pl.BlockSpec((1, D_MODEL), lambda i, j: (0, 0)),
            pl.BlockSpec((None, D_MODEL, tf), lambda i, j: (layer, 0, up_blk(j))),
            pl.BlockSpec((None, D_MODEL, tf), lambda i, j: (layer, 0, up_blk(j) + nj)),
            pl.BlockSpec((None, tf, D_MODEL), lambda i, j: (layer, jnp.maximum(j - 1, 0), 0)),
            pl.BlockSpec((1, D_MODEL), lambda i, j: (0, 0)),
        ],
        out_specs=pl.BlockSpec((tm, D_MODEL), lambda i, j: (i, 0)),
        scratch_shapes=[pltpu.VMEM((tm, D_MODEL), BF16), pltpu.VMEM((2, tm, tf), BF16)],
        compiler_params=_cparams("parallel", "arbitrary"),
        name="ffn_final" if final_norm else "ffn",
    )(x, g, wi, wi, wo, gf)


def _rope_table_body(pos_ref, freq_ref, sign_ref, cos_ref, sin_ref):
    ang = pos_ref[...].astype(F32) * freq_ref[...]
    cos_ref[...] = jnp.cos(ang)
    sin_ref[...] = jnp.sin(ang) * sign_ref[...]


def _rope_tables(positions):
    t = positions.size
    half = HEAD_DIM // 2
    inv_freq = ROPE_THETA ** (-jnp.arange(half, dtype=F32) / half)
    freq = jnp.tile(inv_freq, LANES // half)[None, :]
    sign = jnp.tile(jnp.concatenate([-jnp.ones((half,), F32), jnp.ones((half,), F32)]),
                    LANES // HEAD_DIM)[None, :]
    tm = ROPE_TM
    return pl.pallas_call(
        _rope_table_body,
        out_shape=(jax.ShapeDtypeStruct((t, LANES), F32), jax.ShapeDtypeStruct((t, LANES), F32)),
        grid=(t // tm,),
        in_specs=[
            pl.BlockSpec((tm, 1), lambda i: (i, 0)),
            pl.BlockSpec((1, LANES), lambda i: (0, 0)),
            pl.BlockSpec((1, LANES), lambda i: (0, 0)),
        ],
        out_specs=(pl.BlockSpec((tm, LANES), lambda i: (i, 0)),
                   pl.BlockSpec((tm, LANES), lambda i: (i, 0))),
        compiler_params=_cparams("parallel"),
        name="rope_tables",
    )(positions.reshape(t, 1), freq, sign)


def _inproj_body(x_ref, g_ref, w_ref, wlr_ref, wa2_ref, ba_ref, proj_ref, gk_ref, h_ref):
    h = _rms(x_ref[...], g_ref[...]).astype(BF16)
    h_ref[...] = h
    for dst, src, width in _PROJ_SEGMENTS:
        proj_ref[:, dst:dst + width] = jnp.dot(h, w_ref[:, src:src + width],
                                               preferred_element_type=F32).astype(proj_ref.dtype)
    lr = jnp.dot(h, wlr_ref[...], preferred_element_type=F32)
    z = jnp.dot(lr.astype(BF16), wa2_ref[...], preferred_element_type=F32) + ba_ref[...]
    gk_ref[...] = (jnp.minimum(z, 0.0) - jnp.log1p(jnp.exp(-jnp.abs(z)))) * (1.0 / GLA_TAU)


def _inproj(x, g, w_in, w_a2, b_a, layer):
    t = x.shape[0]
    tm = INPROJ_TM
    const = lambda i: (0, 0)
    lconst = lambda i: (layer, 0, 0)
    return pl.pallas_call(
        _inproj_body,
        out_shape=(jax.ShapeDtypeStruct((t, PROJ_W), BF16),
                   jax.ShapeDtypeStruct((t, GLA_QK_W), F32),
                   jax.ShapeDtypeStruct((t, D_MODEL), BF16)),
        grid=(t // tm,),
        in_specs=[
            pl.BlockSpec((tm, D_MODEL), lambda i: (i, 0)),
            pl.BlockSpec((1, D_MODEL), const),
            pl.BlockSpec((None, D_MODEL, PROJ_W), lconst, pipeline_mode=pl.Buffered(1)),
            pl.BlockSpec((None, D_MODEL, LANES), lambda i: (layer, 0, _IN_LR // LANES)),
            pl.BlockSpec((None, LANES, GLA_QK_W), lconst),
            pl.BlockSpec((1, GLA_QK_W), const),
        ],
        out_specs=(pl.BlockSpec((tm, PROJ_W), lambda i: (i, 0)),
                   pl.BlockSpec((tm, GLA_QK_W), lambda i: (i, 0)),
                   pl.BlockSpec((tm, D_MODEL), lambda i: (i, 0))),
        compiler_params=_cparams("parallel"),
        name="inproj",
    )(x, g, w_in, w_in, w_a2, b_a)


def _swap_halves(x):
    lane = lax.broadcasted_iota(jnp.int32, x.shape, 1)
    first = (lane % HEAD_DIM) < (HEAD_DIM // 2)
    return jnp.where(first, pltpu.roll(x, LANES - HEAD_DIM // 2, 1), pltpu.roll(x, HEAD_DIM // 2, 1))


def _rope(x, cos, sin):
    cols = []
    for c in range(x.shape[1] // LANES):
        xc = x[:, c * LANES:(c + 1) * LANES]
        cols.append(xc * cos + _swap_halves(xc) * sin)
    return jnp.concatenate(cols, axis=1)


def _attn_body(sink_ref, q_ref, kc_ref, kp_ref, vc_ref, vp_ref, cc_ref, sc_ref, cp_ref, sp_ref, o_ref):
    n = pl.program_id(1)
    blk = ATT_BLOCK
    q = _rope(q_ref[...].astype(F32), cc_ref[...], sc_ref[...]).astype(BF16)
    k = jnp.concatenate([_rope(kp_ref[...].astype(F32), cp_ref[...], sp_ref[...]),
                         _rope(kc_ref[...].astype(F32), cc_ref[...], sc_ref[...])], axis=0).astype(BF16)
    v = jnp.concatenate([vp_ref[...], vc_ref[...]], axis=0)

    qi = lax.broadcasted_iota(jnp.int32, (blk, 2 * blk), 0)
    si = lax.broadcasted_iota(jnp.int32, (blk, 2 * blk), 1)
    rel = blk + qi - si
    mask = (rel >= 0) & (rel < blk) & ((si >= blk) | (n > 0))

    group = ATT_HEADS // ATT_KV_HEADS
    outs = []
    for hd in range(ATT_HEADS):
        g = hd // group
        qh = q[:, hd * HEAD_DIM:(hd + 1) * HEAD_DIM]
        kg = k[:, g * HEAD_DIM:(g + 1) * HEAD_DIM]
        vg = v[:, g * HEAD_DIM:(g + 1) * HEAD_DIM]
        s = lax.dot_general(qh, kg, (((1,), (1,)), ((), ())),
                            preferred_element_type=F32) * (HEAD_DIM ** -0.5)
        s = jnp.where(mask, s, -1e30)
        sink = sink_ref[hd]
        m = jnp.maximum(jnp.max(s, axis=1, keepdims=True), sink)
        e = jnp.exp(s - m)
        denom = jnp.sum(e, axis=1, keepdims=True) + jnp.exp(sink - m)
        o = jnp.dot(e.astype(BF16), vg, preferred_element_type=F32)
        outs.append(o / denom)
    o_ref[...] = jnp.concatenate(outs, axis=1).astype(o_ref.dtype)


def _attention(proj, cos_t, sin_t, sinks, batch, seq):
    t = proj.shape[0]
    blk = ATT_BLOCK
    nb = seq // blk
    cur = lambda b, n: b * nb + n
    prev = lambda b, n: b * nb + jnp.maximum(n - 1, 0)
    return pl.pallas_call(
        _attn_body,
        out_shape=jax.ShapeDtypeStruct((t, ATT_Q_W), BF16),
        grid=(batch, nb),
        in_specs=[
            pl.BlockSpec(memory_space=pltpu.SMEM),
            pl.BlockSpec((blk, ATT_Q_W), lambda b, n: (cur(b, n), _BLK_AQ)),
            pl.BlockSpec((blk, ATT_KV_W), lambda b, n: (cur(b, n), _BLK_AK)),
            pl.BlockSpec((blk, ATT_KV_W), lambda b, n: (prev(b, n), _BLK_AK)),
            pl.BlockSpec((blk, ATT_KV_W), lambda b, n: (cur(b, n), _BLK_AV)),
            pl.BlockSpec((blk, ATT_KV_W), lambda b, n: (prev(b, n), _BLK_AV)),
            pl.BlockSpec((blk, LANES), lambda b, n: (cur(b, n), 0)),
            pl.BlockSpec((blk, LANES), lambda b, n: (cur(b, n), 0)),
            pl.BlockSpec((blk, LANES), lambda b, n: (prev(b, n), 0)),
            pl.BlockSpec((blk, LANES), lambda b, n: (prev(b, n), 0)),
        ],
        out_specs=pl.BlockSpec((blk, ATT_Q_W), lambda b, n: (cur(b, n), 0)),
        compiler_params=_cparams("parallel", "arbitrary"),
        name="swa_attention",
    )(sinks, proj, proj, proj, proj, proj, cos_t, sin_t, cos_t, sin_t)


def _pool_body(u_ref, halo_ref, w_ref, scale_ref, o_ref):
    s_idx = pl.program_id(1)
    ts = u_ref.shape[0]
    u = u_ref[...].astype(F32)
    halo = jnp.where(s_idx > 0, halo_ref[...].astype(F32), 0.0)
    a1 = jnp.concatenate([halo, u], axis=0)
    length = ts + POOL_HALO
    a2 = a1[1:] + a1[:length - 1]
    a4 = a2[2:] + a2[:length - 3]
    a8 = a4[4:] + a4[:length - 7]
    a16 = a8[8:] + a8[:length - 15]
    sums = {2: a2[POOL_HALO - 1:], 4: a4[POOL_HALO - 3:], 8: a8[POOL_HALO - 7:], 16: a16[POOL_HALO - 15:]}

    tok = s_idx * ts + lax.broadcasted_iota(jnp.int32, (ts, 1), 0)
    chan = lax.broadcasted_iota(jnp.int32, (1, POOL_WIDTH), 1)
    d = jnp.zeros((ts, POOL_WIDTH), F32)
    for gi, w in enumerate(POOL_WINDOWS):
        inv_cnt = 1.0 / jnp.minimum(tok + 1, w).astype(F32)
        in_group = (chan >= gi * POOL_GROUP_DIM) & (chan < (gi + 1) * POOL_GROUP_DIM)
        d = jnp.where(in_group, sums[w] * inv_cnt, d)
    d = d - u
    y = jnp.dot(d.astype(BF16), w_ref[...], preferred_element_type=F32)
    o_ref[...] = (y * scale_ref[...]).astype(o_ref.dtype)


def _pool(proj, w_bd, scale, batch, seq, layer):
    t = proj.shape[0]
    ts = POOL_TS
    ns = seq // ts
    return pl.pallas_call(
        _pool_body,
        out_shape=jax.ShapeDtypeStruct((t, POOL_WIDTH), BF16),
        grid=(batch, ns),
        in_specs=[
            pl.BlockSpec((ts, POOL_WIDTH), lambda b, s: (b * ns + s, _BLK_PU)),
            pl.BlockSpec((POOL_HALO, POOL_WIDTH),
                         lambda b, s: (jnp.maximum((b * ns + s) * (ts // POOL_HALO) - 1, 0), _BLK_PU)),
            pl.BlockSpec((None, POOL_WIDTH, POOL_WIDTH), lambda b, s: (layer, 0, 0)),
            pl.BlockSpec((1, POOL_WIDTH), lambda b, s: (0, 0)),
        ],
        out_specs=pl.BlockSpec((ts, POOL_WIDTH), lambda b, s: (b * ns + s, 0)),
        compiler_params=_cparams("parallel", "arbitrary"),
        name="pool_mixer",
    )(proj, proj, w_bd, scale)


def _gla_body(qk_ref, v_ref, og_ref, gk_ref, norm_ref, tril_ref, bd_ref, o_ref, st_ref, oacc_ref):
    s_idx = pl.program_id(1)
    ts = qk_ref.shape[0]
    c = GLA_CHUNK

    @pl.when(s_idx == 0)
    def _():
        st_ref[...] = jnp.zeros_like(st_ref)

    qk = qk_ref[...].astype(F32)
    q = qk[:, :GLA_QK_W] * (GLA_DK ** -0.5)
    k = qk[:, GLA_QK_W:]
    v = v_ref[...]
    b = jnp.dot(tril_ref[...], gk_ref[...], preferred_element_type=F32,
                precision=lax.Precision.HIGHEST)
    q_t = (q * jnp.exp(b)).astype(BF16)
    k_t = (k * jnp.exp(-b)).astype(BF16)

    kcol = lax.broadcasted_iota(jnp.int32, (1, GLA_QK_W), 1)
    vcol = lax.broadcasted_iota(jnp.int32, (1, GLA_V_W), 1)
    ri = lax.broadcasted_iota(jnp.int32, (c, c), 0)
    ci = lax.broadcasted_iota(jnp.int32, (c, c), 1)
    causal = ri >= ci
    bd = bd_ref[...]

    for n in range(ts // c):
        rows = slice(n * c, (n + 1) * c)
        b_n = b[rows]
        b_last = b_n[c - 1:c, :]
        q_n = q_t[rows]
        k_n = k_t[rows]
        v_n = v[rows]
        k_s = (k[rows] * jnp.exp(b_last - b_n)).astype(BF16)
        q4 = jnp.concatenate(
            [jnp.where((kcol >= h * GLA_DK) & (kcol < (h + 1) * GLA_DK), q_n, jnp.zeros_like(q_n))
             for h in range(GLA_HEADS)], axis=0)
        a4 = lax.dot_general(q4, k_n, (((1,), (1,)), ((), ())), preferred_element_type=F32)
        o_n = jnp.zeros((c, GLA_V_W), F32)
        for h in range(GLA_HEADS):
            a_h = jnp.where(causal, a4[h * c:(h + 1) * c], 0.0).astype(BF16)
            o_h = jnp.dot(a_h, v_n, preferred_element_type=F32)
            o_n = jnp.where((vcol >= h * GLA_DV) & (vcol < (h + 1) * GLA_DV), o_h, o_n)
        st = st_ref[...]
        o_n = o_n + lax.dot_general(q_n, st.astype(BF16), (((1,), (1,)), ((), ())),
                                    preferred_element_type=F32)
        kv = lax.dot_general(v_n, k_s, (((0,), (0,)), ((), ())), preferred_element_type=F32)
        st_ref[...] = st * jnp.exp(b_last) + kv * bd
        oacc_ref[rows, :] = o_n

    o = oacc_ref[...]
    o2 = o * o
    inv = jnp.zeros_like(o)
    for h in range(GLA_HEADS):
        in_head = (vcol >= h * GLA_DV) & (vcol < (h + 1) * GLA_DV)
        ms = jnp.sum(jnp.where(in_head, o2, 0.0), axis=1, keepdims=True) * (1.0 / GLA_DV)
        inv = jnp.where(in_head, lax.rsqrt(ms + EPS), inv)
    og = og_ref[...].astype(F32)
    o_ref[...] = (o * inv * norm_ref[...] * (og * jax.nn.sigmoid(og))).astype(o_ref.dtype)


def _gla(proj, gk, norm, batch, seq):
    t = proj.shape[0]
    ts = GLA_TS
    ns = seq // ts
    r = np.arange(ts)
    tril = ((r[:, None] // GLA_CHUNK == r[None, :] // GLA_CHUNK) & (r[:, None] >= r[None, :]))
    tril = jnp.asarray(tril, F32)
    bd = (np.arange(GLA_V_W)[:, None] // GLA_DV) == (np.arange(GLA_QK_W)[None, :] // GLA_DK)
    bd = jnp.asarray(bd, F32)
    row = lambda b, s: b * ns + s
    return pl.pallas_call(
        _gla_body,
        out_shape=jax.ShapeDtypeStruct((t, GLA_V_W), BF16),
        grid=(batch, ns),
        in_specs=[
            pl.BlockSpec((ts, 2 * GLA_QK_W), lambda b, s: (row(b, s), _BLK_GQK)),
            pl.BlockSpec((ts, GLA_V_W), lambda b, s: (row(b, s), _BLK_GV)),
            pl.BlockSpec((ts, GLA_V_W), lambda b, s: (row(b, s), _BLK_GO)),
            pl.BlockSpec((ts, GLA_QK_W), lambda b, s: (row(b, s), 0)),
            pl.BlockSpec((1, GLA_V_W), lambda b, s: (0, 0)),
            pl.BlockSpec((ts, ts), lambda b, s: (0, 0)),
            pl.BlockSpec((GLA_V_W, GLA_QK_W), lambda b, s: (0, 0)),
        ],
        out_specs=pl.BlockSpec((ts, GLA_V_W), lambda b, s: (row(b, s), 0)),
        scratch_shapes=[pltpu.VMEM((GLA_V_W, GLA_QK_W), F32), pltpu.VMEM((ts, GLA_V_W), F32)],
        compiler_params=_cparams("parallel", "arbitrary"),
        name="gla",
    )(proj, proj, proj, gk, norm, tril, bd)


def _merge_body(h_ref, ya_ref, yp_ref, yg_ref, wga_ref, wgp_ref, wgg_ref, bga_ref, bgp_ref, bgg_ref,
                wa_ref, wp_ref, wg_ref, o_ref, wgate_ref):
    tn = o_ref.shape[1]

    @pl.when(pl.program_id(1) == 0)
    def _():
        for br, win_ref in enumerate((wga_ref, wgp_ref, wgg_ref)):
            win = win_ref[...].astype(F32)
            wgate_ref[br] = win[:, GATE_LEAD:GATE_LEAD + tn].astype(BF16)

    h = h_ref[...]
    acc = None
    for br, (y_ref, bgate_ref, wbr_ref) in enumerate(((ya_ref, bga_ref, wa_ref),
                                                     (yp_ref, bgp_ref, wp_ref),
                                                     (yg_ref, bgg_ref, wg_ref))):
        gate = jax.nn.sigmoid(jnp.dot(h, wgate_ref[br], preferred_element_type=F32) + bgate_ref[...])
        term = gate * jnp.dot(y_ref[...], wbr_ref[...], preferred_element_type=F32)
        acc = term if acc is None else acc + term
    o_ref[...] = acc.astype(o_ref.dtype)


def _merge(h, y_att, y_pool, y_gla, w_in, b_gate, w_br_att, w_br_pool, w_br_gla, layer):
    t = h.shape[0]
    tm, tn = MERGE_TM, MERGE_TN
    nn = D_MODEL // tn
    row = lambda j, i: (i, 0)
    y_spec = pl.BlockSpec((tm, ATT_Q_W), row)
    gate_base = _IN_GATE - GATE_LEAD
    gate_specs = [pl.BlockSpec((pl.Squeezed(), pl.Element(D_MODEL), pl.Element(tn + LANES)),
                               functools.partial(
                                   lambda j, i, br: (layer, 0, pl.multiple_of(
                                       gate_base + br * D_MODEL + j * tn, LANES)), br=br))
                  for br in range(N_BRANCH)]
    bias_specs = [pl.BlockSpec((1, tn), functools.partial(lambda j, i, br: (0, br * nn + j), br=br))
                  for br in range(N_BRANCH)]
    br_spec = pl.BlockSpec((None, ATT_Q_W, tn), lambda j, i: (layer, 0, j))
    return pl.pallas_call(
        _merge_body,
        out_shape=jax.ShapeDtypeStruct((t, D_MODEL), BF16),
        grid=(nn, t // tm),
        in_specs=[pl.BlockSpec((tm, D_MODEL), row), y_spec, y_spec, y_spec,
                  *gate_specs, *bias_specs, br_spec, br_spec, br_spec],
        out_specs=pl.BlockSpec((tm, tn), lambda j, i: (i, j)),
        scratch_shapes=[pltpu.VMEM((N_BRANCH, D_MODEL, tn), BF16)],
        compiler_params=_cparams("arbitrary", "arbitrary"),
        name="gated_merge",
    )(h, y_att, y_pool, y_gla, w_in, w_in, w_in, b_gate, b_gate, b_gate,
      w_br_att, w_br_pool, w_br_gla)


def _outproj_body(m_ref, w_ref, x_ref, o_ref):
    o_ref[...] = x_ref[...] + jnp.dot(m_ref[...], w_ref[...], preferred_element_type=F32)


def _outproj(merged, w_out, x, layer):
    t = x.shape[0]
    tm, tn = OUT_TM, OUT_TN
    return pl.pallas_call(
        _outproj_body,
        out_shape=jax.ShapeDtypeStruct((t, D_MODEL), F32),
        grid=(t // tm, D_MODEL // tn),
        in_specs=[
            pl.BlockSpec((tm, D_MODEL), lambda i, j: (i, 0)),
            pl.BlockSpec((None, D_MODEL, tn), lambda i, j: (layer, 0, j)),
            pl.BlockSpec((tm, tn), lambda i, j: (i, j)),
        ],
        out_specs=pl.BlockSpec((tm, tn), lambda i, j: (i, j)),
        compiler_params=_cparams("parallel", "arbitrary"),
        name="outproj",
    )(merged, w_out, x)


def _prep_mixer_weights(w_in, w_pool, w_gla_a2, w_br_att, w_br_pool, w_br_gla, w_out):
    depth = w_in.shape[0]
    w_in_bf = jnp.pad(w_in.astype(BF16), ((0, 0), (0, 0), (0, W_IN_PAD - W_IN_COLS)))
    w_a2 = jnp.pad(w_gla_a2.astype(BF16), ((0, 0), (0, LANES - GLA_LOWRANK), (0, 0)))
    groups = len(POOL_WINDOWS)
    eye = jnp.eye(groups, dtype=BF16)
    w_bd = (w_pool.astype(BF16)[:, :, :, None, :] * eye[None, :, None, :, None]).reshape(
        depth, POOL_WIDTH, POOL_WIDTH)
    return dict(w_in=w_in_bf, w_a2=w_a2, w_bd=w_bd,
                w_br_att=w_br_att.astype(BF16), w_br_pool=w_br_pool.astype(BF16),
                w_br_gla=w_br_gla.astype(BF16), w_out=w_out.astype(BF16))


def _mixer(x, cos_t, sin_t, batch, seq, layer, mw, norm_mix, b_gate, att_sinks, pool_scale,
           b_gla_a, gla_norm):
    proj, gk, h = _inproj(x, norm_mix[None, :], mw["w_in"], mw["w_a2"], b_gla_a[None, :], layer)
    y_att = _attention(proj, cos_t, sin_t, att_sinks, batch, seq)
    y_pool = _pool(proj, mw["w_bd"], pool_scale[None, :], batch, seq, layer)
    y_gla = _gla(proj, gk, gla_norm[None, :], batch, seq)
    merged = _merge(h, y_att, y_pool, y_gla, mw["w_in"], b_gate[None, :],
                    mw["w_br_att"], mw["w_br_pool"], mw["w_br_gla"], layer)
    return _outproj(merged, mw["w_out"], x, layer)


def kernel(x, positions, norm_ffn1, ffn1_wi, ffn1_wo, norm_mix, w_in, b_gate, att_sinks, w_pool, pool_scale, w_gla_a2, b_gla_a, gla_norm, w_br_att, w_br_pool, w_br_gla, w_out, norm_ffn2, ffn2_wi, ffn2_wo, norm_final):
    batch, seq, d = x.shape
    depth = norm_ffn1.shape[0]
    assert d == D_MODEL and seq % max(POOL_TS, GLA_TS, ATT_BLOCK) == 0
    t = batch * seq
    assert t % max(FFN_TM, MERGE_TM, OUT_TM, ROPE_TM, INPROJ_TM) == 0
    xt = x.reshape(t, d)
    cos_t, sin_t = _rope_tables(positions)
    gf = norm_final[None, :]
    wi1, wo1 = ffn1_wi.astype(BF16), ffn1_wo.astype(BF16)
    wi2, wo2 = ffn2_wi.astype(BF16), ffn2_wo.astype(BF16)
    mw = _prep_mixer_weights(w_in, w_pool, w_gla_a2, w_br_att, w_br_pool, w_br_gla, w_out)
    for l in range(depth):
        xt = _ffn(xt, norm_ffn1[l][None, :], wi1, wo1, gf, l, False)
        xt = _mixer(xt, cos_t, sin_t, batch, seq, l, mw, norm_mix[l], b_gate[l], att_sinks[l],
                    pool_scale[l], b_gla_a[l], gla_norm[l])
        xt = _ffn(xt, norm_ffn2[l][None, :], wi2, wo2, gf, l, l == depth - 1)
    return xt.reshape(batch, seq, d)
```

```python
import functools

import numpy as np
import jax
import jax.numpy as jnp
from jax import lax
from jax.experimental import pallas as pl
from jax.experimental.pallas import tpu as pltpu

F32 = jnp.float32
BF16 = jnp.bfloat16

D_MODEL = 2048
D_FF = 5632
ATT_HEADS = 12
ATT_KV_HEADS = 4
HEAD_DIM = 64
ATT_BLOCK = 128
ROPE_THETA = 10000.0
ATT_Q_W = ATT_HEADS * HEAD_DIM
ATT_KV_W = ATT_KV_HEADS * HEAD_DIM
POOL_WINDOWS = (2, 4, 8, 16)
POOL_GROUP_DIM = 192
POOL_WIDTH = 768
GLA_HEADS = 4
GLA_DK = 96
GLA_DV = 192
GLA_QK_W = GLA_HEADS * GLA_DK
GLA_V_W = GLA_HEADS * GLA_DV
GLA_LOWRANK = 16
GLA_TAU = 16.0
GLA_CHUNK = 64
N_BRANCH = 3
EPS = 1e-6

_IN_QA, _IN_KA, _IN_VA, _IN_PU = 0, 768, 1024, 1280
_IN_QG, _IN_KG, _IN_VG, _IN_OG = 2048, 2432, 2816, 3584
_IN_LR = 4352
_IN_GATE = 4368

PROJ_W = 4352
_BLK_GQK, _BLK_AQ, _BLK_PU, _BLK_GV, _BLK_GO = 0, 1, 2, 3, 4
_BLK_AK, _BLK_AV = 15, 16
_PROJ_SEGMENTS = ((0, _IN_QG, 768), (768, _IN_QA, 768), (1536, _IN_PU, 768),
                  (2304, _IN_VG, 1536), (3840, _IN_KA, 512))
W_IN_COLS = _IN_GATE + N_BRANCH * D_MODEL
GATE_LEAD = _IN_GATE % 128
W_IN_PAD = -(-W_IN_COLS // 128) * 128

LANES = 128
POOL_HALO = 16
VMEM_LIMIT = 58 * 1024 * 1024

FFN_TM, FFN_TF = 1024, 256
INPROJ_TM = 512
INPROJ_NCH = 1024
POOL_TS = 512
GLA_TS = 256
MERGE_TM, MERGE_TN = 1024, 512
OUT_TM, OUT_TN = 1024, 1024
ROPE_TM = 1024


def _cparams(*sem):
    return pltpu.CompilerParams(dimension_semantics=sem, vmem_limit_bytes=VMEM_LIMIT)


def _rms(x, g):
    return x * lax.rsqrt(jnp.mean(x * x, axis=-1, keepdims=True) + EPS) * g


def _ffn_body(x_ref, g_ref, gf_ref, wi_hbm, wo_hbm, o_ref, h_ref, u_ref, wa_buf, wb_buf, wo_buf, sem,
              *, layer, n_tiles, final_norm):
    tf = FFN_TF
    nch = D_FF // tf
    i = pl.program_id(0)

    def up_copies(c, slot):
        col = pl.multiple_of(c * tf, tf)
        return (pltpu.make_async_copy(wi_hbm.at[layer, :, pl.ds(col, tf)], wa_buf.at[slot], sem.at[0, slot]),
                pltpu.make_async_copy(wi_hbm.at[layer, :, pl.ds(D_FF + col, tf)], wb_buf.at[slot],
                                      sem.at[1, slot]))

    def down_copy(c, slot):
        row = pl.multiple_of(c * tf, tf)
        return pltpu.make_async_copy(wo_hbm.at[layer, pl.ds(row, tf), :], wo_buf.at[slot], sem.at[2, slot])

    def up(slot):
        h = h_ref[...]
        a = jnp.dot(h, wa_buf[slot].astype(BF16), preferred_element_type=F32)
        b = jnp.dot(h, wb_buf[slot].astype(BF16), preferred_element_type=F32)
        u_ref[slot] = (a * jax.nn.sigmoid(a) * (0.5 * b)).astype(BF16)

    def down(slot):
        o_ref[...] += jnp.dot(u_ref[slot], wo_buf[slot].astype(BF16), preferred_element_type=F32)

    def stage(c, slot):
        for cp in up_copies(c, slot):
            cp.wait()
        down_copy(c - 1, 1 - slot).wait()
        down_copy(c, slot).start()
        down(1 - slot)
        up(slot)

    @pl.when(i == 0)
    def _():
        for cp in up_copies(0, 0):
            cp.start()

    x = x_ref[...]
    h_ref[...] = _rms(x, g_ref[...]).astype(BF16)
    o_ref[...] = x
    for cp in up_copies(0, 0):
        cp.wait()
    for cp in up_copies(1, 1):
        cp.start()
    down_copy(0, 0).start()
    up(0)

    def pair(k, carry):
        c = 2 * k + 1
        for cp in up_copies(c + 1, 0):
            cp.start()
        stage(c, 1)
        for cp in up_copies(c + 2, 1):
            cp.start()
        stage(c + 1, 0)
        return carry

    lax.fori_loop(0, (nch - 2) // 2, pair, 0)

    stage(nch - 1, 1)

    @pl.when(i + 1 < n_tiles)
    def _():
        for cp in up_copies(0, 0):
            cp.start()

    down_copy(nch - 1, 1).wait()
    down(1)
    if final_norm:
        o_ref[...] = _rms(o_ref[...], gf_ref[...])


def _ffn(x, g, wi, wo, gf, layer, final_norm):
    t = x.shape[0]
    tm, tf = FFN_TM, FFN_TF
    assert (D_FF // tf) % 2 == 0
    n_tiles = t // tm
    return pl.pallas_call(
        functools.partial(_ffn_body, layer=layer, n_tiles=n_tiles, final_norm=final_norm),
        out_shape=jax.ShapeDtypeStruct((t, D_MODEL), F32),
        grid=(n_tiles,),
        in_specs=[
            pl.BlockSpec((tm, D_MODEL), lambda i: (i, 0)),
            pl.BlockSpec((1, D_MODEL), lambda i: (0, 0)),
            pl.BlockSpec((1, D_MODEL), lambda i: (0, 0)),
            pl.BlockSpec(memory_space=pl.ANY),
            pl.BlockSpec(memory_space=pl.ANY),
        ],
        out_specs=pl.BlockSpec((tm, D_MODEL), lambda i: (i, 0)),
        scratch_shapes=[
            pltpu.VMEM((tm, D_MODEL), BF16),
            pltpu.VMEM((2, tm, tf), BF16),
            pltpu.VMEM((2, D_MODEL, tf), F32),
            pltpu.VMEM((2, D_MODEL, tf), F32),
            pltpu.VMEM((2, tf, D_MODEL), F32),
            pltpu.SemaphoreType.DMA((3, 2)),
        ],
        compiler_params=_cparams("arbitrary"),
        name="ffn_final" if final_norm else "ffn",
    )(x, g, gf, wi, wo)


def _rope_table_body(pos_ref, freq_ref, sign_ref, cos_ref, sin_ref):
    ang = pos_ref[...].astype(F32) * freq_ref[...]
    cos_ref[...] = jnp.cos(ang)
    sin_ref[...] = jnp.sin(ang) * sign_ref[...]


def _rope_tables(positions):
    t = positions.size
    half = HEAD_DIM // 2
    inv_freq = ROPE_THETA ** (-jnp.arange(half, dtype=F32) / half)
    freq = jnp.tile(inv_freq, LANES // half)[None, :]
    sign = jnp.tile(jnp.concatenate([-jnp.ones((half,), F32), jnp.ones((half,), F32)]),
                    LANES // HEAD_DIM)[None, :]
    tm = ROPE_TM
    return pl.pallas_call(
        _rope_table_body,
        out_shape=(jax.ShapeDtypeStruct((t, LANES), F32), jax.ShapeDtypeStruct((t, LANES), F32)),
        grid=(t // tm,),
        in_specs=[
            pl.BlockSpec((tm, 1), lambda i: (i, 0)),
            pl.BlockSpec((1, LANES), lambda i: (0, 0)),
            pl.BlockSpec((1, LANES), lambda i: (0, 0)),
        ],
        out_specs=(pl.BlockSpec((tm, LANES), lambda i: (i, 0)),
                   pl.BlockSpec((tm, LANES), lambda i: (i, 0))),
        compiler_params=_cparams("parallel"),
        name="rope_tables",
    )(positions.reshape(t, 1), freq, sign)


def _inproj_body(x_ref, g_ref, w_ref, wlr_ref, wa2_ref, ba_ref, proj_ref, gk_ref, h_ref):
    h = _rms(x_ref[...], g_ref[...]).astype(BF16)
    h_ref[...] = h
    for dst, src, width in _PROJ_SEGMENTS:
        proj_ref[:, dst:dst + width] = jnp.dot(h, w_ref[:, src:src + width],
                                               preferred_element_type=F32).astype(proj_ref.dtype)
    lr = jnp.dot(h, wlr_ref[...], preferred_element_type=F32)
    z = jnp.dot(lr.astype(BF16), wa2_ref[...], preferred_element_type=F32) + ba_ref[...]
    gk_ref[...] = (jnp.minimum(z, 0.0) - jnp.log1p(jnp.exp(-jnp.abs(z)))) * (1.0 / GLA_TAU)


def _inproj(x, g, w_in, w_a2, b_a, layer):
    t = x.shape[0]
    tm = INPROJ_TM
    const = lambda i: (0, 0)
    lconst = lambda i: (layer, 0, 0)
    return pl.pallas_call(
        _inproj_body,
        out_shape=(jax.ShapeDtypeStruct((t, PROJ_W), BF16),
                   jax.ShapeDtypeStruct((t, GLA_QK_W), F32),
                   jax.ShapeDtypeStruct((t, D_MODEL), BF16)),
        grid=(t // tm,),
        in_specs=[
            pl.BlockSpec((tm, D_MODEL), lambda i: (i, 0)),
            pl.BlockSpec((1, D_MODEL), const),
            pl.BlockSpec((None, D_MODEL, PROJ_W), lconst, pipeline_mode=pl.Buffered(1)),
            pl.BlockSpec((None, D_MODEL, LANES), lambda i: (layer, 0, _IN_LR // LANES)),
            pl.BlockSpec((None, LANES, GLA_QK_W), lconst),
            pl.BlockSpec((1, GLA_QK_W), const),
        ],
        out_specs=(pl.BlockSpec((tm, PROJ_W), lambda i: (i, 0)),
                   pl.BlockSpec((tm, GLA_QK_W), lambda i: (i, 0)),
                   pl.BlockSpec((tm, D_MODEL), lambda i: (i, 0))),
        compiler_params=_cparams("parallel"),
        name="inproj",
    )(x, g, w_in, w_in, w_a2, b_a)


def _swap_halves(x):
    lane = lax.broadcasted_iota(jnp.int32, x.shape, 1)
    first = (lane % HEAD_DIM) < (HEAD_DIM // 2)
    return jnp.where(first, pltpu.roll(x, LANES - HEAD_DIM // 2, 1), pltpu.roll(x, HEAD_DIM // 2, 1))


def _rope(x, cos, sin):
    cols = []
    for c in range(x.shape[1] // LANES):
        xc = x[:, c * LANES:(c + 1) * LANES]
        cols.append(xc * cos + _swap_halves(xc) * sin)
    return jnp.concatenate(cols, axis=1)


def _attn_body(sink_ref, q_ref, kc_ref, kp_ref, vc_ref, vp_ref, cc_ref, sc_ref, cp_ref, sp_ref, o_ref):
    n = pl.program_id(1)
    blk = ATT_BLOCK
    q = _rope(q_ref[...].astype(F32), cc_ref[...], sc_ref[...]).astype(BF16)
    k = jnp.concatenate([_rope(kp_ref[...].astype(F32), cp_ref[...], sp_ref[...]),
                         _rope(kc_ref[...].astype(F32), cc_ref[...], sc_ref[...])], axis=0).astype(BF16)
    v = jnp.concatenate([vp_ref[...], vc_ref[...]], axis=0)

    qi = lax.broadcasted_iota(jnp.int32, (blk, 2 * blk), 0)
    si = lax.broadcasted_iota(jnp.int32, (blk, 2 * blk), 1)
    rel = blk + qi - si
    mask = (rel >= 0) & (rel < blk) & ((si >= blk) | (n > 0))

    group = ATT_HEADS // ATT_KV_HEADS
    outs = []
    for hd in range(ATT_HEADS):
        g = hd // group
        qh = q[:, hd * HEAD_DIM:(hd + 1) * HEAD_DIM]
        kg = k[:, g * HEAD_DIM:(g + 1) * HEAD_DIM]
        vg = v[:, g * HEAD_DIM:(g + 1) * HEAD_DIM]
        s = lax.dot_general(qh, kg, (((1,), (1,)), ((), ())),
                            preferred_element_type=F32) * (HEAD_DIM ** -0.5)
        s = jnp.where(mask, s, -1e30)
        sink = sink_ref[hd]
        m = jnp.maximum(jnp.max(s, axis=1, keepdims=True), sink)
        e = jnp.exp(s - m)
        denom = jnp.sum(e, axis=1, keepdims=True) + jnp.exp(sink - m)
        o = jnp.dot(e.astype(BF16), vg, preferred_element_type=F32)
        outs.append(o / denom)
    o_ref[...] = jnp.concatenate(outs, axis=1).astype(o_ref.dtype)


def _attention(proj, cos_t, sin_t, sinks, batch, seq):
    t = proj.shape[0]
    blk = ATT_BLOCK
    nb = seq // blk
    cur = lambda b, n: b * nb + n
    prev = lambda b, n: b * nb + jnp.maximum(n - 1, 0)
    return pl.pallas_call(
        _attn_body,
        out_shape=jax.ShapeDtypeStruct((t, ATT_Q_W), BF16),
        grid=(batch, nb),
        in_specs=[
            pl.BlockSpec(memory_space=pltpu.SMEM),
            pl.BlockSpec((blk, ATT_Q_W), lambda b, n: (cur(b, n), _BLK_AQ)),
            pl.BlockSpec((blk, ATT_KV_W), lambda b, n: (cur(b, n), _BLK_AK)),
            pl.BlockSpec((blk, ATT_KV_W), lambda b, n: (prev(b, n), _BLK_AK)),
            pl.BlockSpec((blk, ATT_KV_W), lambda b, n: (cur(b, n), _BLK_AV)),
            pl.BlockSpec((blk, ATT_KV_W), lambda b, n: (prev(b, n), _BLK_AV)),
            pl.BlockSpec((blk, LANES), lambda b, n: (cur(b, n), 0)),
            pl.BlockSpec((blk, LANES), lambda b, n: (cur(b, n), 0)),
            pl.BlockSpec((blk, LANES), lambda b, n: (prev(b, n), 0)),
            pl.BlockSpec((blk, LANES), lambda b, n: (prev(b, n), 0)),
        ],
        out_specs=pl.BlockSpec((blk, ATT_Q_W), lambda b, n: (cur(b, n), 0)),
        compiler_params=_cparams("parallel", "arbitrary"),
        name="swa_attention",
    )(sinks, proj, proj, proj, proj, proj, cos_t, sin_t, cos_t, sin_t)


def _pool_body(u_ref, halo_ref, w_ref, scale_ref, o_ref):
    s_idx = pl.program_id(1)
    ts = u_ref.shape[0]
    u = u_ref[...].astype(F32)
    halo = jnp.where(s_idx > 0, halo_ref[...].astype(F32), 0.0)
    a1 = jnp.concatenate([halo, u], axis=0)
    length = ts + POOL_HALO
    a2 = a1[1:] + a1[:length - 1]
    a4 = a2[2:] + a2[:length - 3]
    a8 = a4[4:] + a4[:length - 7]
    a16 = a8[8:] + a8[:length - 15]
    sums = {2: a2[POOL_HALO - 1:], 4: a4[POOL_HALO - 3:], 8: a8[POOL_HALO - 7:], 16: a16[POOL_HALO - 15:]}

    tok = s_idx * ts + lax.broadcasted_iota(jnp.int32, (ts, 1), 0)
    chan = lax.broadcasted_iota(jnp.int32, (1, POOL_WIDTH), 1)
    d = jnp.zeros((ts, POOL_WIDTH), F32)
    for gi, w in enumerate(POOL_WINDOWS):
        inv_cnt = 1.0 / jnp.minimum(tok + 1, w).astype(F32)
        in_group = (chan >= gi * POOL_GROUP_DIM) & (chan < (gi + 1) * POOL_GROUP_DIM)
        d = jnp.where(in_group, sums[w] * inv_cnt, d)
    d = d - u
    y = jnp.dot(d.astype(BF16), w_ref[...], preferred_element_type=F32)
    o_ref[...] = (y * scale_ref[...]).astype(o_ref.dtype)


def _pool(proj, w_bd, scale, batch, seq, layer):
    t = proj.shape[0]
    ts = POOL_TS
    ns = seq // ts
    return pl.pallas_call(
        _pool_body,
        out_shape=jax.ShapeDtypeStruct((t, POOL_WIDTH), BF16),
        grid=(batch, ns),
        in_specs=[
            pl.BlockSpec((ts, POOL_WIDTH), lambda b, s: (b * ns + s, _BLK_PU)),
            pl.BlockSpec((POOL_HALO, POOL_WIDTH),
                         lambda b, s: (jnp.maximum((b * ns + s) * (ts // POOL_HALO) - 1, 0), _BLK_PU)),
            pl.BlockSpec((None, POOL_WIDTH, POOL_WIDTH), lambda b, s: (layer, 0, 0)),
            pl.BlockSpec((1, POOL_WIDTH), lambda b, s: (0, 0)),
        ],
        out_specs=pl.BlockSpec((ts, POOL_WIDTH), lambda b, s: (b * ns + s, 0)),
        compiler_params=_cparams("parallel", "arbitrary"),
        name="pool_mixer",
    )(proj, proj, w_bd, scale)


def _gla_body(qk_ref, v_ref, og_ref, gk_ref, norm_ref, tril_ref, bd_ref, o_ref, st_ref, oacc_ref):
    s_idx = pl.program_id(1)
    ts = qk_ref.shape[0]
    c = GLA_CHUNK

    @pl.when(s_idx == 0)
    def _():
        st_ref[...] = jnp.zeros_like(st_ref)

    qk = qk_ref[...].astype(F32)
    q = qk[:, :GLA_QK_W] * (GLA_DK ** -0.5)
    k = qk[:, GLA_QK_W:]
    v = v_ref[...]
    b = jnp.dot(tril_ref[...], gk_ref[...], preferred_element_type=F32,
                precision=lax.Precision.HIGHEST)
    q_t = (q * jnp.exp(b)).astype(BF16)
    k_t = (k * jnp.exp(-b)).astype(BF16)

    kcol = lax.broadcasted_iota(jnp.int32, (1, GLA_QK_W), 1)
    vcol = lax.broadcasted_iota(jnp.int32, (1, GLA_V_W), 1)
    ri = lax.broadcasted_iota(jnp.int32, (c, c), 0)
    ci = lax.broadcasted_iota(jnp.int32, (c, c), 1)
    causal = ri >= ci
    bd = bd_ref[...]

    for n in range(ts // c):
        rows = slice(n * c, (n + 1) * c)
        b_n = b[rows]
        b_last = b_n[c - 1:c, :]
        q_n = q_t[rows]
        k_n = k_t[rows]
        v_n = v[rows]
        k_s = (k[rows] * jnp.exp(b_last - b_n)).astype(BF16)
        q4 = jnp.concatenate(
            [jnp.where((kcol >= h * GLA_DK) & (kcol < (h + 1) * GLA_DK), q_n, jnp.zeros_like(q_n))
             for h in range(GLA_HEADS)], axis=0)
        a4 = lax.dot_general(q4, k_n, (((1,), (1,)), ((), ())), preferred_element_type=F32)
        o_n = jnp.zeros((c, GLA_V_W), F32)
        for h in range(GLA_HEADS):
            a_h = jnp.where(causal, a4[h * c:(h + 1) * c], 0.0).astype(BF16)
            o_h = jnp.dot(a_h, v_n, preferred_element_type=F32)
            o_n = jnp.where((vcol >= h * GLA_DV) & (vcol < (h + 1) * GLA_DV), o_h, o_n)
        st = st_ref[...]
        o_n = o_n + lax.dot_general(q_n, st.astype(BF16), (((1,), (1,)), ((), ())),
                                    preferred_element_type=F32)
        kv = lax.dot_general(v_n, k_s, (((0,), (0,)), ((), ())), preferred_element_type=F32)
        st_ref[...] = st * jnp.exp(b_last) + kv * bd
        oacc_ref[rows, :] = o_n

    o = oacc_ref[...]
    o2 = o * o
    inv = jnp.zeros_like(o)
    for h in range(GLA_HEADS):
        in_head = (vcol >= h * GLA_DV) & (vcol < (h + 1) * GLA_DV)
        ms = jnp.sum(jnp.where(in_head, o2, 0.0), axis=1, keepdims=True) * (1.0 / GLA_DV)
        inv = jnp.where(in_head, lax.rsqrt(ms + EPS), inv)
    og = og_ref[...].astype(F32)
    o_ref[...] = (o * inv * norm_ref[...] * (og * jax.nn.sigmoid(og))).astype(o_ref.dtype)


def _gla(proj, gk, norm, batch, seq):
    t = proj.shape[0]
    ts = GLA_TS
    ns = seq // ts
    r = np.arange(ts)
    tril = ((r[:, None] // GLA_CHUNK == r[None, :] // GLA_CHUNK) & (r[:, None] >= r[None, :]))
    tril = jnp.asarray(tril, F32)
    bd = (np.arange(GLA_V_W)[:, None] // GLA_DV) == (np.arange(GLA_QK_W)[None, :] // GLA_DK)
    bd = jnp.asarray(bd, F32)
    row = lambda b, s: b * ns + s
    return pl.pallas_call(
        _gla_body,
        out_shape=jax.ShapeDtypeStruct((t, GLA_V_W), BF16),
        grid=(batch, ns),
        in_specs=[
            pl.BlockSpec((ts, 2 * GLA_QK_W), lambda b, s: (row(b, s), _BLK_GQK)),
            pl.BlockSpec((ts, GLA_V_W), lambda b, s: (row(b, s), _BLK_GV)),
            pl.BlockSpec((ts, GLA_V_W), lambda b, s: (row(b, s), _BLK_GO)),
            pl.BlockSpec((ts, GLA_QK_W), lambda b, s: (row(b, s), 0)),
            pl.BlockSpec((1, GLA_V_W), lambda b, s: (0, 0)),
            pl.BlockSpec((ts, ts), lambda b, s: (0, 0)),
            pl.BlockSpec((GLA_V_W, GLA_QK_W), lambda b, s: (0, 0)),
        ],
        out_specs=pl.BlockSpec((ts, GLA_V_W), lambda b, s: (row(b, s), 0)),
        scratch_shapes=[pltpu.VMEM((GLA_V_W, GLA_QK_W), F32), pltpu.VMEM((ts, GLA_V_W), F32)],
        compiler_params=_cparams("parallel", "arbitrary"),
        name="gla",
    )(proj, proj, proj, gk, norm, tril, bd)


def _merge_body(h_ref, ya_ref, yp_ref, yg_ref, wga_ref, wgp_ref, wgg_ref, bga_ref, bgp_ref, bgg_ref,
                wa_ref, wp_ref, wg_ref, o_ref, wgate_ref):
    tn = o_ref.shape[1]

    @pl.when(pl.program_id(1) == 0)
    def _():
        for br, win_ref in enumerate((wga_ref, wgp_ref, wgg_ref)):
            win = win_ref[...].astype(F32)
            wgate_ref[br] = win[:, GATE_LEAD:GATE_LEAD + tn].astype(BF16)

    h = h_ref[...]
    acc = None
    for br, (y_ref, bgate_ref, wbr_ref) in enumerate(((ya_ref, bga_ref, wa_ref),
                                                     (yp_ref, bgp_ref, wp_ref),
                                                     (yg_ref, bgg_ref, wg_ref))):
        gate = jax.nn.sigmoid(jnp.dot(h, wgate_ref[br], preferred_element_type=F32) + bgate_ref[...])
        term = gate * jnp.dot(y_ref[...], wbr_ref[...], preferred_element_type=F32)
        acc = term if acc is None else acc + term
    o_ref[...] = acc.astype(o_ref.dtype)


def _merge(h, y_att, y_pool, y_gla, w_in, b_gate, w_br_att, w_br_pool, w_br_gla, layer):
    t = h.shape[0]
    tm, tn = MERGE_TM, MERGE_TN
    nn = D_MODEL // tn
    row = lambda j, i: (i, 0)
    y_spec = pl.BlockSpec((tm, ATT_Q_W), row)
    gate_base = _IN_GATE - GATE_LEAD
    gate_specs = [pl.BlockSpec((pl.Squeezed(), pl.Element(D_MODEL), pl.Element(tn + LANES)),
                               functools.partial(
                                   lambda j, i, br: (layer, 0, pl.multiple_of(
                                       gate_base + br * D_MODEL + j * tn, LANES)), br=br))
                  for br in range(N_BRANCH)]
    bias_specs = [pl.BlockSpec((1, tn), functools.partial(lambda j, i, br: (0, br * nn + j), br=br))
                  for br in range(N_BRANCH)]
    br_spec = pl.BlockSpec((None, ATT_Q_W, tn), lambda j, i: (layer, 0, j))
    return pl.pallas_call(
        _merge_body,
        out_shape=jax.ShapeDtypeStruct((t, D_MODEL), BF16),
        grid=(nn, t // tm),
        in_specs=[pl.BlockSpec((tm, D_MODEL), row), y_spec, y_spec, y_spec,
                  *gate_specs, *bias_specs, br_spec, br_spec, br_spec],
        out_specs=pl.BlockSpec((tm, tn), lambda j, i: (i, j)),
        scratch_shapes=[pltpu.VMEM((N_BRANCH, D_MODEL, tn), BF16)],
        compiler_params=_cparams("arbitrary", "arbitrary"),
        name="gated_merge",
    )(h, y_att, y_pool, y_gla, w_in, w_in, w_in, b_gate, b_gate, b_gate,
      w_br_att, w_br_pool, w_br_gla)


def _outproj_body(m_ref, w_ref, x_ref, o_ref):
    o_ref[...] = x_ref[...] + jnp.dot(m_ref[...], w_ref[...], preferred_element_type=F32)


def _outproj(merged, w_out, x, layer):
    t = x.shape[0]
    tm, tn = OUT_TM, OUT_TN
    return pl.pallas_call(
        _outproj_body,
        out_shape=jax.ShapeDtypeStruct((t, D_MODEL), F32),
        grid=(t // tm, D_MODEL // tn),
        in_specs=[
            pl.BlockSpec((tm, D_MODEL), lambda i, j: (i, 0)),
            pl.BlockSpec((None, D_MODEL, tn), lambda i, j: (layer, 0, j)),
            pl.BlockSpec((tm, tn), lambda i, j: (i, j)),
        ],
        out_specs=pl.BlockSpec((tm, tn), lambda i, j: (i, j)),
        compiler_params=_cparams("parallel", "arbitrary"),
        name="outproj",
    )(merged, w_out, x)


def _prep_mixer_weights(w_in, w_pool, w_gla_a2, w_br_att, w_br_pool, w_br_gla, w_out):
    depth = w_in.shape[0]
    w_in_bf = jnp.pad(w_in.astype(BF16), ((0, 0), (0, 0), (0, W_IN_PAD - W_IN_COLS)))
    w_a2 = jnp.pad(w_gla_a2.astype(BF16), ((0, 0), (0, LANES - GLA_LOWRANK), (0, 0)))
    groups = len(POOL_WINDOWS)
    eye = jnp.eye(groups, dtype=BF16)
    w_bd = (w_pool.astype(BF16)[:, :, :, None, :] * eye[None, :, None, :, None]).reshape(
        depth, POOL_WIDTH, POOL_WIDTH)
    return dict(w_in=w_in_bf, w_a2=w_a2, w_bd=w_bd,
                w_br_att=w_br_att.astype(BF16), w_br_pool=w_br_pool.astype(BF16),
                w_br_gla=w_br_gla.astype(BF16), w_out=w_out.astype(BF16))


def _mixer(x, cos_t, sin_t, batch, seq, layer, mw, norm_mix, b_gate, att_sinks, pool_scale,
           b_gla_a, gla_norm):
    proj, gk, h = _inproj(x, norm_mix[None, :], mw["w_in"], mw["w_a2"], b_gla_a[None, :], layer)
    y_att = _attention(proj, cos_t, sin_t, att_sinks, batch, seq)
    y_pool = _pool(proj, mw["w_bd"], pool_scale[None, :], batch, seq, layer)
    y_gla = _gla(proj, gk, gla_norm[None, :], batch, seq)
    merged = _merge(h, y_att, y_pool, y_gla, mw["w_in"], b_gate[None, :],
                    mw["w_br_att"], mw["w_br_pool"], mw["w_br_gla"], layer)
    return _outproj(merged, mw["w_out"], x, layer)


def kernel(x, positions, norm_ffn1, ffn1_wi, ffn1_wo, norm_mix, w_in, b_gate, att_sinks, w_pool, pool_scale, w_gla_a2, b_gla_a, gla_norm, w_br_att, w_br_pool, w_br_gla, w_out, norm_ffn2, ffn2_wi, ffn2_wo, norm_final):
    batch, seq, d = x.shape
    depth = norm_ffn1.shape[0]
    assert d == D_MODEL and seq % max(POOL_TS, GLA_TS, ATT_BLOCK) == 0
    t = batch * seq
    assert t % max(FFN_TM, MERGE_TM, OUT_TM, ROPE_TM, INPROJ_TM) == 0
    xt = x.reshape(t, d)
    cos_t, sin_t = _rope_tables(positions)
    gf = norm_final[None, :]
    mw = _prep_mixer_weights(w_in, w_pool, w_gla_a2, w_br_att, w_br_pool, w_br_gla, w_out)
    for l in range(depth):
        xt = _ffn(xt, norm_ffn1[l][None, :], ffn1_wi, ffn1_wo, gf, l, False)
        xt = _mixer(xt, cos_t, sin_t, batch, seq, l, mw, norm_mix[l], b_gate[l], att_sinks[l],
                    pool_scale[l], b_gla_a[l], gla_norm[l])
        xt = _ffn(xt, norm_ffn2[l][None, :], ffn2_wi, ffn2_wo, gf, l, l == depth - 1)
    return xt.reshape(batch, seq, d)
```

```python
import functools

import numpy as np
import jax
import jax.numpy as jnp
from jax import lax
from jax.experimental import pallas as pl
from jax.experimental.pallas import tpu as pltpu

F32 = jnp.float32
BF16 = jnp.bfloat16

D_MODEL = 2048
D_FF = 5632
ATT_HEADS = 12
ATT_KV_HEADS = 4
HEAD_DIM = 64
ATT_BLOCK = 128
ROPE_THETA = 10000.0
ATT_Q_W = ATT_HEADS * HEAD_DIM
ATT_KV_W = ATT_KV_HEADS * HEAD_DIM
POOL_WINDOWS = (2, 4, 8, 16)
POOL_GROUP_DIM = 192
POOL_WIDTH = 768
GLA_HEADS = 4
GLA_DK = 96
GLA_DV = 192
GLA_QK_W = GLA_HEADS * GLA_DK
GLA_V_W = GLA_HEADS * GLA_DV
GLA_LOWRANK = 16
GLA_TAU = 16.0
GLA_CHUNK = 64
N_BRANCH = 3
EPS = 1e-6

_IN_QA, _IN_KA, _IN_VA, _IN_PU = 0, 768, 1024, 1280
_IN_QG, _IN_KG, _IN_VG, _IN_OG = 2048, 2432, 2816, 3584
_IN_LR = 4352
_IN_GATE = 4368

PROJ_W = 4352
_BLK_GQK, _BLK_AQ, _BLK_PU, _BLK_GV, _BLK_GO = 0, 1, 2, 3, 4
_BLK_AK, _BLK_AV = 15, 16
_PROJ_SEGMENTS = ((0, _IN_QG, 768), (768, _IN_QA, 768), (1536, _IN_PU, 768),
                  (2304, _IN_VG, 1536), (3840, _IN_KA, 512))
W_IN_COLS = _IN_GATE + N_BRANCH * D_MODEL
GATE_LEAD = _IN_GATE % 128
W_IN_PAD = -(-W_IN_COLS // 128) * 128

LANES = 128
POOL_HALO = 16
VMEM_LIMIT = 58 * 1024 * 1024

FFN_TM, FFN_TF = 1024, 256
FFN_UNROLL = 4
INPROJ_TM = 512
INPROJ_NCH = 1024
POOL_TS = 512
GLA_TS = 256
MERGE_TM, MERGE_TN = 1024, 512
OUT_TM, OUT_TN = 1024, 1024
ROPE_TM = 1024


def _cparams(*sem):
    return pltpu.CompilerParams(dimension_semantics=sem, vmem_limit_bytes=VMEM_LIMIT)


def _rms(x, g):
    return x * lax.rsqrt(jnp.mean(x * x, axis=-1, keepdims=True) + EPS) * g


def _ffn_body(x_ref, g_ref, gf_ref, wi_hbm, wo_hbm, o_ref, h_ref, u_ref, wa_buf, wb_buf, wo_buf, sem,
              *, layer, n_tiles, final_norm):
    tf = FFN_TF
    nch = D_FF // tf
    i = pl.program_id(0)

    def up_copies(c, slot):
        col = pl.multiple_of(c * tf, tf)
        return (pltpu.make_async_copy(wi_hbm.at[layer, :, pl.ds(col, tf)], wa_buf.at[slot], sem.at[0, slot]),
                pltpu.make_async_copy(wi_hbm.at[layer, :, pl.ds(D_FF + col, tf)], wb_buf.at[slot],
                                      sem.at[1, slot]))

    def down_copy(c, slot):
        row = pl.multiple_of(c * tf, tf)
        return pltpu.make_async_copy(wo_hbm.at[layer, pl.ds(row, tf), :], wo_buf.at[slot], sem.at[2, slot])

    def up(slot):
        h = h_ref[...]
        a = jnp.dot(h, wa_buf[slot].astype(BF16), preferred_element_type=F32)
        b = jnp.dot(h, wb_buf[slot].astype(BF16), preferred_element_type=F32)
        u_ref[slot] = (a * jax.nn.sigmoid(a) * (0.5 * b)).astype(BF16)

    def down(slot):
        o_ref[...] += jnp.dot(u_ref[slot], wo_buf[slot].astype(BF16), preferred_element_type=F32)

    def start_up(c, slot):
        ca, cb = up_copies(c, slot)
        ca.start(priority=0)
        cb.start(priority=1)

    def stage(c, slot):
        for cp in up_copies(c, slot):
            cp.wait()
        down_copy(c - 1, 1 - slot).wait()
        down_copy(c, slot).start(priority=1)
        down(1 - slot)
        up(slot)

    @pl.when(i == 0)
    def _():
        start_up(0, 0)

    x = x_ref[...]
    h_ref[...] = _rms(x, g_ref[...]).astype(BF16)
    o_ref[...] = x
    for cp in up_copies(0, 0):
        cp.wait()
    start_up(1, 1)
    down_copy(0, 0).start(priority=1)
    up(0)

    def stages(k, carry):
        c0 = FFN_UNROLL * k + 1
        for s in range(FFN_UNROLL):
            slot = (1 + s) % 2
            start_up(c0 + s + 1, 1 - slot)
            stage(c0 + s, slot)
        return carry

    lax.fori_loop(0, (nch - 2) // FFN_UNROLL, stages, 0)

    stage(nch - 1, 1)

    @pl.when(i + 1 < n_tiles)
    def _():
        start_up(0, 0)

    down_copy(nch - 1, 1).wait()
    down(1)
    if final_norm:
        o_ref[...] = _rms(o_ref[...], gf_ref[...])


def _ffn(x, g, wi, wo, gf, layer, final_norm):
    t = x.shape[0]
    tm, tf = FFN_TM, FFN_TF
    assert FFN_UNROLL % 2 == 0 and (D_FF // tf - 2) % FFN_UNROLL == 0
    n_tiles = t // tm
    return pl.pallas_call(
        functools.partial(_ffn_body, layer=layer, n_tiles=n_tiles, final_norm=final_norm),
        out_shape=jax.ShapeDtypeStruct((t, D_MODEL), F32),
        grid=(n_tiles,),
        in_specs=[
            pl.BlockSpec((tm, D_MODEL), lambda i: (i, 0)),
            pl.BlockSpec((1, D_MODEL), lambda i: (0, 0)),
            pl.BlockSpec((1, D_MODEL), lambda i: (0, 0)),
            pl.BlockSpec(memory_space=pl.ANY),
            pl.BlockSpec(memory_space=pl.ANY),
        ],
        out_specs=pl.BlockSpec((tm, D_MODEL), lambda i: (i, 0)),
        scratch_shapes=[
            pltpu.VMEM((tm, D_MODEL), BF16),
            pltpu.VMEM((2, tm, tf), BF16),
            pltpu.VMEM((2, D_MODEL, tf), F32),
            pltpu.VMEM((2, D_MODEL, tf), F32),
            pltpu.VMEM((2, tf, D_MODEL), F32),
            pltpu.SemaphoreType.DMA((3, 2)),
        ],
        compiler_params=_cparams("arbitrary"),
        name="ffn_final" if final_norm else "ffn",
    )(x, g, gf, wi, wo)


def _rope_table_body(pos_ref, freq_ref, sign_ref, cos_ref, sin_ref):
    ang = pos_ref[...].astype(F32) * freq_ref[...]
    cos_ref[...] = jnp.cos(ang)
    sin_ref[...] = jnp.sin(ang) * sign_ref[...]


def _rope_tables(positions):
    t = positions.size
    half = HEAD_DIM // 2
    inv_freq = ROPE_THETA ** (-jnp.arange(half, dtype=F32) / half)
    freq = jnp.tile(inv_freq, LANES // half)[None, :]
    sign = jnp.tile(jnp.concatenate([-jnp.ones((half,), F32), jnp.ones((half,), F32)]),
                    LANES // HEAD_DIM)[None, :]
    tm = ROPE_TM
    return pl.pallas_call(
        _rope_table_body,
        out_shape=(jax.ShapeDtypeStruct((t, LANES), F32), jax.ShapeDtypeStruct((t, LANES), F32)),
        grid=(t // tm,),
        in_specs=[
            pl.BlockSpec((tm, 1), lambda i: (i, 0)),
            pl.BlockSpec((1, LANES), lambda i: (0, 0)),
            pl.BlockSpec((1, LANES), lambda i: (0, 0)),
        ],
        out_specs=(pl.BlockSpec((tm, LANES), lambda i: (i, 0)),
                   pl.BlockSpec((tm, LANES), lambda i: (i, 0))),
        compiler_params=_cparams("parallel"),
        name="rope_tables",
    )(positions.reshape(t, 1), freq, sign)


def _inproj_body(x_ref, g_ref, w_ref, wlr_ref, wa2_ref, ba_ref, proj_ref, gk_ref, h_ref):
    h = _rms(x_ref[...], g_ref[...]).astype(BF16)
    h_ref[...] = h
    for dst, src, width in _PROJ_SEGMENTS:
        proj_ref[:, dst:dst + width] = jnp.dot(h, w_ref[:, src:src + width],
                                               preferred_element_type=F32).astype(proj_ref.dtype)
    lr = jnp.dot(h, wlr_ref[...], preferred_element_type=F32)
    z = jnp.dot(lr.astype(BF16), wa2_ref[...], preferred_element_type=F32) + ba_ref[...]
    gk_ref[...] = (jnp.minimum(z, 0.0) - jnp.log1p(jnp.exp(-jnp.abs(z)))) * (1.0 / GLA_TAU)


def _inproj(x, g, w_in, w_a2, b_a, layer):
    t = x.shape[0]
    tm = INPROJ_TM
    const = lambda i: (0, 0)
    lconst = lambda i: (layer, 0, 0)
    return pl.pallas_call(
        _inproj_body,
        out_shape=(jax.ShapeDtypeStruct((t, PROJ_W), BF16),
                   jax.ShapeDtypeStruct((t, GLA_QK_W), F32),
                   jax.ShapeDtypeStruct((t, D_MODEL), BF16)),
        grid=(t // tm,),
        in_specs=[
            pl.BlockSpec((tm, D_MODEL), lambda i: (i, 0)),
            pl.BlockSpec((1, D_MODEL), const),
            pl.BlockSpec((None, D_MODEL, PROJ_W), lconst, pipeline_mode=pl.Buffered(1)),
            pl.BlockSpec((None, D_MODEL, LANES), lambda i: (layer, 0, _IN_LR // LANES)),
            pl.BlockSpec((None, LANES, GLA_QK_W), lconst),
            pl.BlockSpec((1, GLA_QK_W), const),
        ],
        out_specs=(pl.BlockSpec((tm, PROJ_W), lambda i: (i, 0)),
                   pl.BlockSpec((tm, GLA_QK_W), lambda i: (i, 0)),
                   pl.BlockSpec((tm, D_MODEL), lambda i: (i, 0))),
        compiler_params=_cparams("parallel"),
        name="inproj",
    )(x, g, w_in, w_in, w_a2, b_a)


def _swap_halves(x):
    lane = lax.broadcasted_iota(jnp.int32, x.shape, 1)
    first = (lane % HEAD_DIM) < (HEAD_DIM // 2)
    return jnp.where(first, pltpu.roll(x, LANES - HEAD_DIM // 2, 1), pltpu.roll(x, HEAD_DIM // 2, 1))


def _rope(x, cos, sin):
    cols = []
    for c in range(x.shape[1] // LANES):
        xc = x[:, c * LANES:(c + 1) * LANES]
        cols.append(xc * cos + _swap_halves(xc) * sin)
    return jnp.concatenate(cols, axis=1)


def _attn_body(sink_ref, q_ref, kc_ref, kp_ref, vc_ref, vp_ref, cc_ref, sc_ref, cp_ref, sp_ref, o_ref):
    n = pl.program_id(1)
    blk = ATT_BLOCK
    q = _rope(q_ref[...].astype(F32), cc_ref[...], sc_ref[...]).astype(BF16)
    k = jnp.concatenate([_rope(kp_ref[...].astype(F32), cp_ref[...], sp_ref[...]),
                         _rope(kc_ref[...].astype(F32), cc_ref[...], sc_ref[...])], axis=0).astype(BF16)
    v = jnp.concatenate([vp_ref[...], vc_ref[...]], axis=0)

    qi = lax.broadcasted_iota(jnp.int32, (blk, 2 * blk), 0)
    si = lax.broadcasted_iota(jnp.int32, (blk, 2 * blk), 1)
    rel = blk + qi - si
    mask = (rel >= 0) & (rel < blk) & ((si >= blk) | (n > 0))

    group = ATT_HEADS // ATT_KV_HEADS
    outs = []
    for hd in range(ATT_HEADS):
        g = hd // group
        qh = q[:, hd * HEAD_DIM:(hd + 1) * HEAD_DIM]
        kg = k[:, g * HEAD_DIM:(g + 1) * HEAD_DIM]
        vg = v[:, g * HEAD_DIM:(g + 1) * HEAD_DIM]
        s = lax.dot_general(qh, kg, (((1,), (1,)), ((), ())),
                            preferred_element_type=F32) * (HEAD_DIM ** -0.5)
        s = jnp.where(mask, s, -1e30)
        sink = sink_ref[hd]
        m = jnp.maximum(jnp.max(s, axis=1, keepdims=True), sink)
        e = jnp.exp(s - m)
        denom = jnp.sum(e, axis=1, keepdims=True) + jnp.exp(sink - m)
        o = jnp.dot(e.astype(BF16), vg, preferred_element_type=F32)
        outs.append(o / denom)
    o_ref[...] = jnp.concatenate(outs, axis=1).astype(o_ref.dtype)


def _attention(proj, cos_t, sin_t, sinks, batch, seq):
    t = proj.shape[0]
    blk = ATT_BLOCK
    nb = seq // blk
    cur = lambda b, n: b * nb + n
    prev = lambda b, n: b * nb + jnp.maximum(n - 1, 0)
    return pl.pallas_call(
        _attn_body,
        out_shape=jax.ShapeDtypeStruct((t, ATT_Q_W), BF16),
        grid=(batch, nb),
        in_specs=[
            pl.BlockSpec(memory_space=pltpu.SMEM),
            pl.BlockSpec((blk, ATT_Q_W), lambda b, n: (cur(b, n), _BLK_AQ)),
            pl.BlockSpec((blk, ATT_KV_W), lambda b, n: (cur(b, n), _BLK_AK)),
            pl.BlockSpec((blk, ATT_KV_W), lambda b, n: (prev(b, n), _BLK_AK)),
            pl.BlockSpec((blk, ATT_KV_W), lambda b, n: (cur(b, n), _BLK_AV)),
            pl.BlockSpec((blk, ATT_KV_W), lambda b, n: (prev(b, n), _BLK_AV)),
            pl.BlockSpec((blk, LANES), lambda b, n: (cur(b, n), 0)),
            pl.BlockSpec((blk, LANES), lambda b, n: (cur(b, n), 0)),
            pl.BlockSpec((blk, LANES), lambda b, n: (prev(b, n), 0)),
            pl.BlockSpec((blk, LANES), lambda b, n: (prev(b, n), 0)),
        ],
        out_specs=pl.BlockSpec((blk, ATT_Q_W), lambda b, n: (cur(b, n), 0)),
        compiler_params=_cparams("parallel", "arbitrary"),
        name="swa_attention",
    )(sinks, proj, proj, proj, proj, proj, cos_t, sin_t, cos_t, sin_t)


def _pool_body(u_ref, halo_ref, w_ref, scale_ref, o_ref):
    s_idx = pl.program_id(1)
    ts = u_ref.shape[0]
    u = u_ref[...].astype(F32)
    halo = jnp.where(s_idx > 0, halo_ref[...].astype(F32), 0.0)
    a1 = jnp.concatenate([halo, u], axis=0)
    length = ts + POOL_HALO
    a2 = a1[1:] + a1[:length - 1]
    a4 = a2[2:] + a2[:length - 3]
    a8 = a4[4:] + a4[:length - 7]
    a16 = a8[8:] + a8[:length - 15]
    sums = {2: a2[POOL_HALO - 1:], 4: a4[POOL_HALO - 3:], 8: a8[POOL_HALO - 7:], 16: a16[POOL_HALO - 15:]}

    tok = s_idx * ts + lax.broadcasted_iota(jnp.int32, (ts, 1), 0)
    chan = lax.broadcasted_iota(jnp.int32, (1, POOL_WIDTH), 1)
    d = jnp.zeros((ts, POOL_WIDTH), F32)
    for gi, w in enumerate(POOL_WINDOWS):
        inv_cnt = 1.0 / jnp.minimum(tok + 1, w).astype(F32)
        in_group = (chan >= gi * POOL_GROUP_DIM) & (chan < (gi + 1) * POOL_GROUP_DIM)
        d = jnp.where(in_group, sums[w] * inv_cnt, d)
    d = d - u
    y = jnp.dot(d.astype(BF16), w_ref[...], preferred_element_type=F32)
    o_ref[...] = (y * scale_ref[...]).astype(o_ref.dtype)


def _pool(proj, w_bd, scale, batch, seq, layer):
    t = proj.shape[0]
    ts = POOL_TS
    ns = seq // ts
    return pl.pallas_call(
        _pool_body,
        out_shape=jax.ShapeDtypeStruct((t, POOL_WIDTH), BF16),
        grid=(batch, ns),
        in_specs=[
            pl.BlockSpec((ts, POOL_WIDTH), lambda b, s: (b * ns + s, _BLK_PU)),
            pl.BlockSpec((POOL_HALO, POOL_WIDTH),
                         lambda b, s: (jnp.maximum((b * ns + s) * (ts // POOL_HALO) - 1, 0), _BLK_PU)),
            pl.BlockSpec((None, POOL_WIDTH, POOL_WIDTH), lambda b, s: (layer, 0, 0)),
            pl.BlockSpec((1, POOL_WIDTH), lambda b, s: (0, 0)),
        ],
        out_specs=pl.BlockSpec((ts, POOL_WIDTH), lambda b, s: (b * ns + s, 0)),
        compiler_params=_cparams("parallel", "arbitrary"),
        name="pool_mixer",
    )(proj, proj, w_bd, scale)


def _gla_body(qk_ref, v_ref, og_ref, gk_ref, norm_ref, tril_ref, bd_ref, o_ref, st_ref, oacc_ref):
    s_idx = pl.program_id(1)
    ts = qk_ref.shape[0]
    c = GLA_CHUNK

    @pl.when(s_idx == 0)
    def _():
        st_ref[...] = jnp.zeros_like(st_ref)

    qk = qk_ref[...].astype(F32)
    q = qk[:, :GLA_QK_W] * (GLA_DK ** -0.5)
    k = qk[:, GLA_QK_W:]
    v = v_ref[...]
    b = jnp.dot(tril_ref[...], gk_ref[...], preferred_element_type=F32,
                precision=lax.Precision.HIGHEST)
    q_t = (q * jnp.exp(b)).astype(BF16)
    k_t = (k * jnp.exp(-b)).astype(BF16)

    kcol = lax.broadcasted_iota(jnp.int32, (1, GLA_QK_W), 1)
    vcol = lax.broadcasted_iota(jnp.int32, (1, GLA_V_W), 1)
    ri = lax.broadcasted_iota(jnp.int32, (c, c), 0)
    ci = lax.broadcasted_iota(jnp.int32, (c, c), 1)
    causal = ri >= ci
    bd = bd_ref[...]

    for n in range(ts // c):
        rows = slice(n * c, (n + 1) * c)
        b_n = b[rows]
        b_last = b_n[c - 1:c, :]
        q_n = q_t[rows]
        k_n = k_t[rows]
        v_n = v[rows]
        k_s = (k[rows] * jnp.exp(b_last - b_n)).astype(BF16)
        q4 = jnp.concatenate(
            [jnp.where((kcol >= h * GLA_DK) & (kcol < (h + 1) * GLA_DK), q_n, jnp.zeros_like(q_n))
             for h in range(GLA_HEADS)], axis=0)
        a4 = lax.dot_general(q4, k_n, (((1,), (1,)), ((), ())), preferred_element_type=F32)
        o_n = jnp.zeros((c, GLA_V_W), F32)
        for h in range(GLA_HEADS):
            a_h = jnp.where(causal, a4[h * c:(h + 1) * c], 0.0).astype(BF16)
            o_h = jnp.dot(a_h, v_n, preferred_element_type=F32)
            o_n = jnp.where((vcol >= h * GLA_DV) & (vcol < (h + 1) * GLA_DV), o_h, o_n)
        st = st_ref[...]
        o_n = o_n + lax.dot_general(q_n, st.astype(BF16), (((1,), (1,)), ((), ())),
                                    preferred_element_type=F32)
        kv = lax.dot_general(v_n, k_s, (((0,), (0,)), ((), ())), preferred_element_type=F32)
        st_ref[...] = st * jnp.exp(b_last) + kv * bd
        oacc_ref[rows, :] = o_n

    o = oacc_ref[...]
    o2 = o * o
    inv = jnp.zeros_like(o)
    for h in range(GLA_HEADS):
        in_head = (vcol >= h * GLA_DV) & (vcol < (h + 1) * GLA_DV)
        ms = jnp.sum(jnp.where(in_head, o2, 0.0), axis=1, keepdims=True) * (1.0 / GLA_DV)
        inv = jnp.where(in_head, lax.rsqrt(ms + EPS), inv)
    og = og_ref[...].astype(F32)
    o_ref[...] = (o * inv * norm_ref[...] * (og * jax.nn.sigmoid(og))).astype(o_ref.dtype)


def _gla(proj, gk, norm, batch, seq):
    t = proj.shape[0]
    ts = GLA_TS
    ns = seq // ts
    r = np.arange(ts)
    tril = ((r[:, None] // GLA_CHUNK == r[None, :] // GLA_CHUNK) & (r[:, None] >= r[None, :]))
    tril = jnp.asarray(tril, F32)
    bd = (np.arange(GLA_V_W)[:, None] // GLA_DV) == (np.arange(GLA_QK_W)[None, :] // GLA_DK)
    bd = jnp.asarray(bd, F32)
    row = lambda b, s: b * ns + s
    return pl.pallas_call(
        _gla_body,
        out_shape=jax.ShapeDtypeStruct((t, GLA_V_W), BF16),
        grid=(batch, ns),
        in_specs=[
            pl.BlockSpec((ts, 2 * GLA_QK_W), lambda b, s: (row(b, s), _BLK_GQK)),
            pl.BlockSpec((ts, GLA_V_W), lambda b, s: (row(b, s), _BLK_GV)),
            pl.BlockSpec((ts, GLA_V_W), lambda b, s: (row(b, s), _BLK_GO)),
            pl.BlockSpec((ts, GLA_QK_W), lambda b, s: (row(b, s), 0)),
            pl.BlockSpec((1, GLA_V_W), lambda b, s: (0, 0)),
            pl.BlockSpec((ts, ts), lambda b, s: (0, 0)),
            pl.BlockSpec((GLA_V_W, GLA_QK_W), lambda b, s: (0, 0)),
        ],
        out_specs=pl.BlockSpec((ts, GLA_V_W), lambda b, s: (row(b, s), 0)),
        scratch_shapes=[pltpu.VMEM((GLA_V_W, GLA_QK_W), F32), pltpu.VMEM((ts, GLA_V_W), F32)],
        compiler_params=_cparams("parallel", "arbitrary"),
        name="gla",
    )(proj, proj, proj, gk, norm, tril, bd)


def _merge_body(h_ref, ya_ref, yp_ref, yg_ref, wga_ref, wgp_ref, wgg_ref, bga_ref, bgp_ref, bgg_ref,
                wa_ref, wp_ref, wg_ref, o_ref, wgate_ref):
    tn = o_ref.shape[1]

    @pl.when(pl.program_id(1) == 0)
    def _():
        for br, win_ref in enumerate((wga_ref, wgp_ref, wgg_ref)):
            win = win_ref[...].astype(F32)
            wgate_ref[br] = win[:, GATE_LEAD:GATE_LEAD + tn].astype(BF16)

    h = h_ref[...]
    acc = None
    for br, (y_ref, bgate_ref, wbr_ref) in enumerate(((ya_ref, bga_ref, wa_ref),
                                                     (yp_ref, bgp_ref, wp_ref),
                                                     (yg_ref, bgg_ref, wg_ref))):
        gate = jax.nn.sigmoid(jnp.dot(h, wgate_ref[br], preferred_element_type=F32) + bgate_ref[...])
        term = gate * jnp.dot(y_ref[...], wbr_ref[...], preferred_element_type=F32)
        acc = term if acc is None else acc + term
    o_ref[...] = acc.astype(o_ref.dtype)


def _merge(h, y_att, y_pool, y_gla, w_in, b_gate, w_br_att, w_br_pool, w_br_gla, layer):
    t = h.shape[0]
    tm, tn = MERGE_TM, MERGE_TN
    nn = D_MODEL // tn
    row = lambda j, i: (i, 0)
    y_spec = pl.BlockSpec((tm, ATT_Q_W), row)
    gate_base = _IN_GATE - GATE_LEAD
    gate_specs = [pl.BlockSpec((pl.Squeezed(), pl.Element(D_MODEL), pl.Element(tn + LANES)),
                               functools.partial(
                                   lambda j, i, br: (layer, 0, pl.multiple_of(
                                       gate_base + br * D_MODEL + j * tn, LANES)), br=br))
                  for br in range(N_BRANCH)]
    bias_specs = [pl.BlockSpec((1, tn), functools.partial(lambda j, i, br: (0, br * nn + j), br=br))
                  for br in range(N_BRANCH)]
    br_spec = pl.BlockSpec((None, ATT_Q_W, tn), lambda j, i: (layer, 0, j))
    return pl.pallas_call(
        _merge_body,
        out_shape=jax.ShapeDtypeStruct((t, D_MODEL), BF16),
        grid=(nn, t // tm),
        in_specs=[pl.BlockSpec((tm, D_MODEL), row), y_spec, y_spec, y_spec,
                  *gate_specs, *bias_specs, br_spec, br_spec, br_spec],
        out_specs=pl.BlockSpec((tm, tn), lambda j, i: (i, j)),
        scratch_shapes=[pltpu.VMEM((N_BRANCH, D_MODEL, tn), BF16)],
        compiler_params=_cparams("arbitrary", "arbitrary"),
        name="gated_merge",
    )(h, y_att, y_pool, y_gla, w_in, w_in, w_in, b_gate, b_gate, b_gate,
      w_br_att, w_br_pool, w_br_gla)


def _outproj_body(m_ref, w_ref, x_ref, o_ref):
    o_ref[...] = x_ref[...] + jnp.dot(m_ref[...], w_ref[...], preferred_element_type=F32)


def _outproj(merged, w_out, x, layer):
    t = x.shape[0]
    tm, tn = OUT_TM, OUT_TN
    return pl.pallas_call(
        _outproj_body,
        out_shape=jax.ShapeDtypeStruct((t, D_MODEL), F32),
        grid=(t // tm, D_MODEL // tn),
        in_specs=[
            pl.BlockSpec((tm, D_MODEL), lambda i, j: (i, 0)),
            pl.BlockSpec((None, D_MODEL, tn), lambda i, j: (layer, 0, j)),
            pl.BlockSpec((tm, tn), lambda i, j: (i, j)),
        ],
        out_specs=pl.BlockSpec((tm, tn), lambda i, j: (i, j)),
        compiler_params=_cparams("parallel", "arbitrary"),
        name="outproj",
    )(merged, w_out, x)


def _prep_mixer_weights(w_in, w_pool, w_gla_a2, w_br_att, w_br_pool, w_br_gla, w_out):
    depth = w_in.shape[0]
    w_in_bf = jnp.pad(w_in, ((0, 0), (0, 0), (0, W_IN_PAD - W_IN_COLS))).astype(BF16)
    w_a2 = jnp.pad(w_gla_a2.astype(BF16), ((0, 0), (0, LANES - GLA_LOWRANK), (0, 0)))
    groups = len(POOL_WINDOWS)
    eye = jnp.eye(groups, dtype=BF16)
    w_bd = (w_pool.astype(BF16)[:, :, :, None, :] * eye[None, :, None, :, None]).reshape(
        depth, POOL_WIDTH, POOL_WIDTH)
    return dict(w_in=w_in_bf, w_a2=w_a2, w_bd=w_bd,
                w_br_att=w_br_att.astype(BF16), w_br_pool=w_br_pool.astype(BF16),
                w_br_gla=w_br_gla.astype(BF16), w_out=w_out.astype(BF16))


def _mixer(x, cos_t, sin_t, batch, seq, layer, mw, norm_mix, b_gate, att_sinks, pool_scale,
           b_gla_a, gla_norm):
    proj, gk, h = _inproj(x, norm_mix[None, :], mw["w_in"], mw["w_a2"], b_gla_a[None, :], layer)
    y_att = _attention(proj, cos_t, sin_t, att_sinks, batch, seq)
    y_pool = _pool(proj, mw["w_bd"], pool_scale[None, :], batch, seq, layer)
    y_gla = _gla(proj, gk, gla_norm[None, :], batch, seq)
    merged = _merge(h, y_att, y_pool, y_gla, mw["w_in"], b_gate[None, :],
                    mw["w_br_att"], mw["w_br_pool"], mw["w_br_gla"], layer)
    return _outproj(merged, mw["w_out"], x, layer)


def kernel(x, positions, norm_ffn1, ffn1_wi, ffn1_wo, norm_mix, w_in, b_gate, att_sinks, w_pool, pool_scale, w_gla_a2, b_gla_a, gla_norm, w_br_att, w_br_pool, w_br_gla, w_out, norm_ffn2, ffn2_wi, ffn2_wo, norm_final):
    batch, seq, d = x.shape
    depth = norm_ffn1.shape[0]
    assert d == D_MODEL and seq % max(POOL_TS, GLA_TS, ATT_BLOCK) == 0
    t = batch * seq
    assert t % max(FFN_TM, MERGE_TM, OUT_TM, ROPE_TM, INPROJ_TM) == 0
    xt = x.reshape(t, d)
    cos_t, sin_t = _rope_tables(positions)
    gf = norm_final[None, :]
    mw = _prep_mixer_weights(w_in, w_pool, w_gla_a2, w_br_att, w_br_pool, w_br_gla, w_out)
    for l in range(depth):
        xt = _ffn(xt, norm_ffn1[l][None, :], ffn1_wi, ffn1_wo, gf, l, False)
        xt = _mixer(xt, cos_t, sin_t, batch, seq, l, mw, norm_mix[l], b_gate[l], att_sinks[l],
                    pool_scale[l], b_gla_a[l], gla_norm[l])
        xt = _ffn(xt, norm_ffn2[l][None, :], ffn2_wi, ffn2_wo, gf, l, l == depth - 1)
    return xt.reshape(batch, seq, d)
```

```python
import functools

import numpy as np
import jax
import jax.numpy as jnp
from jax import lax
from jax.experimental import pallas as pl
from jax.experimental.pallas import tpu as pltpu

F32 = jnp.float32
BF16 = jnp.bfloat16

D_MODEL = 2048
D_FF = 5632
ATT_HEADS = 12
ATT_KV_HEADS = 4
HEAD_DIM = 64
ATT_BLOCK = 128
ROPE_THETA = 10000.0
ATT_Q_W = ATT_HEADS * HEAD_DIM
ATT_KV_W = ATT_KV_HEADS * HEAD_DIM
POOL_WINDOWS = (2, 4, 8, 16)
POOL_GROUP_DIM = 192
POOL_WIDTH = 768
GLA_HEADS = 4
GLA_DK = 96
GLA_DV = 192
GLA_QK_W = GLA_HEADS * GLA_DK
GLA_V_W = GLA_HEADS * GLA_DV
GLA_LOWRANK = 16
GLA_TAU = 16.0
GLA_CHUNK = 64
N_BRANCH = 3
EPS = 1e-6

_IN_QA, _IN_KA, _IN_VA, _IN_PU = 0, 768, 1024, 1280
_IN_QG, _IN_KG, _IN_VG, _IN_OG = 2048, 2432, 2816, 3584
_IN_LR = 4352
_IN_GATE = 4368

PROJ_W = 4352
_BLK_GQK, _BLK_AQ, _BLK_PU, _BLK_GV, _BLK_GO = 0, 1, 2, 3, 4
_BLK_AK, _BLK_AV = 15, 16
_PROJ_SEGMENTS = ((0, _IN_QG, 768, None), (768, _IN_QA, 768, HEAD_DIM ** -0.5),
                  (1536, _IN_PU, 768, None), (2304, _IN_VG, 1536, None),
                  (3840, _IN_KA, 256, 1.0), (4096, _IN_VA, 256, None))
W_IN_COLS = _IN_GATE + N_BRANCH * D_MODEL
GATE_LEAD = _IN_GATE % 128

LANES = 128
POOL_HALO = 16
VMEM_LIMIT = 58 * 1024 * 1024

FFN_TM, FFN_TF = 1024, 256
FFN_UNROLL = 4
INPROJ_TM = 512
INPROJ_NCH = 1024
POOL_TS = 512
GLA_TS = 256
MERGE_TM, MERGE_TN = 1024, 512
OUT_TM, OUT_TN = 1024, 1024
ROPE_TM = 1024


def _cparams(*sem):
    return pltpu.CompilerParams(dimension_semantics=sem, vmem_limit_bytes=VMEM_LIMIT)


def _rms(x, g):
    return x * lax.rsqrt(jnp.mean(x * x, axis=-1, keepdims=True) + EPS) * g


def _ffn_body(x_ref, g_ref, gf_ref, wi_hbm, wo_hbm, o_ref, h_ref, u_ref, wa_buf, wb_buf, wo_buf, sem,
              *, layer, n_tiles, final_norm):
    tf = FFN_TF
    nch = D_FF // tf
    i = pl.program_id(0)

    def up_copies(c, slot):
        col = pl.multiple_of(c * tf, tf)
        return (pltpu.make_async_copy(wi_hbm.at[layer, :, pl.ds(col, tf)], wa_buf.at[slot], sem.at[0, slot]),
                pltpu.make_async_copy(wi_hbm.at[layer, :, pl.ds(D_FF + col, tf)], wb_buf.at[slot],
                                      sem.at[1, slot]))

    def down_copy(c, slot):
        row = pl.multiple_of(c * tf, tf)
        return pltpu.make_async_copy(wo_hbm.at[layer, pl.ds(row, tf), :], wo_buf.at[slot], sem.at[2, slot])

    def up(slot):
        h = h_ref[...]
        a = jnp.dot(h, wa_buf[slot].astype(BF16), preferred_element_type=F32)
        b = jnp.dot(h, wb_buf[slot].astype(BF16), preferred_element_type=F32)
        u_ref[slot] = (a * jax.nn.sigmoid(a) * (0.5 * b)).astype(BF16)

    def down(slot):
        o_ref[...] += jnp.dot(u_ref[slot], wo_buf[slot].astype(BF16), preferred_element_type=F32)

    def start_up(c, slot):
        ca, cb = up_copies(c, slot)
        ca.start(priority=0)
        cb.start(priority=1)

    def stage(c, slot):
        for cp in up_copies(c, slot):
            cp.wait()
        down_copy(c - 1, 1 - slot).wait()
        down_copy(c, slot).start(priority=1)
        down(1 - slot)
        up(slot)

    @pl.when(i == 0)
    def _():
        start_up(0, 0)

    x = x_ref[...]
    h_ref[...] = _rms(x, g_ref[...]).astype(BF16)
    o_ref[...] = x
    for cp in up_copies(0, 0):
        cp.wait()
    start_up(1, 1)
    down_copy(0, 0).start(priority=1)
    up(0)

    def stages(k, carry):
        c0 = FFN_UNROLL * k + 1
        for s in range(FFN_UNROLL):
            slot = (1 + s) % 2
            start_up(c0 + s + 1, 1 - slot)
            stage(c0 + s, slot)
        return carry

    lax.fori_loop(0, (nch - 2) // FFN_UNROLL, stages, 0)

    stage(nch - 1, 1)

    @pl.when(i + 1 < n_tiles)
    def _():
        start_up(0, 0)

    down_copy(nch - 1, 1).wait()
    down(1)
    if final_norm:
        o_ref[...] = _rms(o_ref[...], gf_ref[...])


def _ffn(x, g, wi, wo, gf, layer, final_norm):
    t = x.shape[0]
    tm, tf = FFN_TM, FFN_TF
    assert FFN_UNROLL % 2 == 0 and (D_FF // tf - 2) % FFN_UNROLL == 0
    n_tiles = t // tm
    return pl.pallas_call(
        functools.partial(_ffn_body, layer=layer, n_tiles=n_tiles, final_norm=final_norm),
        out_shape=jax.ShapeDtypeStruct((t, D_MODEL), F32),
        grid=(n_tiles,),
        in_specs=[
            pl.BlockSpec((tm, D_MODEL), lambda i: (i, 0)),
            pl.BlockSpec((1, D_MODEL), lambda i: (0, 0)),
            pl.BlockSpec((1, D_MODEL), lambda i: (0, 0)),
            pl.BlockSpec(memory_space=pl.ANY),
            pl.BlockSpec(memory_space=pl.ANY),
        ],
        out_specs=pl.BlockSpec((tm, D_MODEL), lambda i: (i, 0)),
        scratch_shapes=[
            pltpu.VMEM((tm, D_MODEL), BF16),
            pltpu.VMEM((2, tm, tf), BF16),
            pltpu.VMEM((2, D_MODEL, tf), F32),
            pltpu.VMEM((2, D_MODEL, tf), F32),
            pltpu.VMEM((2, tf, D_MODEL), F32),
            pltpu.SemaphoreType.DMA((3, 2)),
        ],
        compiler_params=_cparams("arbitrary"),
        name="ffn_final" if final_norm else "ffn",
    )(x, g, gf, wi, wo)


def _rope_table_body(pos_ref, freq_ref, sign_ref, cos_ref, sin_ref):
    ang = pos_ref[...].astype(F32) * freq_ref[...]
    cos_ref[...] = jnp.cos(ang)
    sin_ref[...] = jnp.sin(ang) * sign_ref[...]


def _rope_tables(positions):
    t = positions.size
    half = HEAD_DIM // 2
    inv_freq = ROPE_THETA ** (-jnp.arange(half, dtype=F32) / half)
    freq = jnp.tile(inv_freq, LANES // half)[None, :]
    sign = jnp.tile(jnp.concatenate([-jnp.ones((half,), F32), jnp.ones((half,), F32)]),
                    LANES // HEAD_DIM)[None, :]
    tm = ROPE_TM
    return pl.pallas_call(
        _rope_table_body,
        out_shape=(jax.ShapeDtypeStruct((t, LANES), F32), jax.ShapeDtypeStruct((t, LANES), F32)),
        grid=(t // tm,),
        in_specs=[
            pl.BlockSpec((tm, 1), lambda i: (i, 0)),
            pl.BlockSpec((1, LANES), lambda i: (0, 0)),
            pl.BlockSpec((1, LANES), lambda i: (0, 0)),
        ],
        out_specs=(pl.BlockSpec((tm, LANES), lambda i: (i, 0)),
                   pl.BlockSpec((tm, LANES), lambda i: (i, 0))),
        compiler_params=_cparams("parallel"),
        name="rope_tables",
    )(positions.reshape(t, 1), freq, sign)


def _swap_halves(x):
    lane = lax.broadcasted_iota(jnp.int32, x.shape, 1)
    first = (lane % HEAD_DIM) < (HEAD_DIM // 2)
    return jnp.where(first, pltpu.roll(x, LANES - HEAD_DIM // 2, 1), pltpu.roll(x, HEAD_DIM // 2, 1))


def _rope(x, cos, sin):
    cols = []
    for c in range(x.shape[1] // LANES):
        xc = x[:, c * LANES:(c + 1) * LANES]
        cols.append(xc * cos + _swap_halves(xc) * sin)
    return jnp.concatenate(cols, axis=1)


def _inproj_body(x_ref, g_ref, w_ref, wlr_ref, wa2_ref, ba_ref, cos_ref, sin_ref,
                 proj_ref, gk_ref, h_ref):
    h = _rms(x_ref[...], g_ref[...]).astype(BF16)
    h_ref[...] = h
    cos, sin = cos_ref[...], sin_ref[...]
    for dst, src, width, rot_scale in _PROJ_SEGMENTS:
        p = jnp.dot(h, w_ref[:, src:src + width], preferred_element_type=F32)
        if rot_scale is not None:
            p = _rope(p, cos * rot_scale, sin * rot_scale)
        proj_ref[:, dst:dst + width] = p.astype(proj_ref.dtype)
    lr = jnp.dot(h, wlr_ref[...], preferred_element_type=F32)
    z = jnp.dot(lr.astype(BF16), wa2_ref[...], preferred_element_type=F32) + ba_ref[...]
    gk_ref[...] = (jnp.minimum(z, 0.0) - jnp.log1p(jnp.exp(-jnp.abs(z)))) * (1.0 / GLA_TAU)


def _inproj(x, g, w_in, w_a2, b_a, cos_t, sin_t, layer):
    t = x.shape[0]
    tm = INPROJ_TM
    const = lambda i: (0, 0)
    lconst = lambda i: (layer, 0, 0)
    rows = lambda i: (i, 0)
    return pl.pallas_call(
        _inproj_body,
        out_shape=(jax.ShapeDtypeStruct((t, PROJ_W), BF16),
                   jax.ShapeDtypeStruct((t, GLA_QK_W), F32),
                   jax.ShapeDtypeStruct((t, D_MODEL), BF16)),
        grid=(t // tm,),
        in_specs=[
            pl.BlockSpec((tm, D_MODEL), lambda i: (i, 0)),
            pl.BlockSpec((1, D_MODEL), const),
            pl.BlockSpec((None, D_MODEL, PROJ_W), lconst, pipeline_mode=pl.Buffered(1)),
            pl.BlockSpec((None, D_MODEL, LANES), lambda i: (layer, 0, _IN_LR // LANES)),
            pl.BlockSpec((None, LANES, GLA_QK_W), lconst),
            pl.BlockSpec((1, GLA_QK_W), const),
            pl.BlockSpec((tm, LANES), rows),
            pl.BlockSpec((tm, LANES), rows),
        ],
        out_specs=(pl.BlockSpec((tm, PROJ_W), lambda i: (i, 0)),
                   pl.BlockSpec((tm, GLA_QK_W), lambda i: (i, 0)),
                   pl.BlockSpec((tm, D_MODEL), lambda i: (i, 0))),
        compiler_params=_cparams("parallel"),
        name="inproj",
    )(x, g, w_in, w_in, w_a2, b_a, cos_t, sin_t)


def _both_halves(x, col, half):
    xc = x[:, col * LANES:(col + 1) * LANES].astype(F32)
    sw = pltpu.roll(xc, LANES // 2, 1)
    first = lax.broadcasted_iota(jnp.int32, xc.shape, 1) < LANES // 2
    dup = jnp.where(first, xc, sw) if half == 0 else jnp.where(first, sw, xc)
    return dup.astype(BF16)


def _attn_body(sink_ref, q_ref, kc_ref, kp_ref, vc_ref, vp_ref, o_ref):
    n = pl.program_id(1)
    blk = ATT_BLOCK
    group = ATT_HEADS // ATT_KV_HEADS
    q = q_ref[...]
    k = jnp.concatenate([kp_ref[...], kc_ref[...]], axis=0)
    v = jnp.concatenate([vp_ref[...], vc_ref[...]], axis=0)

    qi = lax.broadcasted_iota(jnp.int32, (blk, 2 * blk), 0)
    si = lax.broadcasted_iota(jnp.int32, (blk, 2 * blk), 1)
    rel = blk + qi - si
    mask = (rel >= 0) & (rel < blk) & ((si >= blk) | (n > 0))
    bias = jnp.where(mask, 0.0, -1e30)
    first = lax.broadcasted_iota(jnp.int32, (blk, LANES), 1) < LANES // 2
    zero = jnp.zeros((blk, LANES), BF16)

    heads = [None] * ATT_HEADS
    for g in range(ATT_KV_HEADS):
        kk = _both_halves(k, g // 2, g % 2)
        vv = _both_halves(v, g // 2, g % 2)
        hs = range(g * group, (g + 1) * group)
        qm = jnp.concatenate(
            [jnp.where(first if hd % 2 == 0 else ~first, q[:, (hd // 2) * LANES:(hd // 2 + 1) * LANES], zero)
             for hd in hs], axis=0)
        s3 = lax.dot_general(qm, kk, (((1,), (1,)), ((), ())), preferred_element_type=F32)
        es, denoms = [], []
        for r, hd in enumerate(hs):
            s = s3[r * blk:(r + 1) * blk] + bias
            sink = sink_ref[hd]
            m = jnp.maximum(jnp.max(s, axis=1, keepdims=True), sink)
            e = jnp.exp(s - m)
            denoms.append(jnp.sum(e, axis=1, keepdims=True) + jnp.exp(sink - m))
            es.append(e.astype(BF16))
        o3 = jnp.dot(jnp.concatenate(es, axis=0), vv, preferred_element_type=F32)
        for r, hd in enumerate(hs):
            heads[hd] = o3[r * blk:(r + 1) * blk] / denoms[r]
    cols = [jnp.where(first, heads[2 * c], heads[2 * c + 1]) for c in range(ATT_HEADS // 2)]
    o_ref[...] = jnp.concatenate(cols, axis=1).astype(o_ref.dtype)


def _attention(proj, sinks, batch, seq):
    t = proj.shape[0]
    blk = ATT_BLOCK
    nb = seq // blk
    cur = lambda b, n: b * nb + n
    prev = lambda b, n: b * nb + jnp.maximum(n - 1, 0)
    return pl.pallas_call(
        _attn_body,
        out_shape=jax.ShapeDtypeStruct((t, ATT_Q_W), BF16),
        grid=(batch, nb),
        in_specs=[
            pl.BlockSpec(memory_space=pltpu.SMEM),
            pl.BlockSpec((blk, ATT_Q_W), lambda b, n: (cur(b, n), _BLK_AQ)),
            pl.BlockSpec((blk, ATT_KV_W), lambda b, n: (cur(b, n), _BLK_AK)),
            pl.BlockSpec((blk, ATT_KV_W), lambda b, n: (prev(b, n), _BLK_AK)),
            pl.BlockSpec((blk, ATT_KV_W), lambda b, n: (cur(b, n), _BLK_AV)),
            pl.BlockSpec((blk, ATT_KV_W), lambda b, n: (prev(b, n), _BLK_AV)),
        ],
        out_specs=pl.BlockSpec((blk, ATT_Q_W), lambda b, n: (cur(b, n), 0)),
        compiler_params=_cparams("parallel", "arbitrary"),
        name="swa_attention",
    )(sinks, proj, proj, proj, proj, proj)


def _pool_body(u_ref, halo_ref, w_ref, scale_ref, o_ref):
    s_idx = pl.program_id(1)
    ts = u_ref.shape[0]
    u = u_ref[...].astype(F32)
    halo = jnp.where(s_idx > 0, halo_ref[...].astype(F32), 0.0)
    a1 = jnp.concatenate([halo, u], axis=0)
    length = ts + POOL_HALO
    a2 = a1[1:] + a1[:length - 1]
    a4 = a2[2:] + a2[:length - 3]
    a8 = a4[4:] + a4[:length - 7]
    a16 = a8[8:] + a8[:length - 15]
    sums = {2: a2[POOL_HALO - 1:], 4: a4[POOL_HALO - 3:], 8: a8[POOL_HALO - 7:], 16: a16[POOL_HALO - 15:]}

    tok = s_idx * ts + lax.broadcasted_iota(jnp.int32, (ts, 1), 0)
    chan = lax.broadcasted_iota(jnp.int32, (1, POOL_WIDTH), 1)
    d = jnp.zeros((ts, POOL_WIDTH), F32)
    for gi, w in enumerate(POOL_WINDOWS):
        inv_cnt = 1.0 / jnp.minimum(tok + 1, w).astype(F32)
        in_group = (chan >= gi * POOL_GROUP_DIM) & (chan < (gi + 1) * POOL_GROUP_DIM)
        d = jnp.where(in_group, sums[w] * inv_cnt, d)
    d = d - u
    y = jnp.dot(d.astype(BF16), w_ref[...], preferred_element_type=F32)
    o_ref[...] = (y * scale_ref[...]).astype(o_ref.dtype)


def _pool(proj, w_bd, scale, batch, seq, layer):
    t = proj.shape[0]
    ts = POOL_TS
    ns = seq // ts
    return pl.pallas_call(
        _pool_body,
        out_shape=jax.ShapeDtypeStruct((t, POOL_WIDTH), BF16),
        grid=(batch, ns),
        in_specs=[
            pl.BlockSpec((ts, POOL_WIDTH), lambda b, s: (b * ns + s, _BLK_PU)),
            pl.BlockSpec((POOL_HALO, POOL_WIDTH),
                         lambda b, s: (jnp.maximum((b * ns + s) * (ts // POOL_HALO) - 1, 0), _BLK_PU)),
            pl.BlockSpec((None, POOL_WIDTH, POOL_WIDTH), lambda b, s: (layer, 0, 0)),
            pl.BlockSpec((1, POOL_WIDTH), lambda b, s: (0, 0)),
        ],
        out_specs=pl.BlockSpec((ts, POOL_WIDTH), lambda b, s: (b * ns + s, 0)),
        compiler_params=_cparams("parallel", "arbitrary"),
        name="pool_mixer",
    )(proj, proj, w_bd, scale)


def _gla_body(qk_ref, v_ref, og_ref, gk_ref, norm_ref, tril_ref, bd_ref, o_ref, st_ref, oacc_ref):
    s_idx = pl.program_id(1)
    ts = qk_ref.shape[0]
    c = GLA_CHUNK

    @pl.when(s_idx == 0)
    def _():
        st_ref[...] = jnp.zeros_like(st_ref)

    qk = qk_ref[...].astype(F32)
    q = qk[:, :GLA_QK_W] * (GLA_DK ** -0.5)
    k = qk[:, GLA_QK_W:]
    v = v_ref[...]
    gk = gk_ref[...]
    g1 = gk.astype(BF16)
    r1 = gk - g1.astype(F32)
    g2 = r1.astype(BF16)
    g3 = (r1 - g2.astype(F32)).astype(BF16)
    tril = tril_ref[...]
    b = (jnp.dot(tril, g1, preferred_element_type=F32) + jnp.dot(tril, g2, preferred_element_type=F32)
         + jnp.dot(tril, g3, preferred_element_type=F32))
    q_t = (q * jnp.exp(b)).astype(BF16)
    k_t = (k * jnp.exp(-b)).astype(BF16)

    kcol = lax.broadcasted_iota(jnp.int32, (1, GLA_QK_W), 1)
    vcol = lax.broadcasted_iota(jnp.int32, (1, GLA_V_W), 1)
    ri = lax.broadcasted_iota(jnp.int32, (c, c), 0)
    ci = lax.broadcasted_iota(jnp.int32, (c, c), 1)
    causal = ri >= ci
    bd = bd_ref[...]

    for n in range(ts // c):
        rows = slice(n * c, (n + 1) * c)
        b_n = b[rows]
        b_last = b_n[c - 1:c, :]
        q_n = q_t[rows]
        k_n = k_t[rows]
        v_n = v[rows]
        k_s = (k[rows] * jnp.exp(b_last - b_n)).astype(BF16)
        q4 = jnp.concatenate(
            [jnp.where((kcol >= h * GLA_DK) & (kcol < (h + 1) * GLA_DK), q_n, jnp.zeros_like(q_n))
             for h in range(GLA_HEADS)], axis=0)
        a4 = lax.dot_general(q4, k_n, (((1,), (1,)), ((), ())), preferred_element_type=F32)
        o_n = jnp.zeros((c, GLA_V_W), F32)
        for h in range(GLA_HEADS):
            a_h = jnp.where(causal, a4[h * c:(h + 1) * c], 0.0).astype(BF16)
            o_h = jnp.dot(a_h, v_n, preferred_element_type=F32)
            o_n = jnp.where((vcol >= h * GLA_DV) & (vcol < (h + 1) * GLA_DV), o_h, o_n)
        st = st_ref[...]
        o_n = o_n + lax.dot_general(q_n, st.astype(BF16), (((1,), (1,)), ((), ())),
                                    preferred_element_type=F32)
        kv = lax.dot_general(v_n, k_s, (((0,), (0,)), ((), ())), preferred_element_type=F32)
        st_ref[...] = st * jnp.exp(b_last) + kv * bd
        oacc_ref[rows, :] = o_n

    o = oacc_ref[...]
    o2 = o * o
    inv = jnp.zeros_like(o)
    for h in range(GLA_HEADS):
        in_head = (vcol >= h * GLA_DV) & (vcol < (h + 1) * GLA_DV)
        ms = jnp.sum(jnp.where(in_head, o2, 0.0), axis=1, keepdims=True) * (1.0 / GLA_DV)
        inv = jnp.where(in_head, lax.rsqrt(ms + EPS), inv)
    og = og_ref[...].astype(F32)
    o_ref[...] = (o * inv * norm_ref[...] * (og * jax.nn.sigmoid(og))).astype(o_ref.dtype)


def _gla(proj, gk, norm, batch, seq):
    t = proj.shape[0]
    ts = GLA_TS
    ns = seq // ts
    r = np.arange(ts)
    tril = ((r[:, None] // GLA_CHUNK == r[None, :] // GLA_CHUNK) & (r[:, None] >= r[None, :]))
    tril = jnp.asarray(tril, BF16)
    bd = (np.arange(GLA_V_W)[:, None] // GLA_DV) == (np.arange(GLA_QK_W)[None, :] // GLA_DK)
    bd = jnp.asarray(bd, F32)
    row = lambda b, s: b * ns + s
    return pl.pallas_call(
        _gla_body,
        out_shape=jax.ShapeDtypeStruct((t, GLA_V_W), BF16),
        grid=(batch, ns),
        in_specs=[
            pl.BlockSpec((ts, 2 * GLA_QK_W), lambda b, s: (row(b, s), _BLK_GQK)),
            pl.BlockSpec((ts, GLA_V_W), lambda b, s: (row(b, s), _BLK_GV)),
            pl.BlockSpec((ts, GLA_V_W), lambda b, s: (row(b, s), _BLK_GO)),
            pl.BlockSpec((ts, GLA_QK_W), lambda b, s: (row(b, s), 0)),
            pl.BlockSpec((1, GLA_V_W), lambda b, s: (0, 0)),
            pl.BlockSpec((ts, ts), lambda b, s: (0, 0)),
            pl.BlockSpec((GLA_V_W, GLA_QK_W), lambda b, s: (0, 0)),
        ],
        out_specs=pl.BlockSpec((ts, GLA_V_W), lambda b, s: (row(b, s), 0)),
        scratch_shapes=[pltpu.VMEM((GLA_V_W, GLA_QK_W), F32), pltpu.VMEM((ts, GLA_V_W), F32)],
        compiler_params=_cparams("parallel", "arbitrary"),
        name="gla",
    )(proj, proj, proj, gk, norm, tril, bd)


def _merge_body(h_ref, ya_ref, yp_ref, yg_ref, wga_ref, wgp_ref, wgg_ref, tga_ref, tgp_ref, tgg_ref,
                bga_ref, bgp_ref, bgg_ref, wa_ref, wp_ref, wg_ref, o_ref, wgate_ref):
    tn = o_ref.shape[1]

    @pl.when(pl.program_id(1) == 0)
    def _():
        for br, (win_ref, tail_ref) in enumerate(((wga_ref, tga_ref), (wgp_ref, tgp_ref),
                                                  (wgg_ref, tgg_ref))):
            win = jnp.concatenate([win_ref[...], tail_ref[...]], axis=1).astype(F32)
            wgate_ref[br] = win[:, GATE_LEAD:GATE_LEAD + tn].astype(BF16)

    h = h_ref[...]
    acc = None
    for br, (y_ref, bgate_ref, wbr_ref) in enumerate(((ya_ref, bga_ref, wa_ref),
                                                     (yp_ref, bgp_ref, wp_ref),
                                                     (yg_ref, bgg_ref, wg_ref))):
        gate = jax.nn.sigmoid(jnp.dot(h, wgate_ref[br], preferred_element_type=F32) + bgate_ref[...])
        term = gate * jnp.dot(y_ref[...], wbr_ref[...], preferred_element_type=F32)
        acc = term if acc is None else acc + term
    o_ref[...] = acc.astype(o_ref.dtype)


def _merge(h, y_att, y_pool, y_gla, w_in, b_gate, w_br_att, w_br_pool, w_br_gla, layer):
    t = h.shape[0]
    tm, tn = MERGE_TM, MERGE_TN
    nn = D_MODEL // tn
    row = lambda j, i: (i, 0)
    y_spec = pl.BlockSpec((tm, ATT_Q_W), row)
    gate_base = _IN_GATE - GATE_LEAD
    gate_specs = [pl.BlockSpec((pl.Squeezed(), pl.Element(D_MODEL), pl.Element(tn)),
                               functools.partial(
                                   lambda j, i, br: (layer, 0, pl.multiple_of(
                                       gate_base + br * D_MODEL + j * tn, LANES)), br=br))
                  for br in range(N_BRANCH)]
    tail_specs = [pl.BlockSpec((None, D_MODEL, LANES),
                               functools.partial(
                                   lambda j, i, br: (layer, 0, (gate_base + br * D_MODEL) // LANES
                                                     + (j + 1) * (tn // LANES)), br=br))
                  for br in range(N_BRANCH)]
    bias_specs = [pl.BlockSpec((1, tn), functools.partial(lambda j, i, br: (0, br * nn + j), br=br))
                  for br in range(N_BRANCH)]
    br_spec = pl.BlockSpec((None, ATT_Q_W, tn), lambda j, i: (layer, 0, j))
    return pl.pallas_call(
        _merge_body,
        out_shape=jax.ShapeDtypeStruct((t, D_MODEL), BF16),
        grid=(nn, t // tm),
        in_specs=[pl.BlockSpec((tm, D_MODEL), row), y_spec, y_spec, y_spec,
                  *gate_specs, *tail_specs, *bias_specs, br_spec, br_spec, br_spec],
        out_specs=pl.BlockSpec((tm, tn), lambda j, i: (i, j)),
        scratch_shapes=[pltpu.VMEM((N_BRANCH, D_MODEL, tn), BF16)],
        compiler_params=_cparams("arbitrary", "arbitrary"),
        name="gated_merge",
    )(h, y_att, y_pool, y_gla, w_in, w_in, w_in, w_in, w_in, w_in, b_gate, b_gate, b_gate,
      w_br_att, w_br_pool, w_br_gla)


def _outproj_body(m_ref, w_ref, x_ref, o_ref):
    o_ref[...] = x_ref[...] + jnp.dot(m_ref[...], w_ref[...], preferred_element_type=F32)


def _outproj(merged, w_out, x, layer):
    t = x.shape[0]
    tm, tn = OUT_TM, OUT_TN
    return pl.pallas_call(
        _outproj_body,
        out_shape=jax.ShapeDtypeStruct((t, D_MODEL), F32),
        grid=(t // tm, D_MODEL // tn),
        in_specs=[
            pl.BlockSpec((tm, D_MODEL), lambda i, j: (i, 0)),
            pl.BlockSpec((None, D_MODEL, tn), lambda i, j: (layer, 0, j)),
            pl.BlockSpec((tm, tn), lambda i, j: (i, j)),
        ],
        out_specs=pl.BlockSpec((tm, tn), lambda i, j: (i, j)),
        compiler_params=_cparams("parallel", "arbitrary"),
        name="outproj",
    )(merged, w_out, x)


def _prep_mixer_weights(w_in, w_pool, w_gla_a2, w_br_att, w_br_pool, w_br_gla, w_out):
    depth = w_in.shape[0]
    w_in_bf = w_in.astype(BF16)
    w_a2 = jnp.pad(w_gla_a2.astype(BF16), ((0, 0), (0, LANES - GLA_LOWRANK), (0, 0)))
    groups = len(POOL_WINDOWS)
    eye = jnp.eye(groups, dtype=BF16)
    w_bd = (w_pool.astype(BF16)[:, :, :, None, :] * eye[None, :, None, :, None]).reshape(
        depth, POOL_WIDTH, POOL_WIDTH)
    return dict(w_in=w_in_bf, w_a2=w_a2, w_bd=w_bd,
                w_br_att=w_br_att.astype(BF16), w_br_pool=w_br_pool.astype(BF16),
                w_br_gla=w_br_gla.astype(BF16), w_out=w_out.astype(BF16))


def _mixer(x, cos_t, sin_t, batch, seq, layer, mw, norm_mix, b_gate, att_sinks, pool_scale,
           b_gla_a, gla_norm):
    proj, gk, h = _inproj(x, norm_mix[None, :], mw["w_in"], mw["w_a2"], b_gla_a[None, :],
                          cos_t, sin_t, layer)
    y_att = _attention(proj, att_sinks, batch, seq)
    y_pool = _pool(proj, mw["w_bd"], pool_scale[None, :], batch, seq, layer)
    y_gla = _gla(proj, gk, gla_norm[None, :], batch, seq)
    merged = _merge(h, y_att, y_pool, y_gla, mw["w_in"], b_gate[None, :],
                    mw["w_br_att"], mw["w_br_pool"], mw["w_br_gla"], layer)
    return _outproj(merged, mw["w_out"], x, layer)


def kernel(x, positions, norm_ffn1, ffn1_wi, ffn1_wo, norm_mix, w_in, b_gate, att_sinks, w_pool, pool_scale, w_gla_a2, b_gla_a, gla_norm, w_br_att, w_br_pool, w_br_gla, w_out, norm_ffn2, ffn2_wi, ffn2_wo, norm_final):
    batch, seq, d = x.shape
    depth = norm_ffn1.shape[0]
    assert d == D_MODEL and seq % max(POOL_TS, GLA_TS, ATT_BLOCK) == 0
    t = batch * seq
    assert t % max(FFN_TM, MERGE_TM, OUT_TM, ROPE_TM, INPROJ_TM) == 0
    xt = x.reshape(t, d)
    cos_t, sin_t = _rope_tables(positions)
    gf = norm_final[None, :]
    mw = _prep_mixer_weights(w_in, w_pool, w_gla_a2, w_br_att, w_br_pool, w_br_gla, w_out)
    for l in range(depth):
        xt = _ffn(xt, norm_ffn1[l][None, :], ffn1_wi, ffn1_wo, gf, l, False)
        xt = _mixer(xt, cos_t, sin_t, batch, seq, l, mw, norm_mix[l], b_gate[l], att_sinks[l],
                    pool_scale[l], b_gla_a[l], gla_norm[l])
        xt = _ffn(xt, norm_ffn2[l][None, :], ffn2_wi, ffn2_wo, gf, l, l == depth - 1)
    return xt.reshape(batch, seq, d)
```

```python
import functools

import numpy as np
import jax
import jax.numpy as jnp
from jax import lax
from jax.experimental import pallas as pl
from jax.experimental.pallas import tpu as pltpu

F32 = jnp.float32
BF16 = jnp.bfloat16

D_MODEL = 2048
D_FF = 5632
ATT_HEADS = 12
ATT_KV_HEADS = 4
HEAD_DIM = 64
ATT_BLOCK = 128
ROPE_THETA = 10000.0
ATT_Q_W = ATT_HEADS * HEAD_DIM
ATT_KV_W = ATT_KV_HEADS * HEAD_DIM
POOL_WINDOWS = (2, 4, 8, 16)
POOL_GROUP_DIM = 192
POOL_WIDTH = 768
GLA_HEADS = 4
GLA_DK = 96
GLA_DV = 192
GLA_QK_W = GLA_HEADS * GLA_DK
GLA_V_W = GLA_HEADS * GLA_DV
GLA_LOWRANK = 16
GLA_TAU = 16.0
GLA_CHUNK = 64
N_BRANCH = 3
EPS = 1e-6

_IN_QA, _IN_KA, _IN_VA, _IN_PU = 0, 768, 1024, 1280
_IN_QG, _IN_KG, _IN_VG, _IN_OG = 2048, 2432, 2816, 3584
_IN_LR = 4352
_IN_GATE = 4368

PROJ_SRC_W = 4352
PROJ_W = 3584
_BLK_GQK, _BLK_AQ, _BLK_GV, _BLK_GO = 0, 1, 2, 3
_BLK_AK, _BLK_AV = 12, 13
_PROJ_SEGMENTS = ((0, _IN_QG, 768, None), (768, _IN_QA, 768, HEAD_DIM ** -0.5),
                  (1536, _IN_VG, 1536, None), (3072, _IN_KA, 256, 1.0), (3328, _IN_VA, 256, None))
W_IN_COLS = _IN_GATE + N_BRANCH * D_MODEL
GATE_LEAD = _IN_GATE % 128

LANES = 128
POOL_HALO = 16
VMEM_LIMIT = 58 * 1024 * 1024

FFN_TM, FFN_TF = 1024, 256
FFN_UNROLL = 4
INPROJ_TM = 512
INPROJ_NCH = 1024
GLA_TS = 256
MERGE_TM, MERGE_TN = 1024, 512
OUT_TM, OUT_TN = 512, 2048
ROPE_TM = 1024


def _cparams(*sem):
    return pltpu.CompilerParams(dimension_semantics=sem, vmem_limit_bytes=VMEM_LIMIT)


def _rms(x, g):
    return x * lax.rsqrt(jnp.mean(x * x, axis=-1, keepdims=True) + EPS) * g


def _ffn_body(x_ref, g_ref, gf_ref, wi_hbm, wo_hbm, o_ref, h_ref, u_ref, wa_buf, wb_buf, wo_buf, sem,
              *, layer, n_tiles, final_norm):
    tf = FFN_TF
    nch = D_FF // tf
    i = pl.program_id(0)

    def up_copies(c, slot):
        col = pl.multiple_of(c * tf, tf)
        return (pltpu.make_async_copy(wi_hbm.at[layer, :, pl.ds(col, tf)], wa_buf.at[slot], sem.at[0, slot]),
                pltpu.make_async_copy(wi_hbm.at[layer, :, pl.ds(D_FF + col, tf)], wb_buf.at[slot],
                                      sem.at[1, slot]))

    def down_copy(c, slot):
        row = pl.multiple_of(c * tf, tf)
        return pltpu.make_async_copy(wo_hbm.at[layer, pl.ds(row, tf), :], wo_buf.at[slot], sem.at[2, slot])

    def up(slot):
        h = h_ref[...]
        a = jnp.dot(h, wa_buf[slot].astype(BF16), preferred_element_type=F32)
        b = jnp.dot(h, wb_buf[slot].astype(BF16), preferred_element_type=F32)
        u_ref[slot] = (a * jax.nn.sigmoid(a) * (0.5 * b)).astype(BF16)

    def down(slot):
        o_ref[...] += jnp.dot(u_ref[slot], wo_buf[slot].astype(BF16), preferred_element_type=F32)

    def start_up(c, slot):
        ca, cb = up_copies(c, slot)
        ca.start(priority=1)
        cb.start(priority=1)

    def stage(c, slot):
        for cp in up_copies(c, slot):
            cp.wait()
        down_copy(c - 1, 1 - slot).wait()
        down_copy(c, slot).start(priority=1)
        down(1 - slot)
        up(slot)

    @pl.when(i == 0)
    def _():
        start_up(0, 0)

    x = x_ref[...]
    h_ref[...] = _rms(x, g_ref[...]).astype(BF16)
    o_ref[...] = x
    for cp in up_copies(0, 0):
        cp.wait()
    start_up(1, 1)
    down_copy(0, 0).start(priority=1)
    up(0)

    def stages(k, carry):
        c0 = FFN_UNROLL * k + 1
        for s in range(FFN_UNROLL):
            slot = (1 + s) % 2
            start_up(c0 + s + 1, 1 - slot)
            stage(c0 + s, slot)
        return carry

    lax.fori_loop(0, (nch - 2) // FFN_UNROLL, stages, 0)

    stage(nch - 1, 1)

    @pl.when(i + 1 < n_tiles)
    def _():
        start_up(0, 0)

    down_copy(nch - 1, 1).wait()
    down(1)
    if final_norm:
        o_ref[...] = _rms(o_ref[...], gf_ref[...])


def _ffn(x, g, wi, wo, gf, layer, final_norm):
    t = x.shape[0]
    tm, tf = FFN_TM, FFN_TF
    assert FFN_UNROLL % 2 == 0 and (D_FF // tf - 2) % FFN_UNROLL == 0
    n_tiles = t // tm
    return pl.pallas_call(
        functools.partial(_ffn_body, layer=layer, n_tiles=n_tiles, final_norm=final_norm),
        out_shape=jax.ShapeDtypeStruct((t, D_MODEL), F32),
        grid=(n_tiles,),
        in_specs=[
            pl.BlockSpec((tm, D_MODEL), lambda i: (i, 0)),
            pl.BlockSpec((1, D_MODEL), lambda i: (0, 0)),
            pl.BlockSpec((1, D_MODEL), lambda i: (0, 0)),
            pl.BlockSpec(memory_space=pl.ANY),
            pl.BlockSpec(memory_space=pl.ANY),
        ],
        out_specs=pl.BlockSpec((tm, D_MODEL), lambda i: (i, 0)),
        scratch_shapes=[
            pltpu.VMEM((tm, D_MODEL), BF16),
            pltpu.VMEM((2, tm, tf), BF16),
            pltpu.VMEM((2, D_MODEL, tf), F32),
            pltpu.VMEM((2, D_MODEL, tf), F32),
            pltpu.VMEM((2, tf, D_MODEL), F32),
            pltpu.SemaphoreType.DMA((3, 2)),
        ],
        compiler_params=_cparams("arbitrary"),
        name="ffn_final" if final_norm else "ffn",
    )(x, g, gf, wi, wo)


def _rope_table_body(pos_ref, freq_ref, sign_ref, cos_ref, sin_ref):
    ang = pos_ref[...].astype(F32) * freq_ref[...]
    cos_ref[...] = jnp.cos(ang)
    sin_ref[...] = jnp.sin(ang) * sign_ref[...]


def _rope_tables(positions):
    t = positions.size
    half = HEAD_DIM // 2
    inv_freq = ROPE_THETA ** (-jnp.arange(half, dtype=F32) / half)
    freq = jnp.tile(inv_freq, LANES // half)[None, :]
    sign = jnp.tile(jnp.concatenate([-jnp.ones((half,), F32), jnp.ones((half,), F32)]),
                    LANES // HEAD_DIM)[None, :]
    tm = ROPE_TM
    return pl.pallas_call(
        _rope_table_body,
        out_shape=(jax.ShapeDtypeStruct((t, LANES), F32), jax.ShapeDtypeStruct((t, LANES), F32)),
        grid=(t // tm,),
        in_specs=[
            pl.BlockSpec((tm, 1), lambda i: (i, 0)),
            pl.BlockSpec((1, LANES), lambda i: (0, 0)),
            pl.BlockSpec((1, LANES), lambda i: (0, 0)),
        ],
        out_specs=(pl.BlockSpec((tm, LANES), lambda i: (i, 0)),
                   pl.BlockSpec((tm, LANES), lambda i: (i, 0))),
        compiler_params=_cparams("parallel"),
        name="rope_tables",
    )(positions.reshape(t, 1), freq, sign)


def _swap_halves(x):
    lane = lax.broadcasted_iota(jnp.int32, x.shape, 1)
    first = (lane % HEAD_DIM) < (HEAD_DIM // 2)
    return jnp.where(first, pltpu.roll(x, LANES - HEAD_DIM // 2, 1), pltpu.roll(x, HEAD_DIM // 2, 1))


def _rope(x, cos, sin):
    cols = []
    for c in range(x.shape[1] // LANES):
        xc = x[:, c * LANES:(c + 1) * LANES]
        cols.append(xc * cos + _swap_halves(xc) * sin)
    return jnp.concatenate(cols, axis=1)


def _pool_mix(u, halo, tok0):
    ts = u.shape[0]
    a1 = jnp.concatenate([halo, u], axis=0)
    length = ts + POOL_HALO
    a2 = a1[1:] + a1[:length - 1]
    a4 = a2[2:] + a2[:length - 3]
    a8 = a4[4:] + a4[:length - 7]
    a16 = a8[8:] + a8[:length - 15]
    sums = {2: a2[POOL_HALO - 1:], 4: a4[POOL_HALO - 3:], 8: a8[POOL_HALO - 7:], 16: a16[POOL_HALO - 15:]}

    tok = tok0 + lax.broadcasted_iota(jnp.int32, (ts, 1), 0)
    chan = lax.broadcasted_iota(jnp.int32, (1, POOL_WIDTH), 1)
    d = jnp.zeros((ts, POOL_WIDTH), F32)
    for gi, w in enumerate(POOL_WINDOWS):
        inv_cnt = 1.0 / jnp.minimum(tok + 1, w).astype(F32)
        in_group = (chan >= gi * POOL_GROUP_DIM) & (chan < (gi + 1) * POOL_GROUP_DIM)
        d = jnp.where(in_group, sums[w] * inv_cnt, d)
    return d - u


def _inproj_body(x_ref, g_ref, w_ref, wlr_ref, wa2_ref, ba_ref, cos_ref, sin_ref, wbd_ref, ps_ref,
                 proj_ref, gk_ref, h_ref, ypool_ref, halo_ref, *, tiles_per_seq):
    tm = x_ref.shape[0]
    h = _rms(x_ref[...], g_ref[...]).astype(BF16)
    h_ref[...] = h
    s_idx = pl.program_id(0) % tiles_per_seq

    @pl.when(s_idx == 0)
    def _():
        halo_ref[...] = jnp.zeros_like(halo_ref)

    pu = jnp.dot(h, w_ref[:, _IN_PU:_IN_PU + POOL_WIDTH], preferred_element_type=F32)
    d = _pool_mix(pu, halo_ref[...], s_idx * tm).astype(BF16)
    halo_ref[...] = pu[tm - POOL_HALO:]

    lr = jnp.dot(h, wlr_ref[...], preferred_element_type=F32).astype(BF16)

    cos, sin = cos_ref[...], sin_ref[...]
    for dst, src, width, rot_scale in _PROJ_SEGMENTS:
        p = jnp.dot(h, w_ref[:, src:src + width], preferred_element_type=F32)
        if rot_scale is not None:
            p = _rope(p, cos * rot_scale, sin * rot_scale)
        proj_ref[:, dst:dst + width] = p.astype(proj_ref.dtype)

    z = jnp.dot(lr, wa2_ref[...], preferred_element_type=F32) + ba_ref[...]
    gk_ref[...] = (jnp.minimum(z, 0.0) - jnp.log1p(jnp.exp(-jnp.abs(z)))) * (1.0 / GLA_TAU)
    y = jnp.dot(d, wbd_ref[...], preferred_element_type=F32)
    ypool_ref[...] = (y * ps_ref[...]).astype(ypool_ref.dtype)


def _inproj(x, g, w_in, w_a2, b_a, cos_t, sin_t, w_bd, pool_scale, seq, layer):
    t = x.shape[0]
    tm = INPROJ_TM
    assert seq % tm == 0
    const = lambda i: (0, 0)
    lconst = lambda i: (layer, 0, 0)
    rows = lambda i: (i, 0)
    return pl.pallas_call(
        functools.partial(_inproj_body, tiles_per_seq=seq // tm),
        out_shape=(jax.ShapeDtypeStruct((t, PROJ_W), BF16),
                   jax.ShapeDtypeStruct((t, GLA_QK_W), F32),
                   jax.ShapeDtypeStruct((t, D_MODEL), BF16),
                   jax.ShapeDtypeStruct((t, POOL_WIDTH), BF16)),
        grid=(t // tm,),
        in_specs=[
            pl.BlockSpec((tm, D_MODEL), rows),
            pl.BlockSpec((1, D_MODEL), const),
            pl.BlockSpec((None, D_MODEL, PROJ_SRC_W), lconst, pipeline_mode=pl.Buffered(1)),
            pl.BlockSpec((None, D_MODEL, LANES), lambda i: (layer, 0, _IN_LR // LANES)),
            pl.BlockSpec((None, LANES, GLA_QK_W), lconst),
            pl.BlockSpec((1, GLA_QK_W), const),
            pl.BlockSpec((tm, LANES), rows),
            pl.BlockSpec((tm, LANES), rows),
            pl.BlockSpec((None, POOL_WIDTH, POOL_WIDTH), lconst),
            pl.BlockSpec((1, POOL_WIDTH), const),
        ],
        out_specs=(pl.BlockSpec((tm, PROJ_W), rows),
                   pl.BlockSpec((tm, GLA_QK_W), rows),
                   pl.BlockSpec((tm, D_MODEL), rows),
                   pl.BlockSpec((tm, POOL_WIDTH), rows)),
        scratch_shapes=[pltpu.VMEM((POOL_HALO, POOL_WIDTH), F32)],
        compiler_params=_cparams("arbitrary"),
        name="inproj",
    )(x, g, w_in, w_in, w_a2, b_a, cos_t, sin_t, w_bd, pool_scale)


def _both_halves(x, col, half):
    xc = x[:, col * LANES:(col + 1) * LANES].astype(F32)
    sw = pltpu.roll(xc, LANES // 2, 1)
    first = lax.broadcasted_iota(jnp.int32, xc.shape, 1) < LANES // 2
    dup = jnp.where(first, xc, sw) if half == 0 else jnp.where(first, sw, xc)
    return dup.astype(BF16)


def _attn_body(sink_ref, q_ref, kc_ref, kp_ref, vc_ref, vp_ref, o_ref):
    n = pl.program_id(1)
    blk = ATT_BLOCK
    group = ATT_HEADS // ATT_KV_HEADS
    q = q_ref[...]
    k = jnp.concatenate([kp_ref[...], kc_ref[...]], axis=0)
    v = jnp.concatenate([vp_ref[...], vc_ref[...]], axis=0)

    qi = lax.broadcasted_iota(jnp.int32, (blk, 2 * blk), 0)
    si = lax.broadcasted_iota(jnp.int32, (blk, 2 * blk), 1)
    rel = blk + qi - si
    mask = (rel >= 0) & (rel < blk) & ((si >= blk) | (n > 0))
    bias = jnp.where(mask, 0.0, -1e30)
    first = lax.broadcasted_iota(jnp.int32, (blk, LANES), 1) < LANES // 2
    zero = jnp.zeros((blk, LANES), BF16)

    heads = [None] * ATT_HEADS
    for g in range(ATT_KV_HEADS):
        kk = _both_halves(k, g // 2, g % 2)
        vv = _both_halves(v, g // 2, g % 2)
        hs = range(g * group, (g + 1) * group)
        qm = jnp.concatenate(
            [jnp.where(first if hd % 2 == 0 else ~first, q[:, (hd // 2) * LANES:(hd // 2 + 1) * LANES], zero)
             for hd in hs], axis=0)
        s3 = lax.dot_general(qm, kk, (((1,), (1,)), ((), ())), preferred_element_type=F32)
        es, denoms = [], []
        for r, hd in enumerate(hs):
            s = s3[r * blk:(r + 1) * blk] + bias
            sink = sink_ref[hd]
            m = jnp.maximum(jnp.max(s, axis=1, keepdims=True), sink)
            e = jnp.exp(s - m)
            denoms.append(jnp.sum(e, axis=1, keepdims=True) + jnp.exp(sink - m))
            es.append(e.astype(BF16))
        o3 = jnp.dot(jnp.concatenate(es, axis=0), vv, preferred_element_type=F32)
        for r, hd in enumerate(hs):
            heads[hd] = o3[r * blk:(r + 1) * blk] / denoms[r]
    cols = [jnp.where(first, heads[2 * c], heads[2 * c + 1]) for c in range(ATT_HEADS // 2)]
    o_ref[...] = jnp.concatenate(cols, axis=1).astype(o_ref.dtype)


def _attention(proj, sinks, batch, seq):
    t = proj.shape[0]
    blk = ATT_BLOCK
    nb = seq // blk
    cur = lambda b, n: b * nb + n
    prev = lambda b, n: b * nb + jnp.maximum(n - 1, 0)
    return pl.pallas_call(
        _attn_body,
        out_shape=jax.ShapeDtypeStruct((t, ATT_Q_W), BF16),
        grid=(batch, nb),
        in_specs=[
            pl.BlockSpec(memory_space=pltpu.SMEM),
            pl.BlockSpec((blk, ATT_Q_W), lambda b, n: (cur(b, n), _BLK_AQ)),
            pl.BlockSpec((blk, ATT_KV_W), lambda b, n: (cur(b, n), _BLK_AK)),
            pl.BlockSpec((blk, ATT_KV_W), lambda b, n: (prev(b, n), _BLK_AK)),
            pl.BlockSpec((blk, ATT_KV_W), lambda b, n: (cur(b, n), _BLK_AV)),
            pl.BlockSpec((blk, ATT_KV_W), lambda b, n: (prev(b, n), _BLK_AV)),
        ],
        out_specs=pl.BlockSpec((blk, ATT_Q_W), lambda b, n: (cur(b, n), 0)),
        compiler_params=_cparams("parallel", "arbitrary"),
        name="swa_attention",
    )(sinks, proj, proj, proj, proj, proj)


def _gla_body(qk_ref, v_ref, og_ref, gk_ref, norm_ref, tril_ref, bd_ref, o_ref, st_ref, oacc_ref):
    s_idx = pl.program_id(1)
    ts = qk_ref.shape[0]
    c = GLA_CHUNK

    @pl.when(s_idx == 0)
    def _():
        st_ref[...] = jnp.zeros_like(st_ref)

    qk = qk_ref[...].astype(F32)
    q = qk[:, :GLA_QK_W] * (GLA_DK ** -0.5)
    k = qk[:, GLA_QK_W:]
    v = v_ref[...]
    gk = gk_ref[...]
    g1 = gk.astype(BF16)
    r1 = gk - g1.astype(F32)
    g2 = r1.astype(BF16)
    g3 = (r1 - g2.astype(F32)).astype(BF16)
    tril = tril_ref[...]
    b = (jnp.dot(tril, g1, preferred_element_type=F32) + jnp.dot(tril, g2, preferred_element_type=F32)
         + jnp.dot(tril, g3, preferred_element_type=F32))
    q_t = (q * jnp.exp(b)).astype(BF16)
    k_t = (k * jnp.exp(-b)).astype(BF16)

    kcol = lax.broadcasted_iota(jnp.int32, (1, GLA_QK_W), 1)
    vcol = lax.broadcasted_iota(jnp.int32, (1, GLA_V_W), 1)
    ri = lax.broadcasted_iota(jnp.int32, (c, c), 0)
    ci = lax.broadcasted_iota(jnp.int32, (c, c), 1)
    causal = ri >= ci
    bd = bd_ref[...]

    for n in range(ts // c):
        rows = slice(n * c, (n + 1) * c)
        b_n = b[rows]
        b_last = b_n[c - 1:c, :]
        q_n = q_t[rows]
        k_n = k_t[rows]
        v_n = v[rows]
        k_s = (k[rows] * jnp.exp(b_last - b_n)).astype(BF16)
        q4 = jnp.concatenate(
            [jnp.where((kcol >= h * GLA_DK) & (kcol < (h + 1) * GLA_DK), q_n, jnp.zeros_like(q_n))
             for h in range(GLA_HEADS)], axis=0)
        a4 = lax.dot_general(q4, k_n, (((1,), (1,)), ((), ())), preferred_element_type=F32)
        o_n = jnp.zeros((c, GLA_V_W), F32)
        for h in range(GLA_HEADS):
            a_h = jnp.where(causal, a4[h * c:(h + 1) * c], 0.0).astype(BF16)
            o_h = jnp.dot(a_h, v_n, preferred_element_type=F32)
            o_n = jnp.where((vcol >= h * GLA_DV) & (vcol < (h + 1) * GLA_DV), o_h, o_n)
        st = st_ref[...]
        o_n = o_n + lax.dot_general(q_n, st.astype(BF16), (((1,), (1,)), ((), ())),
                                    preferred_element_type=F32)
        kv = lax.dot_general(v_n, k_s, (((0,), (0,)), ((), ())), preferred_element_type=F32)
        st_ref[...] = st * jnp.exp(b_last) + kv * bd
        oacc_ref[rows, :] = o_n

    o = oacc_ref[...]
    o2 = o * o
    inv = jnp.zeros_like(o)
    for h in range(GLA_HEADS):
        in_head = (vcol >= h * GLA_DV) & (vcol < (h + 1) * GLA_DV)
        ms = jnp.sum(jnp.where(in_head, o2, 0.0), axis=1, keepdims=True) * (1.0 / GLA_DV)
        inv = jnp.where(in_head, lax.rsqrt(ms + EPS), inv)
    og = og_ref[...].astype(F32)
    o_ref[...] = (o * inv * norm_ref[...] * (og * jax.nn.sigmoid(og))).astype(o_ref.dtype)


def _gla(proj, gk, norm, batch, seq):
    t = proj.shape[0]
    ts = GLA_TS
    ns = seq // ts
    r = np.arange(ts)
    tril = ((r[:, None] // GLA_CHUNK == r[None, :] // GLA_CHUNK) & (r[:, None] >= r[None, :]))
    tril = jnp.asarray(tril, BF16)
    bd = (np.arange(GLA_V_W)[:, None] // GLA_DV) == (np.arange(GLA_QK_W)[None, :] // GLA_DK)
    bd = jnp.asarray(bd, F32)
    row = lambda b, s: b * ns + s
    return pl.pallas_call(
        _gla_body,
        out_shape=jax.ShapeDtypeStruct((t, GLA_V_W), BF16),
        grid=(batch, ns),
        in_specs=[
            pl.BlockSpec((ts, 2 * GLA_QK_W), lambda b, s: (row(b, s), _BLK_GQK)),
            pl.BlockSpec((ts, GLA_V_W), lambda b, s: (row(b, s), _BLK_GV)),
            pl.BlockSpec((ts, GLA_V_W), lambda b, s: (row(b, s), _BLK_GO)),
            pl.BlockSpec((ts, GLA_QK_W), lambda b, s: (row(b, s), 0)),
            pl.BlockSpec((1, GLA_V_W), lambda b, s: (0, 0)),
            pl.BlockSpec((ts, ts), lambda b, s: (0, 0)),
            pl.BlockSpec((GLA_V_W, GLA_QK_W), lambda b, s: (0, 0)),
        ],
        out_specs=pl.BlockSpec((ts, GLA_V_W), lambda b, s: (row(b, s), 0)),
        scratch_shapes=[pltpu.VMEM((GLA_V_W, GLA_QK_W), F32), pltpu.VMEM((ts, GLA_V_W), F32)],
        compiler_params=_cparams("parallel", "arbitrary"),
        name="gla",
    )(proj, proj, proj, gk, norm, tril, bd)


def _merge_body(h_ref, ya_ref, yp_ref, yg_ref, wga_ref, wgp_ref, wgg_ref, tga_ref, tgp_ref, tgg_ref,
                bga_ref, bgp_ref, bgg_ref, wa_ref, wp_ref, wg_ref, o_ref, wgate_ref):
    tn = o_ref.shape[1]

    @pl.when(pl.program_id(1) == 0)
    def _():
        for br, (win_ref, tail_ref) in enumerate(((wga_ref, tga_ref), (wgp_ref, tgp_ref),
                                                  (wgg_ref, tgg_ref))):
            win = jnp.concatenate([win_ref[...], tail_ref[...]], axis=1).astype(F32)
            wgate_ref[br] = win[:, GATE_LEAD:GATE_LEAD + tn].astype(BF16)

    h = h_ref[...]
    acc = None
    for br, (y_ref, bgate_ref, wbr_ref) in enumerate(((ya_ref, bga_ref, wa_ref),
                                                     (yp_ref, bgp_ref, wp_ref),
                                                     (yg_ref, bgg_ref, wg_ref))):
        gate = jax.nn.sigmoid(jnp.dot(h, wgate_ref[br], preferred_element_type=F32) + bgate_ref[...])
        term = gate * jnp.dot(y_ref[...], wbr_ref[...], preferred_element_type=F32)
        acc = term if acc is None else acc + term
    o_ref[...] = acc.astype(o_ref.dtype)


def _merge(h, y_att, y_pool, y_gla, w_in, b_gate, w_br_att, w_br_pool, w_br_gla, layer):
    t = h.shape[0]
    tm, tn = MERGE_TM, MERGE_TN
    nn = D_MODEL // tn
    row = lambda j, i: (i, 0)
    y_spec = pl.BlockSpec((tm, ATT_Q_W), row)
    gate_base = _IN_GATE - GATE_LEAD
    gate_specs = [pl.BlockSpec((pl.Squeezed(), pl.Element(D_MODEL), pl.Element(tn)),
                               functools.partial(
                                   lambda j, i, br: (layer, 0, pl.multiple_of(
                                       gate_base + br * D_MODEL + j * tn, LANES)), br=br))
                  for br in range(N_BRANCH)]
    tail_specs = [pl.BlockSpec((None, D_MODEL, LANES),
                               functools.partial(
                                   lambda j, i, br: (layer, 0, (gate_base + br * D_MODEL) // LANES
                                                     + (j + 1) * (tn // LANES)), br=br))
                  for br in range(N_BRANCH)]
    bias_specs = [pl.BlockSpec((1, tn), functools.partial(lambda j, i, br: (0, br * nn + j), br=br))
                  for br in range(N_BRANCH)]
    br_spec = pl.BlockSpec((None, ATT_Q_W, tn), lambda j, i: (layer, 0, j))
    return pl.pallas_call(
        _merge_body,
        out_shape=jax.ShapeDtypeStruct((t, D_MODEL), BF16),
        grid=(nn, t // tm),
        in_specs=[pl.BlockSpec((tm, D_MODEL), row), y_spec, y_spec, y_spec,
                  *gate_specs, *tail_specs, *bias_specs, br_spec, br_spec, br_spec],
        out_specs=pl.BlockSpec((tm, tn), lambda j, i: (i, j)),
        scratch_shapes=[pltpu.VMEM((N_BRANCH, D_MODEL, tn), BF16)],
        compiler_params=_cparams("arbitrary", "arbitrary"),
        name="gated_merge",
    )(h, y_att, y_pool, y_gla, w_in, w_in, w_in, w_in, w_in, w_in, b_gate, b_gate, b_gate,
      w_br_att, w_br_pool, w_br_gla)


def _outproj_body(m_ref, w_ref, x_ref, o_ref):
    o_ref[...] = x_ref[...] + jnp.dot(m_ref[...], w_ref[...], preferred_element_type=F32)


def _outproj(merged, w_out, x, layer):
    t = x.shape[0]
    tm, tn = OUT_TM, OUT_TN
    return pl.pallas_call(
        _outproj_body,
        out_shape=jax.ShapeDtypeStruct((t, D_MODEL), F32),
        grid=(t // tm, D_MODEL // tn),
        in_specs=[
            pl.BlockSpec((tm, D_MODEL), lambda i, j: (i, 0)),
            pl.BlockSpec((None, D_MODEL, tn), lambda i, j: (layer, 0, j)),
            pl.BlockSpec((tm, tn), lambda i, j: (i, j)),
        ],
        out_specs=pl.BlockSpec((tm, tn), lambda i, j: (i, j)),
        compiler_params=_cparams("parallel", "arbitrary"),
        name="outproj",
    )(merged, w_out, x)


def _prep_mixer_weights(w_in, w_pool, w_gla_a2, w_br_att, w_br_pool, w_br_gla, w_out):
    depth = w_in.shape[0]
    w_in_bf = w_in.astype(BF16)
    w_a2 = jnp.pad(w_gla_a2.astype(BF16), ((0, 0), (0, LANES - GLA_LOWRANK), (0, 0)))
    groups = len(POOL_WINDOWS)
    eye = jnp.eye(groups, dtype=BF16)
    w_bd = (w_pool.astype(BF16)[:, :, :, None, :] * eye[None, :, None, :, None]).reshape(
        depth, POOL_WIDTH, POOL_WIDTH)
    return dict(w_in=w_in_bf, w_a2=w_a2, w_bd=w_bd,
                w_br_att=w_br_att.astype(BF16), w_br_pool=w_br_pool.astype(BF16),
                w_br_gla=w_br_gla.astype(BF16), w_out=w_out.astype(BF16))


def _mixer(x, cos_t, sin_t, batch, seq, layer, mw, norm_mix, b_gate, att_sinks, pool_scale,
           b_gla_a, gla_norm):
    proj, gk, h, y_pool = _inproj(x, norm_mix[None, :], mw["w_in"], mw["w_a2"], b_gla_a[None, :],
                                  cos_t, sin_t, mw["w_bd"], pool_scale[None, :], seq, layer)
    y_att = _attention(proj, att_sinks, batch, seq)
    y_gla = _gla(proj, gk, gla_norm[None, :], batch, seq)
    merged = _merge(h, y_att, y_pool, y_gla, mw["w_in"], b_gate[None, :],
                    mw["w_br_att"], mw["w_br_pool"], mw["w_br_gla"], layer)
    return _outproj(merged, mw["w_out"], x, layer)


def kernel(x, positions, norm_ffn1, ffn1_wi, ffn1_wo, norm_mix, w_in, b_gate, att_sinks, w_pool, pool_scale, w_gla_a2, b_gla_a, gla_norm, w_br_att, w_br_pool, w_br_gla, w_out, norm_ffn2, ffn2_wi, ffn2_wo, norm_final):
    batch, seq, d = x.shape
    depth = norm_ffn1.shape[0]
    assert d == D_MODEL and seq % max(INPROJ_TM, GLA_TS, ATT_BLOCK) == 0
    t = batch * seq
    assert t % max(FFN_TM, MERGE_TM, OUT_TM, ROPE_TM, INPROJ_TM) == 0
    xt = x.reshape(t, d)
    cos_t, sin_t = _rope_tables(positions)
    gf = norm_final[None, :]
    mw = _prep_mixer_weights(w_in, w_pool, w_gla_a2, w_br_att, w_br_pool, w_br_gla, w_out)
    for l in range(depth):
        xt = _ffn(xt, norm_ffn1[l][None, :], ffn1_wi, ffn1_wo, gf, l, False)
        xt = _mixer(xt, cos_t, sin_t, batch, seq, l, mw, norm_mix[l], b_gate[l], att_sinks[l],
                    pool_scale[l], b_gla_a[l], gla_norm[l])
        xt = _ffn(xt, norm_ffn2[l][None, :], ffn2_wi, ffn2_wo, gf, l, l == depth - 1)
    return xt.reshape(batch, seq, d)
```

```python
import functools

import numpy as np
import jax
import jax.numpy as jnp
from jax import lax
from jax.experimental import pallas as pl
from jax.experimental.pallas import tpu as pltpu

F32 = jnp.float32
BF16 = jnp.bfloat16

D_MODEL = 2048
D_FF = 5632
ATT_HEADS = 12
ATT_KV_HEADS = 4
HEAD_DIM = 64
ATT_BLOCK = 128
ROPE_THETA = 10000.0
ATT_Q_W = ATT_HEADS * HEAD_DIM
ATT_KV_W = ATT_KV_HEADS * HEAD_DIM
POOL_WINDOWS = (2, 4, 8, 16)
POOL_GROUP_DIM = 192
POOL_WIDTH = 768
GLA_HEADS = 4
GLA_DK = 96
GLA_DV = 192
GLA_QK_W = GLA_HEADS * GLA_DK
GLA_V_W = GLA_HEADS * GLA_DV
GLA_LOWRANK = 16
GLA_TAU = 16.0
GLA_CHUNK = 64
N_BRANCH = 3
EPS = 1e-6

_IN_QA, _IN_KA, _IN_VA, _IN_PU = 0, 768, 1024, 1280
_IN_QG, _IN_KG, _IN_VG, _IN_OG = 2048, 2432, 2816, 3584
_IN_LR = 4352
_IN_GATE = 4368

PROJ_SRC_W = 4352
PROJ_W = 3584
_BLK_GQK, _BLK_AQ, _BLK_GV, _BLK_GO = 0, 1, 2, 3
_BLK_AK, _BLK_AV = 12, 13
_PROJ_SEGMENTS = ((0, _IN_QG, 768, None), (768, _IN_QA, 768, HEAD_DIM ** -0.5),
                  (1536, _IN_VG, 1536, None), (3072, _IN_KA, 256, 1.0), (3328, _IN_VA, 256, None))
W_IN_COLS = _IN_GATE + N_BRANCH * D_MODEL
GATE_LEAD = _IN_GATE % 128

LANES = 128
POOL_HALO = 16
VMEM_LIMIT = 58 * 1024 * 1024

FFN_TM, FFN_TF = 1024, 256
FFN_UNROLL = 4
FFN_NORM_ROWS = 256
INPROJ_TM = 512
INPROJ_NCH = 1024
GLA_TS = 256
GLA_PAIR = 2
MERGE_TM, MERGE_TN = 1024, 512
OUT_TM, OUT_TN = 512, 2048
ROPE_TM = 1024
CAST_ROWS = 256


def _cparams(*sem):
    return pltpu.CompilerParams(dimension_semantics=sem, vmem_limit_bytes=VMEM_LIMIT)


def _rms(x, g):
    return x * lax.rsqrt(jnp.mean(x * x, axis=-1, keepdims=True) + EPS) * g


def _ffn_body(x_ref, g_ref, gf_ref, wi_hbm, wo_hbm, o_ref, h_ref, u_ref, wa_buf, wb_buf, wo_buf, sem,
              *, layer, n_tiles, final_norm):
    tf = FFN_TF
    nch = D_FF // tf
    i = pl.program_id(0)

    def up_copies(c, slot):
        col = pl.multiple_of(c * tf, tf)
        return (pltpu.make_async_copy(wi_hbm.at[layer, :, pl.ds(col, tf)], wa_buf.at[slot], sem.at[0, slot]),
                pltpu.make_async_copy(wi_hbm.at[layer, :, pl.ds(D_FF + col, tf)], wb_buf.at[slot],
                                      sem.at[1, slot]))

    def down_copy(c, slot):
        row = pl.multiple_of(c * tf, tf)
        return pltpu.make_async_copy(wo_hbm.at[layer, pl.ds(row, tf), :], wo_buf.at[slot], sem.at[2, slot])

    def up(slot):
        h = h_ref[...]
        a = jnp.dot(h, wa_buf[slot].astype(BF16), preferred_element_type=F32)
        b = jnp.dot(h, wb_buf[slot].astype(BF16), preferred_element_type=F32)
        u_ref[slot] = (a * jax.nn.sigmoid(a) * (0.5 * b)).astype(BF16)

    def down(slot):
        o_ref[...] += jnp.dot(u_ref[slot], wo_buf[slot].astype(BF16), preferred_element_type=F32)

    def start_up(c, slot):
        ca, cb = up_copies(c, slot)
        ca.start(priority=1)
        cb.start(priority=1)

    def stage(c, slot):
        for cp in up_copies(c, slot):
            cp.wait()
        down_copy(c - 1, 1 - slot).wait()
        down_copy(c, slot).start(priority=1)
        down(1 - slot)
        up(slot)

    @pl.when(i == 0)
    def _():
        start_up(0, 0)

    for cp in up_copies(0, 0):
        cp.wait()
    start_up(1, 1)
    down_copy(0, 0).start(priority=1)
    wa0 = wa_buf[0].astype(BF16)
    wb0 = wb_buf[0].astype(BF16)
    for r in range(x_ref.shape[0] // FFN_NORM_ROWS):
        rows = slice(r * FFN_NORM_ROWS, (r + 1) * FFN_NORM_ROWS)
        x = x_ref[rows, :]
        h = _rms(x, g_ref[...]).astype(BF16)
        h_ref[rows, :] = h
        o_ref[rows, :] = x
        a = jnp.dot(h, wa0, preferred_element_type=F32)
        b = jnp.dot(h, wb0, preferred_element_type=F32)
        u_ref[0, rows, :] = (a * jax.nn.sigmoid(a) * (0.5 * b)).astype(BF16)

    def stages(k, carry):
        c0 = FFN_UNROLL * k + 1
        for s in range(FFN_UNROLL):
            slot = (1 + s) % 2
            start_up(c0 + s + 1, 1 - slot)
            stage(c0 + s, slot)
        return carry

    lax.fori_loop(0, (nch - 2) // FFN_UNROLL, stages, 0)

    stage(nch - 1, 1)

    @pl.when(i + 1 < n_tiles)
    def _():
        start_up(0, 0)

    down_copy(nch - 1, 1).wait()
    down(1)
    if final_norm:
        o_ref[...] = _rms(o_ref[...], gf_ref[...])


def _ffn(x, g, wi, wo, gf, layer, final_norm):
    t = x.shape[0]
    tm, tf = FFN_TM, FFN_TF
    assert FFN_UNROLL % 2 == 0 and (D_FF // tf - 2) % FFN_UNROLL == 0
    n_tiles = t // tm
    return pl.pallas_call(
        functools.partial(_ffn_body, layer=layer, n_tiles=n_tiles, final_norm=final_norm),
        out_shape=jax.ShapeDtypeStruct((t, D_MODEL), F32),
        grid=(n_tiles,),
        in_specs=[
            pl.BlockSpec((tm, D_MODEL), lambda i: (i, 0)),
            pl.BlockSpec((1, D_MODEL), lambda i: (0, 0)),
            pl.BlockSpec((1, D_MODEL), lambda i: (0, 0)),
            pl.BlockSpec(memory_space=pl.ANY),
            pl.BlockSpec(memory_space=pl.ANY),
        ],
        out_specs=pl.BlockSpec((tm, D_MODEL), lambda i: (i, 0)),
        scratch_shapes=[
            pltpu.VMEM((tm, D_MODEL), BF16),
            pltpu.VMEM((2, tm, tf), BF16),
            pltpu.VMEM((2, D_MODEL, tf), F32),
            pltpu.VMEM((2, D_MODEL, tf), F32),
            pltpu.VMEM((2, tf, D_MODEL), F32),
            pltpu.SemaphoreType.DMA((3, 2)),
        ],
        compiler_params=_cparams("arbitrary"),
        name="ffn_final" if final_norm else "ffn",
    )(x, g, gf, wi, wo)


def _rope_table_body(pos_ref, freq_ref, sign_ref, cos_ref, sin_ref):
    ang = pos_ref[...].astype(F32) * freq_ref[...]
    cos_ref[...] = jnp.cos(ang)
    sin_ref[...] = jnp.sin(ang) * sign_ref[...]


def _rope_tables(positions):
    t = positions.size
    half = HEAD_DIM // 2
    inv_freq = ROPE_THETA ** (-jnp.arange(half, dtype=F32) / half)
    freq = jnp.tile(inv_freq, LANES // half)[None, :]
    sign = jnp.tile(jnp.concatenate([-jnp.ones((half,), F32), jnp.ones((half,), F32)]),
                    LANES // HEAD_DIM)[None, :]
    tm = ROPE_TM
    return pl.pallas_call(
        _rope_table_body,
        out_shape=(jax.ShapeDtypeStruct((t, LANES), F32), jax.ShapeDtypeStruct((t, LANES), F32)),
        grid=(t // tm,),
        in_specs=[
            pl.BlockSpec((tm, 1), lambda i: (i, 0)),
            pl.BlockSpec((1, LANES), lambda i: (0, 0)),
            pl.BlockSpec((1, LANES), lambda i: (0, 0)),
        ],
        out_specs=(pl.BlockSpec((tm, LANES), lambda i: (i, 0)),
                   pl.BlockSpec((tm, LANES), lambda i: (i, 0))),
        compiler_params=_cparams("parallel"),
        name="rope_tables",
    )(positions.reshape(t, 1), freq, sign)


def _swap_halves(x):
    lane = lax.broadcasted_iota(jnp.int32, x.shape, 1)
    first = (lane % HEAD_DIM) < (HEAD_DIM // 2)
    return jnp.where(first, pltpu.roll(x, LANES - HEAD_DIM // 2, 1), pltpu.roll(x, HEAD_DIM // 2, 1))


def _rope(x, cos, sin):
    cols = []
    for c in range(x.shape[1] // LANES):
        xc = x[:, c * LANES:(c + 1) * LANES]
        cols.append(xc * cos + _swap_halves(xc) * sin)
    return jnp.concatenate(cols, axis=1)


def _pool_mix(u, halo, tok0):
    ts = u.shape[0]
    a1 = jnp.concatenate([halo, u], axis=0)
    length = ts + POOL_HALO
    a2 = a1[1:] + a1[:length - 1]
    a4 = a2[2:] + a2[:length - 3]
    a8 = a4[4:] + a4[:length - 7]
    a16 = a8[8:] + a8[:length - 15]
    sums = {2: a2[POOL_HALO - 1:], 4: a4[POOL_HALO - 3:], 8: a8[POOL_HALO - 7:], 16: a16[POOL_HALO - 15:]}

    tok = tok0 + lax.broadcasted_iota(jnp.int32, (ts, 1), 0)
    chan = lax.broadcasted_iota(jnp.int32, (1, POOL_WIDTH), 1)
    d = jnp.zeros((ts, POOL_WIDTH), F32)
    for gi, w in enumerate(POOL_WINDOWS):
        inv_cnt = 1.0 / jnp.minimum(tok + 1, w).astype(F32)
        in_group = (chan >= gi * POOL_GROUP_DIM) & (chan < (gi + 1) * POOL_GROUP_DIM)
        d = jnp.where(in_group, sums[w] * inv_cnt, d)
    return d - u


def _inproj_body(x_ref, g_ref, w_ref, wlr_ref, wa2_ref, ba_ref, cos_ref, sin_ref, wbd_ref, ps_ref,
                 proj_ref, gk_ref, h_ref, ypool_ref, halo_ref, *, tiles_per_seq):
    tm = x_ref.shape[0]
    h = _rms(x_ref[...], g_ref[...]).astype(BF16)
    h_ref[...] = h
    s_idx = pl.program_id(0) % tiles_per_seq

    @pl.when(s_idx == 0)
    def _():
        halo_ref[...] = jnp.zeros_like(halo_ref)

    pu = jnp.dot(h, w_ref[:, _IN_PU:_IN_PU + POOL_WIDTH], preferred_element_type=F32)
    d = _pool_mix(pu, halo_ref[...], s_idx * tm).astype(BF16)
    halo_ref[...] = pu[tm - POOL_HALO:]

    lr = jnp.dot(h, wlr_ref[...], preferred_element_type=F32).astype(BF16)

    cos, sin = cos_ref[...], sin_ref[...]
    for dst, src, width, rot_scale in _PROJ_SEGMENTS:
        p = jnp.dot(h, w_ref[:, src:src + width], preferred_element_type=F32)
        if rot_scale is not None:
            p = _rope(p, cos * rot_scale, sin * rot_scale)
        proj_ref[:, dst:dst + width] = p.astype(proj_ref.dtype)

    z = jnp.dot(lr, wa2_ref[...], preferred_element_type=F32) + ba_ref[...]
    gk_ref[...] = (jnp.minimum(z, 0.0) - jnp.log1p(jnp.exp(-jnp.abs(z)))) * (1.0 / GLA_TAU)
    y = jnp.dot(d, wbd_ref[...], preferred_element_type=F32)
    ypool_ref[...] = (y * ps_ref[...]).astype(ypool_ref.dtype)


def _inproj(x, g, w_in, w_a2, b_a, cos_t, sin_t, w_bd, pool_scale, seq, layer):
    t = x.shape[0]
    tm = INPROJ_TM
    assert seq % tm == 0
    const = lambda i: (0, 0)
    lconst = lambda i: (layer, 0, 0)
    rows = lambda i: (i, 0)
    return pl.pallas_call(
        functools.partial(_inproj_body, tiles_per_seq=seq // tm),
        out_shape=(jax.ShapeDtypeStruct((t, PROJ_W), BF16),
                   jax.ShapeDtypeStruct((t, GLA_QK_W), F32),
                   jax.ShapeDtypeStruct((t, D_MODEL), BF16),
                   jax.ShapeDtypeStruct((t, POOL_WIDTH), BF16)),
        grid=(t // tm,),
        in_specs=[
            pl.BlockSpec((tm, D_MODEL), rows),
            pl.BlockSpec((1, D_MODEL), const),
            pl.BlockSpec((None, D_MODEL, PROJ_SRC_W), lconst, pipeline_mode=pl.Buffered(1)),
            pl.BlockSpec((None, D_MODEL, LANES), lambda i: (layer, 0, _IN_LR // LANES)),
            pl.BlockSpec((None, LANES, GLA_QK_W), lconst),
            pl.BlockSpec((1, GLA_QK_W), const),
            pl.BlockSpec((tm, LANES), rows),
            pl.BlockSpec((tm, LANES), rows),
            pl.BlockSpec((None, POOL_WIDTH, POOL_WIDTH), lconst),
            pl.BlockSpec((1, POOL_WIDTH), const),
        ],
        out_specs=(pl.BlockSpec((tm, PROJ_W), rows),
                   pl.BlockSpec((tm, GLA_QK_W), rows),
                   pl.BlockSpec((tm, D_MODEL), rows),
                   pl.BlockSpec((tm, POOL_WIDTH), rows)),
        scratch_shapes=[pltpu.VMEM((POOL_HALO, POOL_WIDTH), F32)],
        compiler_params=_cparams("arbitrary"),
        name="inproj",
    )(x, g, w_in, w_in, w_a2, b_a, cos_t, sin_t, w_bd, pool_scale)


def _both_halves(x, col, half):
    xc = x[:, col * LANES:(col + 1) * LANES].astype(F32)
    sw = pltpu.roll(xc, LANES // 2, 1)
    first = lax.broadcasted_iota(jnp.int32, xc.shape, 1) < LANES // 2
    dup = jnp.where(first, xc, sw) if half == 0 else jnp.where(first, sw, xc)
    return dup.astype(BF16)


def _attn_body(sink_ref, q_ref, kc_ref, kp_ref, vc_ref, vp_ref, o_ref):
    n = pl.program_id(1)
    blk = ATT_BLOCK
    group = ATT_HEADS // ATT_KV_HEADS
    q = q_ref[...]
    k = jnp.concatenate([kp_ref[...], kc_ref[...]], axis=0)
    v = jnp.concatenate([vp_ref[...], vc_ref[...]], axis=0)

    qi = lax.broadcasted_iota(jnp.int32, (blk, 2 * blk), 0)
    si = lax.broadcasted_iota(jnp.int32, (blk, 2 * blk), 1)
    rel = blk + qi - si
    mask = (rel >= 0) & (rel < blk) & ((si >= blk) | (n > 0))
    bias = jnp.where(mask, 0.0, -1e30)
    first = lax.broadcasted_iota(jnp.int32, (blk, LANES), 1) < LANES // 2
    zero = jnp.zeros((blk, LANES), BF16)

    heads = [None] * ATT_HEADS
    for g in range(ATT_KV_HEADS):
        kk = _both_halves(k, g // 2, g % 2)
        vv = _both_halves(v, g // 2, g % 2)
        hs = range(g * group, (g + 1) * group)
        qm = jnp.concatenate(
            [jnp.where(first if hd % 2 == 0 else ~first, q[:, (hd // 2) * LANES:(hd // 2 + 1) * LANES], zero)
             for hd in hs], axis=0)
        s3 = lax.dot_general(qm, kk, (((1,), (1,)), ((), ())), preferred_element_type=F32)
        es, denoms = [], []
        for r, hd in enumerate(hs):
            s = s3[r * blk:(r + 1) * blk] + bias
            sink = sink_ref[hd]
            m = jnp.maximum(jnp.max(s, axis=1, keepdims=True), sink)
            e = jnp.exp(s - m)
            denoms.append(jnp.sum(e, axis=1, keepdims=True) + jnp.exp(sink - m))
            es.append(e.astype(BF16))
        o3 = jnp.dot(jnp.concatenate(es, axis=0), vv, preferred_element_type=F32)
        for r, hd in enumerate(hs):
            heads[hd] = o3[r * blk:(r + 1) * blk] / denoms[r]
    cols = [jnp.where(first, heads[2 * c], heads[2 * c + 1]) for c in range(ATT_HEADS // 2)]
    o_ref[...] = jnp.concatenate(cols, axis=1).astype(o_ref.dtype)


def _attention(proj, sinks, batch, seq):
    t = proj.shape[0]
    blk = ATT_BLOCK
    nb = seq // blk
    cur = lambda b, n: b * nb + n
    prev = lambda b, n: b * nb + jnp.maximum(n - 1, 0)
    return pl.pallas_call(
        _attn_body,
        out_shape=jax.ShapeDtypeStruct((t, ATT_Q_W), BF16),
        grid=(batch, nb),
        in_specs=[
            pl.BlockSpec(memory_space=pltpu.SMEM),
            pl.BlockSpec((blk, ATT_Q_W), lambda b, n: (cur(b, n), _BLK_AQ)),
            pl.BlockSpec((blk, ATT_KV_W), lambda b, n: (cur(b, n), _BLK_AK)),
            pl.BlockSpec((blk, ATT_KV_W), lambda b, n: (prev(b, n), _BLK_AK)),
            pl.BlockSpec((blk, ATT_KV_W), lambda b, n: (cur(b, n), _BLK_AV)),
            pl.BlockSpec((blk, ATT_KV_W), lambda b, n: (prev(b, n), _BLK_AV)),
        ],
        out_specs=pl.BlockSpec((blk, ATT_Q_W), lambda b, n: (cur(b, n), 0)),
        compiler_params=_cparams("parallel", "arbitrary"),
        name="swa_attention",
    )(sinks, proj, proj, proj, proj, proj)


def _gla_body(qk_ref, v_ref, og_ref, gk_ref, norm_ref, tril_ref, bd_ref, o_ref, st_ref, oacc_ref):
    s_idx = pl.program_id(1)
    npair, ts = qk_ref.shape[0], qk_ref.shape[1]
    c = GLA_CHUNK

    @pl.when(s_idx == 0)
    def _():
        st_ref[...] = jnp.zeros_like(st_ref)

    tril = tril_ref[...]
    prep = []
    for p in range(npair):
        qk = qk_ref[p].astype(F32)
        q = qk[:, :GLA_QK_W] * (GLA_DK ** -0.5)
        k = qk[:, GLA_QK_W:]
        gk = gk_ref[p]
        g1 = gk.astype(BF16)
        r1 = gk - g1.astype(F32)
        g2 = r1.astype(BF16)
        g3 = (r1 - g2.astype(F32)).astype(BF16)
        b = (jnp.dot(tril, g1, preferred_element_type=F32) + jnp.dot(tril, g2, preferred_element_type=F32)
             + jnp.dot(tril, g3, preferred_element_type=F32))
        prep.append(((q * jnp.exp(b)).astype(BF16), (k * jnp.exp(-b)).astype(BF16), k, b, v_ref[p]))

    kcol = lax.broadcasted_iota(jnp.int32, (1, GLA_QK_W), 1)
    vcol = lax.broadcasted_iota(jnp.int32, (1, GLA_V_W), 1)
    ri = lax.broadcasted_iota(jnp.int32, (c, c), 0)
    ci = lax.broadcasted_iota(jnp.int32, (c, c), 1)
    causal = ri >= ci
    bd = bd_ref[...]

    for n in range(ts // c):
        rows = slice(n * c, (n + 1) * c)
        for p in range(npair):
            q_t, k_t, k, b, v = prep[p]
            b_n = b[rows]
            b_last = b_n[c - 1:c, :]
            q_n = q_t[rows]
            k_n = k_t[rows]
            v_n = v[rows]
            k_s = (k[rows] * jnp.exp(b_last - b_n)).astype(BF16)
            q4 = jnp.concatenate(
                [jnp.where((kcol >= h * GLA_DK) & (kcol < (h + 1) * GLA_DK), q_n, jnp.zeros_like(q_n))
                 for h in range(GLA_HEADS)], axis=0)
            a4 = lax.dot_general(q4, k_n, (((1,), (1,)), ((), ())), preferred_element_type=F32)
            o_n = jnp.zeros((c, GLA_V_W), F32)
            for h in range(GLA_HEADS):
                a_h = jnp.where(causal, a4[h * c:(h + 1) * c], 0.0).astype(BF16)
                o_h = jnp.dot(a_h, v_n, preferred_element_type=F32)
                o_n = jnp.where((vcol >= h * GLA_DV) & (vcol < (h + 1) * GLA_DV), o_h, o_n)
            st = st_ref[p]
            o_n = o_n + lax.dot_general(q_n, st.astype(BF16), (((1,), (1,)), ((), ())),
                                        preferred_element_type=F32)
            kv = lax.dot_general(v_n, k_s, (((0,), (0,)), ((), ())), preferred_element_type=F32)
            st_ref[p] = st * jnp.exp(b_last) + kv * bd
            oacc_ref[p, rows, :] = o_n

    for p in range(npair):
        o = oacc_ref[p]
        o2 = o * o
        inv = jnp.zeros_like(o)
        for h in range(GLA_HEADS):
            in_head = (vcol >= h * GLA_DV) & (vcol < (h + 1) * GLA_DV)
            ms = jnp.sum(jnp.where(in_head, o2, 0.0), axis=1, keepdims=True) * (1.0 / GLA_DV)
            inv = jnp.where(in_head, lax.rsqrt(ms + EPS), inv)
        og = og_ref[p].astype(F32)
        o_ref[p] = (o * inv * norm_ref[...] * (og * jax.nn.sigmoid(og))).astype(o_ref.dtype)


def _gla(proj, gk, norm, batch, seq):
    t = proj.shape[0]
    ts = GLA_TS
    ns = seq // ts
    pair = GLA_PAIR
    assert batch % pair == 0
    r = np.arange(ts)
    tril = ((r[:, None] // GLA_CHUNK == r[None, :] // GLA_CHUNK) & (r[:, None] >= r[None, :]))
    tril = jnp.asarray(tril, BF16)
    bd = (np.arange(GLA_V_W)[:, None] // GLA_DV) == (np.arange(GLA_QK_W)[None, :] // GLA_DK)
    bd = jnp.asarray(bd, F32)
    proj3 = proj.reshape(batch, seq, PROJ_W)
    out = pl.pallas_call(
        _gla_body,
        out_shape=jax.ShapeDtypeStruct((batch, seq, GLA_V_W), BF16),
        grid=(batch // pair, ns),
        in_specs=[
            pl.BlockSpec((pair, ts, 2 * GLA_QK_W), lambda b, s: (b, s, _BLK_GQK)),
            pl.BlockSpec((pair, ts, GLA_V_W), lambda b, s: (b, s, _BLK_GV)),
            pl.BlockSpec((pair, ts, GLA_V_W), lambda b, s: (b, s, _BLK_GO)),
            pl.BlockSpec((pair, ts, GLA_QK_W), lambda b, s: (b, s, 0)),
            pl.BlockSpec((1, GLA_V_W), lambda b, s: (0, 0)),
            pl.BlockSpec((ts, ts), lambda b, s: (0, 0)),
            pl.BlockSpec((GLA_V_W, GLA_QK_W), lambda b, s: (0, 0)),
        ],
        out_specs=pl.BlockSpec((pair, ts, GLA_V_W), lambda b, s: (b, s, 0)),
        scratch_shapes=[pltpu.VMEM((pair, GLA_V_W, GLA_QK_W), F32),
                        pltpu.VMEM((pair, ts, GLA_V_W), F32)],
        compiler_params=_cparams("parallel", "arbitrary"),
        name="gla",
    )(proj3, proj3, proj3, gk.reshape(batch, seq, GLA_QK_W), norm, tril, bd)
    return out.reshape(t, GLA_V_W)


def _merge_body(h_ref, ya_ref, yp_ref, yg_ref, wga_ref, wgp_ref, wgg_ref, tga_ref, tgp_ref, tgg_ref,
                bga_ref, bgp_ref, bgg_ref, wa_ref, wp_ref, wg_ref, o_ref, wgate_ref):
    tn = o_ref.shape[1]

    @pl.when(pl.program_id(1) == 0)
    def _():
        for br, (win_ref, tail_ref) in enumerate(((wga_ref, tga_ref), (wgp_ref, tgp_ref),
                                                  (wgg_ref, tgg_ref))):
            win = jnp.concatenate([win_ref[...], tail_ref[...]], axis=1).astype(F32)
            wgate_ref[br] = win[:, GATE_LEAD:GATE_LEAD + tn].astype(BF16)

    h = h_ref[...]
    acc = None
    for br, (y_ref, bgate_ref, wbr_ref) in enumerate(((ya_ref, bga_ref, wa_ref),
                                                     (yp_ref, bgp_ref, wp_ref),
                                                     (yg_ref, bgg_ref, wg_ref))):
        gate = jax.nn.sigmoid(jnp.dot(h, wgate_ref[br], preferred_element_type=F32) + bgate_ref[...])
        term = gate * jnp.dot(y_ref[...], wbr_ref[...], preferred_element_type=F32)
        acc = term if acc is None else acc + term
    o_ref[...] = acc.astype(o_ref.dtype)


def _merge(h, y_att, y_pool, y_gla, w_in, b_gate, w_br_att, w_br_pool, w_br_gla, layer):
    t = h.shape[0]
    tm, tn = MERGE_TM, MERGE_TN
    nn = D_MODEL // tn
    row = lambda j, i: (i, 0)
    y_spec = pl.BlockSpec((tm, ATT_Q_W), row)
    gate_base = _IN_GATE - GATE_LEAD
    gate_specs = [pl.BlockSpec((pl.Squeezed(), pl.Element(D_MODEL), pl.Element(tn)),
                               functools.partial(
                                   lambda j, i, br: (layer, 0, pl.multiple_of(
                                       gate_base + br * D_MODEL + j * tn, LANES)), br=br))
                  for br in range(N_BRANCH)]
    tail_specs = [pl.BlockSpec((None, D_MODEL, LANES),
                               functools.partial(
                                   lambda j, i, br: (layer, 0, (gate_base + br * D_MODEL) // LANES
                                                     + (j + 1) * (tn // LANES)), br=br))
                  for br in range(N_BRANCH)]
    bias_specs = [pl.BlockSpec((1, tn), functools.partial(lambda j, i, br: (0, br * nn + j), br=br))
                  for br in range(N_BRANCH)]
    br_spec = pl.BlockSpec((None, ATT_Q_W, tn), lambda j, i: (layer, 0, j))
    return pl.pallas_call(
        _merge_body,
        out_shape=jax.ShapeDtypeStruct((t, D_MODEL), BF16),
        grid=(nn, t // tm),
        in_specs=[pl.BlockSpec((tm, D_MODEL), row), y_spec, y_spec, y_spec,
                  *gate_specs, *tail_specs, *bias_specs, br_spec, br_spec, br_spec],
        out_specs=pl.BlockSpec((tm, tn), lambda j, i: (i, j)),
        scratch_shapes=[pltpu.VMEM((N_BRANCH, D_MODEL, tn), BF16)],
        compiler_params=_cparams("arbitrary", "arbitrary"),
        name="gated_merge",
    )(h, y_att, y_pool, y_gla, w_in, w_in, w_in, w_in, w_in, w_in, b_gate, b_gate, b_gate,
      w_br_att, w_br_pool, w_br_gla)


def _outproj_body(m_ref, w_ref, x_ref, o_ref):
    o_ref[...] = x_ref[...] + jnp.dot(m_ref[...], w_ref[...], preferred_element_type=F32)


def _outproj(merged, w_out, x, layer):
    t = x.shape[0]
    tm, tn = OUT_TM, OUT_TN
    return pl.pallas_call(
        _outproj_body,
        out_shape=jax.ShapeDtypeStruct((t, D_MODEL), F32),
        grid=(t // tm, D_MODEL // tn),
        in_specs=[
            pl.BlockSpec((tm, D_MODEL), lambda i, j: (i, 0)),
            pl.BlockSpec((None, D_MODEL, tn), lambda i, j: (layer, 0, j)),
            pl.BlockSpec((tm, tn), lambda i, j: (i, j)),
        ],
        out_specs=pl.BlockSpec((tm, tn), lambda i, j: (i, j)),
        compiler_params=_cparams("parallel", "arbitrary"),
        name="outproj",
    )(merged, w_out, x)


def _cast_body(x_ref, o_ref):
    o_ref[...] = x_ref[...].astype(o_ref.dtype)


def _cast_bf16(w):
    depth, r, c = w.shape
    assert r % CAST_ROWS == 0
    spec = pl.BlockSpec((None, CAST_ROWS, c), lambda l, i: (l, i, 0))
    return pl.pallas_call(
        _cast_body,
        out_shape=jax.ShapeDtypeStruct(w.shape, BF16),
        grid=(depth, r // CAST_ROWS),
        in_specs=[spec],
        out_specs=spec,
        compiler_params=_cparams("parallel", "parallel"),
        name="cast_bf16",
    )(w)


def _prep_mixer_weights(w_in, w_pool, w_gla_a2, w_br_att, w_br_pool, w_br_gla, w_out):
    depth = w_in.shape[0]
    w_in_bf = _cast_bf16(w_in)
    w_a2 = jnp.pad(w_gla_a2.astype(BF16), ((0, 0), (0, LANES - GLA_LOWRANK), (0, 0)))
    groups = len(POOL_WINDOWS)
    eye = jnp.eye(groups, dtype=BF16)
    w_bd = (w_pool.astype(BF16)[:, :, :, None, :] * eye[None, :, None, :, None]).reshape(
        depth, POOL_WIDTH, POOL_WIDTH)
    return dict(w_in=w_in_bf, w_a2=w_a2, w_bd=w_bd,
                w_br_att=_cast_bf16(w_br_att), w_br_pool=_cast_bf16(w_br_pool),
                w_br_gla=_cast_bf16(w_br_gla), w_out=_cast_bf16(w_out))


def _mixer(x, cos_t, sin_t, batch, seq, layer, mw, norm_mix, b_gate, att_sinks, pool_scale,
           b_gla_a, gla_norm):
    proj, gk, h, y_pool = _inproj(x, norm_mix[None, :], mw["w_in"], mw["w_a2"], b_gla_a[None, :],
                                  cos_t, sin_t, mw["w_bd"], pool_scale[None, :], seq, layer)
    y_att = _attention(proj, att_sinks, batch, seq)
    y_gla = _gla(proj, gk, gla_norm[None, :], batch, seq)
    merged = _merge(h, y_att, y_pool, y_gla, mw["w_in"], b_gate[None, :],
                    mw["w_br_att"], mw["w_br_pool"], mw["w_br_gla"], layer)
    return _outproj(merged, mw["w_out"], x, layer)


def kernel(x, positions, norm_ffn1, ffn1_wi, ffn1_wo, norm_mix, w_in, b_gate, att_sinks, w_pool, pool_scale, w_gla_a2, b_gla_a, gla_norm, w_br_att, w_br_pool, w_br_gla, w_out, norm_ffn2, ffn2_wi, ffn2_wo, norm_final):
    batch, seq, d = x.shape
    depth = norm_ffn1.shape[0]
    assert d == D_MODEL and seq % max(INPROJ_TM, GLA_TS, ATT_BLOCK) == 0
    t = batch * seq
    assert t % max(FFN_TM, MERGE_TM, OUT_TM, ROPE_TM, INPROJ_TM) == 0
    xt = x.reshape(t, d)
    cos_t, sin_t = _rope_tables(positions)
    gf = norm_final[None, :]
    mw = _prep_mixer_weights(w_in, w_pool, w_gla_a2, w_br_att, w_br_pool, w_br_gla, w_out)
    for l in range(depth):
        xt = _ffn(xt, norm_ffn1[l][None, :], ffn1_wi, ffn1_wo, gf, l, False)
        xt = _mixer(xt, cos_t, sin_t, batch, seq, l, mw, norm_mix[l], b_gate[l], att_sinks[l],
                    pool_scale[l], b_gla_a[l], gla_norm[l])
        xt = _ffn(xt, norm_ffn2[l][None, :], ffn2_wi, ffn2_wo, gf, l, l == depth - 1)
    return xt.reshape(batch, seq, d)
```

```python
import functools

import numpy as np
import jax
import jax.numpy as jnp
from jax import lax
from jax.experimental import pallas as pl
from jax.experimental.pallas import tpu as pltpu

F32 = jnp.float32
BF16 = jnp.bfloat16

D_MODEL = 2048
D_FF = 5632
ATT_HEADS = 12
ATT_KV_HEADS = 4
HEAD_DIM = 64
ATT_BLOCK = 128
ROPE_THETA = 10000.0
ATT_Q_W = ATT_HEADS * HEAD_DIM
ATT_KV_W = ATT_KV_HEADS * HEAD_DIM
POOL_WINDOWS = (2, 4, 8, 16)
POOL_GROUP_DIM = 192
POOL_WIDTH = 768
GLA_HEADS = 4
GLA_DK = 96
GLA_DV = 192
GLA_QK_W = GLA_HEADS * GLA_DK
GLA_V_W = GLA_HEADS * GLA_DV
GLA_LOWRANK = 16
GLA_TAU = 16.0
GLA_CHUNK = 64
N_BRANCH = 3
EPS = 1e-6

_IN_QA, _IN_KA, _IN_VA, _IN_PU = 0, 768, 1024, 1280
_IN_QG, _IN_KG, _IN_VG, _IN_OG = 2048, 2432, 2816, 3584
_IN_LR = 4352
_IN_GATE = 4368

PROJ_SRC_W = 4352
PROJ_W = 2304
_BLK_GQK, _BLK_GV, _BLK_GO = 0, 1, 2
_PROJ_SEGMENTS = ((0, _IN_QG, 768), (768, _IN_VG, 1536))
W_IN_COLS = _IN_GATE + N_BRANCH * D_MODEL
GATE_LEAD = _IN_GATE % 128

LANES = 128
POOL_HALO = 16
VMEM_LIMIT = 58 * 1024 * 1024

FFN_TM, FFN_TF = 1024, 256
FFN_UNROLL = 4
FFN_NORM_ROWS = 256
INPROJ_TM = 512
INPROJ_NCH = 1024
GLA_TS = 256
GLA_PAIR = 2
MERGE_TM, MERGE_TN = 1024, 512
OUT_TM, OUT_TN = 512, 2048
ROPE_TM = 1024


def _cparams(*sem):
    return pltpu.CompilerParams(dimension_semantics=sem, vmem_limit_bytes=VMEM_LIMIT)


def _rms(x, g):
    return x * lax.rsqrt(jnp.mean(x * x, axis=-1, keepdims=True) + EPS) * g


def _ffn_body(x_ref, g_ref, gf_ref, wi_hbm, wo_hbm, o_ref, h_ref, u_ref, wa_buf, wb_buf, wo_buf, sem,
              *, layer, n_tiles, final_norm):
    tf = FFN_TF
    nch = D_FF // tf
    i = pl.program_id(0)

    def up_copies(c, slot):
        col = pl.multiple_of(c * tf, tf)
        return (pltpu.make_async_copy(wi_hbm.at[layer, :, pl.ds(col, tf)], wa_buf.at[slot], sem.at[0, slot]),
                pltpu.make_async_copy(wi_hbm.at[layer, :, pl.ds(D_FF + col, tf)], wb_buf.at[slot],
                                      sem.at[1, slot]))

    def down_copy(c, slot):
        row = pl.multiple_of(c * tf, tf)
        return pltpu.make_async_copy(wo_hbm.at[layer, pl.ds(row, tf), :], wo_buf.at[slot], sem.at[2, slot])

    def up(slot):
        h = h_ref[...]
        a = jnp.dot(h, wa_buf[slot].astype(BF16), preferred_element_type=F32)
        b = jnp.dot(h, wb_buf[slot].astype(BF16), preferred_element_type=F32)
        u_ref[slot] = (a * jax.nn.sigmoid(a) * (0.5 * b)).astype(BF16)

    def down(slot):
        o_ref[...] += jnp.dot(u_ref[slot], wo_buf[slot].astype(BF16), preferred_element_type=F32)

    def start_up(c, slot):
        ca, cb = up_copies(c, slot)
        ca.start(priority=1)
        cb.start(priority=1)

    def stage(c, slot):
        for cp in up_copies(c, slot):
            cp.wait()
        down_copy(c - 1, 1 - slot).wait()
        down_copy(c, slot).start(priority=1)
        down(1 - slot)
        up(slot)

    @pl.when(i == 0)
    def _():
        start_up(0, 0)

    for cp in up_copies(0, 0):
        cp.wait()
    start_up(1, 1)
    down_copy(0, 0).start(priority=1)
    wa0 = wa_buf[0].astype(BF16)
    wb0 = wb_buf[0].astype(BF16)
    for r in range(x_ref.shape[0] // FFN_NORM_ROWS):
        rows = slice(r * FFN_NORM_ROWS, (r + 1) * FFN_NORM_ROWS)
        x = x_ref[rows, :]
        h = _rms(x, g_ref[...]).astype(BF16)
        h_ref[rows, :] = h
        o_ref[rows, :] = x
        a = jnp.dot(h, wa0, preferred_element_type=F32)
        b = jnp.dot(h, wb0, preferred_element_type=F32)
        u_ref[0, rows, :] = (a * jax.nn.sigmoid(a) * (0.5 * b)).astype(BF16)

    def stages(k, carry):
        c0 = FFN_UNROLL * k + 1
        for s in range(FFN_UNROLL):
            slot = (1 + s) % 2
            start_up(c0 + s + 1, 1 - slot)
            stage(c0 + s, slot)
        return carry

    lax.fori_loop(0, (nch - 2) // FFN_UNROLL, stages, 0)

    stage(nch - 1, 1)

    @pl.when(i + 1 < n_tiles)
    def _():
        start_up(0, 0)

    down_copy(nch - 1, 1).wait()
    down(1)
    if final_norm:
        o_ref[...] = _rms(o_ref[...], gf_ref[...])


def _ffn(x, g, wi, wo, gf, layer, final_norm):
    t = x.shape[0]
    tm, tf = FFN_TM, FFN_TF
    assert FFN_UNROLL % 2 == 0 and (D_FF // tf - 2) % FFN_UNROLL == 0
    n_tiles = t // tm
    return pl.pallas_call(
        functools.partial(_ffn_body, layer=layer, n_tiles=n_tiles, final_norm=final_norm),
        out_shape=jax.ShapeDtypeStruct((t, D_MODEL), F32),
        grid=(n_tiles,),
        in_specs=[
            pl.BlockSpec((tm, D_MODEL), lambda i: (i, 0)),
            pl.BlockSpec((1, D_MODEL), lambda i: (0, 0)),
            pl.BlockSpec((1, D_MODEL), lambda i: (0, 0)),
            pl.BlockSpec(memory_space=pl.ANY),
            pl.BlockSpec(memory_space=pl.ANY),
        ],
        out_specs=pl.BlockSpec((tm, D_MODEL), lambda i: (i, 0)),
        scratch_shapes=[
            pltpu.VMEM((tm, D_MODEL), BF16),
            pltpu.VMEM((2, tm, tf), BF16),
            pltpu.VMEM((2, D_MODEL, tf), F32),
            pltpu.VMEM((2, D_MODEL, tf), F32),
            pltpu.VMEM((2, tf, D_MODEL), F32),
            pltpu.SemaphoreType.DMA((3, 2)),
        ],
        compiler_params=_cparams("arbitrary"),
        name="ffn_final" if final_norm else "ffn",
    )(x, g, gf, wi, wo)


def _rope_table_body(pos_ref, freq_ref, sign_ref, cos_ref, sin_ref):
    ang = pos_ref[...].astype(F32) * freq_ref[...]
    cos_ref[...] = jnp.cos(ang)
    sin_ref[...] = jnp.sin(ang) * sign_ref[...]


def _rope_tables(positions):
    t = positions.size
    half = HEAD_DIM // 2
    inv_freq = ROPE_THETA ** (-jnp.arange(half, dtype=F32) / half)
    freq = jnp.tile(inv_freq, LANES // half)[None, :]
    sign = jnp.tile(jnp.concatenate([-jnp.ones((half,), F32), jnp.ones((half,), F32)]),
                    LANES // HEAD_DIM)[None, :]
    tm = ROPE_TM
    return pl.pallas_call(
        _rope_table_body,
        out_shape=(jax.ShapeDtypeStruct((t, LANES), F32), jax.ShapeDtypeStruct((t, LANES), F32)),
        grid=(t // tm,),
        in_specs=[
            pl.BlockSpec((tm, 1), lambda i: (i, 0)),
            pl.BlockSpec((1, LANES), lambda i: (0, 0)),
            pl.BlockSpec((1, LANES), lambda i: (0, 0)),
        ],
        out_specs=(pl.BlockSpec((tm, LANES), lambda i: (i, 0)),
                   pl.BlockSpec((tm, LANES), lambda i: (i, 0))),
        compiler_params=_cparams("parallel"),
        name="rope_tables",
    )(positions.reshape(t, 1), freq, sign)


def _swap_halves(x):
    lane = lax.broadcasted_iota(jnp.int32, x.shape, 1)
    first = (lane % HEAD_DIM) < (HEAD_DIM // 2)
    return jnp.where(first, pltpu.roll(x, LANES - HEAD_DIM // 2, 1), pltpu.roll(x, HEAD_DIM // 2, 1))


def _rope(x, cos, sin):
    cols = []
    for c in range(x.shape[1] // LANES):
        xc = x[:, c * LANES:(c + 1) * LANES]
        cols.append(xc * cos + _swap_halves(xc) * sin)
    return jnp.concatenate(cols, axis=1)


def _pool_mix(u, halo, tok0):
    ts = u.shape[0]
    a1 = jnp.concatenate([halo, u], axis=0)
    length = ts + POOL_HALO
    a2 = a1[1:] + a1[:length - 1]
    a4 = a2[2:] + a2[:length - 3]
    a8 = a4[4:] + a4[:length - 7]
    a16 = a8[8:] + a8[:length - 15]
    sums = {2: a2[POOL_HALO - 1:], 4: a4[POOL_HALO - 3:], 8: a8[POOL_HALO - 7:], 16: a16[POOL_HALO - 15:]}

    tok = tok0 + lax.broadcasted_iota(jnp.int32, (ts, 1), 0)
    chan = lax.broadcasted_iota(jnp.int32, (1, POOL_WIDTH), 1)
    d = jnp.zeros((ts, POOL_WIDTH), F32)
    for gi, w in enumerate(POOL_WINDOWS):
        inv_cnt = 1.0 / jnp.minimum(tok + 1, w).astype(F32)
        in_group = (chan >= gi * POOL_GROUP_DIM) & (chan < (gi + 1) * POOL_GROUP_DIM)
        d = jnp.where(in_group, sums[w] * inv_cnt, d)
    return d - u


def _inproj_body(sink_ref, x_ref, g_ref, w_ref, wlr_ref, wa2_ref, ba_ref, cos_ref, sin_ref, wbd_ref,
                 ps_ref, proj_ref, gk_ref, h_ref, ypool_ref, yatt_ref, halo_ref, kprev_ref, vprev_ref,
                 *, tiles_per_seq):
    tm = x_ref.shape[0]
    blk = ATT_BLOCK
    h = _rms(x_ref[...], g_ref[...]).astype(BF16)
    h_ref[...] = h
    s_idx = pl.program_id(0) % tiles_per_seq

    @pl.when(s_idx == 0)
    def _():
        halo_ref[...] = jnp.zeros_like(halo_ref)
        kprev_ref[...] = jnp.zeros_like(kprev_ref)
        vprev_ref[...] = jnp.zeros_like(vprev_ref)

    pu = jnp.dot(h, w_ref[:, _IN_PU:_IN_PU + POOL_WIDTH], preferred_element_type=F32)
    d = _pool_mix(pu, halo_ref[...], s_idx * tm).astype(BF16)
    halo_ref[...] = pu[tm - POOL_HALO:]

    lr = jnp.dot(h, wlr_ref[...], preferred_element_type=F32).astype(BF16)

    cos, sin = cos_ref[...], sin_ref[...]
    scale = HEAD_DIM ** -0.5
    q = _rope(jnp.dot(h, w_ref[:, _IN_QA:_IN_QA + ATT_Q_W], preferred_element_type=F32),
              cos * scale, sin * scale).astype(BF16)
    k = _rope(jnp.dot(h, w_ref[:, _IN_KA:_IN_KA + ATT_KV_W], preferred_element_type=F32),
              cos, sin).astype(BF16)
    v = jnp.dot(h, w_ref[:, _IN_VA:_IN_VA + ATT_KV_W], preferred_element_type=F32).astype(BF16)
    k = jnp.concatenate([kprev_ref[...], k], axis=0)
    v = jnp.concatenate([vprev_ref[...], v], axis=0)
    kprev_ref[...] = k[tm:]
    vprev_ref[...] = v[tm:]

    for dst, src, width in _PROJ_SEGMENTS:
        p = jnp.dot(h, w_ref[:, src:src + width], preferred_element_type=F32)
        proj_ref[:, dst:dst + width] = p.astype(proj_ref.dtype)

    for r in range(tm // blk):
        has_prev = (s_idx > 0) if r == 0 else True
        o = _attend_block(q[r * blk:(r + 1) * blk], k[r * blk:(r + 2) * blk], v[r * blk:(r + 2) * blk],
                          has_prev, sink_ref)
        yatt_ref[r * blk:(r + 1) * blk, :] = o.astype(yatt_ref.dtype)

    z = jnp.dot(lr, wa2_ref[...], preferred_element_type=F32) + ba_ref[...]
    gk_ref[...] = (jnp.minimum(z, 0.0) - jnp.log1p(jnp.exp(-jnp.abs(z)))) * (1.0 / GLA_TAU)
    y = jnp.dot(d, wbd_ref[...], preferred_element_type=F32)
    ypool_ref[...] = (y * ps_ref[...]).astype(ypool_ref.dtype)


def _inproj(x, g, w_in, w_a2, b_a, cos_t, sin_t, w_bd, pool_scale, sinks, seq, layer):
    t = x.shape[0]
    tm = INPROJ_TM
    assert seq % tm == 0 and tm % ATT_BLOCK == 0
    const = lambda i: (0, 0)
    lconst = lambda i: (layer, 0, 0)
    rows = lambda i: (i, 0)
    return pl.pallas_call(
        functools.partial(_inproj_body, tiles_per_seq=seq // tm),
        out_shape=(jax.ShapeDtypeStruct((t, PROJ_W), BF16),
                   jax.ShapeDtypeStruct((t, GLA_QK_W), F32),
                   jax.ShapeDtypeStruct((t, D_MODEL), BF16),
                   jax.ShapeDtypeStruct((t, POOL_WIDTH), BF16),
                   jax.ShapeDtypeStruct((t, ATT_Q_W), BF16)),
        grid=(t // tm,),
        in_specs=[
            pl.BlockSpec(memory_space=pltpu.SMEM),
            pl.BlockSpec((tm, D_MODEL), rows),
            pl.BlockSpec((1, D_MODEL), const),
            pl.BlockSpec((None, D_MODEL, PROJ_SRC_W), lconst, pipeline_mode=pl.Buffered(1)),
            pl.BlockSpec((None, D_MODEL, LANES), lambda i: (layer, 0, _IN_LR // LANES)),
            pl.BlockSpec((None, LANES, GLA_QK_W), lconst),
            pl.BlockSpec((1, GLA_QK_W), const),
            pl.BlockSpec((tm, LANES), rows),
            pl.BlockSpec((tm, LANES), rows),
            pl.BlockSpec((None, POOL_WIDTH, POOL_WIDTH), lconst),
            pl.BlockSpec((1, POOL_WIDTH), const),
        ],
        out_specs=(pl.BlockSpec((tm, PROJ_W), rows),
                   pl.BlockSpec((tm, GLA_QK_W), rows),
                   pl.BlockSpec((tm, D_MODEL), rows),
                   pl.BlockSpec((tm, POOL_WIDTH), rows),
                   pl.BlockSpec((tm, ATT_Q_W), rows)),
        scratch_shapes=[pltpu.VMEM((POOL_HALO, POOL_WIDTH), F32),
                        pltpu.VMEM((ATT_BLOCK, ATT_KV_W), BF16),
                        pltpu.VMEM((ATT_BLOCK, ATT_KV_W), BF16)],
        compiler_params=_cparams("arbitrary"),
        name="inproj",
    )(sinks, x, g, w_in, w_in, w_a2, b_a, cos_t, sin_t, w_bd, pool_scale)


def _both_halves(x, col, half):
    xc = x[:, col * LANES:(col + 1) * LANES].astype(F32)
    sw = pltpu.roll(xc, LANES // 2, 1)
    first = lax.broadcasted_iota(jnp.int32, xc.shape, 1) < LANES // 2
    dup = jnp.where(first, xc, sw) if half == 0 else jnp.where(first, sw, xc)
    return dup.astype(BF16)


def _attend_block(q, k, v, has_prev, sink_ref):
    blk = ATT_BLOCK
    group = ATT_HEADS // ATT_KV_HEADS

    qi = lax.broadcasted_iota(jnp.int32, (blk, 2 * blk), 0)
    si = lax.broadcasted_iota(jnp.int32, (blk, 2 * blk), 1)
    rel = blk + qi - si
    mask = (rel >= 0) & (rel < blk) & ((si >= blk) | has_prev)
    bias = jnp.where(mask, 0.0, -1e30)
    first = lax.broadcasted_iota(jnp.int32, (blk, LANES), 1) < LANES // 2
    zero = jnp.zeros((blk, LANES), BF16)

    heads = [None] * ATT_HEADS
    for g in range(ATT_KV_HEADS):
        kk = _both_halves(k, g // 2, g % 2)
        vv = _both_halves(v, g // 2, g % 2)
        hs = range(g * group, (g + 1) * group)
        qm = jnp.concatenate(
            [jnp.where(first if hd % 2 == 0 else ~first, q[:, (hd // 2) * LANES:(hd // 2 + 1) * LANES], zero)
             for hd in hs], axis=0)
        s3 = lax.dot_general(qm, kk, (((1,), (1,)), ((), ())), preferred_element_type=F32)
        es, denoms = [], []
        for r, hd in enumerate(hs):
            s = s3[r * blk:(r + 1) * blk] + bias
            sink = sink_ref[hd]
            m = jnp.maximum(jnp.max(s, axis=1, keepdims=True), sink)
            e = jnp.exp(s - m)
            denoms.append(jnp.sum(e, axis=1, keepdims=True) + jnp.exp(sink - m))
            es.append(e.astype(BF16))
        o3 = jnp.dot(jnp.concatenate(es, axis=0), vv, preferred_element_type=F32)
        for r, hd in enumerate(hs):
            heads[hd] = o3[r * blk:(r + 1) * blk] / denoms[r]
    cols = [jnp.where(first, heads[2 * c], heads[2 * c + 1]) for c in range(ATT_HEADS // 2)]
    return jnp.concatenate(cols, axis=1)


def _gla_body(qk_ref, v_ref, og_ref, gk_ref, norm_ref, tril_ref, bd_ref, o_ref, st_ref, oacc_ref):
    s_idx = pl.program_id(1)
    npair, ts = qk_ref.shape[0], qk_ref.shape[1]
    c = GLA_CHUNK

    @pl.when(s_idx == 0)
    def _():
        st_ref[...] = jnp.zeros_like(st_ref)

    tril = tril_ref[...]
    prep = []
    for p in range(npair):
        qk = qk_ref[p].astype(F32)
        q = qk[:, :GLA_QK_W] * (GLA_DK ** -0.5)
        k = qk[:, GLA_QK_W:]
        gk = gk_ref[p]
        g1 = gk.astype(BF16)
        r1 = gk - g1.astype(F32)
        g2 = r1.astype(BF16)
        g3 = (r1 - g2.astype(F32)).astype(BF16)
        b = (jnp.dot(tril, g1, preferred_element_type=F32) + jnp.dot(tril, g2, preferred_element_type=F32)
             + jnp.dot(tril, g3, preferred_element_type=F32))
        prep.append(((q * jnp.exp(b)).astype(BF16), (k * jnp.exp(-b)).astype(BF16), k, b, v_ref[p]))

    kcol = lax.broadcasted_iota(jnp.int32, (1, GLA_QK_W), 1)
    vcol = lax.broadcasted_iota(jnp.int32, (1, GLA_V_W), 1)
    ri = lax.broadcasted_iota(jnp.int32, (c, c), 0)
    ci = lax.broadcasted_iota(jnp.int32, (c, c), 1)
    causal = ri >= ci
    bd = bd_ref[...]

    for n in range(ts // c):
        rows = slice(n * c, (n + 1) * c)
        for p in range(npair):
            q_t, k_t, k, b, v = prep[p]
            b_n = b[rows]
            b_last = b_n[c - 1:c, :]
            q_n = q_t[rows]
            k_n = k_t[rows]
            v_n = v[rows]
            k_s = (k[rows] * jnp.exp(b_last - b_n)).astype(BF16)
            q4 = jnp.concatenate(
                [jnp.where((kcol >= h * GLA_DK) & (kcol < (h + 1) * GLA_DK), q_n, jnp.zeros_like(q_n))
                 for h in range(GLA_HEADS)], axis=0)
            a4 = lax.dot_general(q4, k_n, (((1,), (1,)), ((), ())), preferred_element_type=F32)
            o_n = jnp.zeros((c, GLA_V_W), F32)
            for h in range(GLA_HEADS):
                a_h = jnp.where(causal, a4[h * c:(h + 1) * c], 0.0).astype(BF16)
                o_h = jnp.dot(a_h, v_n, preferred_element_type=F32)
                o_n = jnp.where((vcol >= h * GLA_DV) & (vcol < (h + 1) * GLA_DV), o_h, o_n)
            st = st_ref[p]
            o_n = o_n + lax.dot_general(q_n, st.astype(BF16), (((1,), (1,)), ((), ())),
                                        preferred_element_type=F32)
            kv = lax.dot_general(v_n, k_s, (((0,), (0,)), ((), ())), preferred_element_type=F32)
            st_ref[p] = st * jnp.exp(b_last) + kv * bd
            oacc_ref[p, rows, :] = o_n

    for p in range(npair):
        o = oacc_ref[p]
        o2 = o * o
        inv = jnp.zeros_like(o)
        for h in range(GLA_HEADS):
            in_head = (vcol >= h * GLA_DV) & (vcol < (h + 1) * GLA_DV)
            ms = jnp.sum(jnp.where(in_head, o2, 0.0), axis=1, keepdims=True) * (1.0 / GLA_DV)
            inv = jnp.where(in_head, lax.rsqrt(ms + EPS), inv)
        og = og_ref[p].astype(F32)
        o_ref[p] = (o * inv * norm_ref[...] * (og * jax.nn.sigmoid(og))).astype(o_ref.dtype)


def _gla(proj, gk, norm, batch, seq):
    t = proj.shape[0]
    ts = GLA_TS
    ns = seq // ts
    pair = GLA_PAIR
    assert batch % pair == 0
    r = np.arange(ts)
    tril = ((r[:, None] // GLA_CHUNK == r[None, :] // GLA_CHUNK) & (r[:, None] >= r[None, :]))
    tril = jnp.asarray(tril, BF16)
    bd = (np.arange(GLA_V_W)[:, None] // GLA_DV) == (np.arange(GLA_QK_W)[None, :] // GLA_DK)
    bd = jnp.asarray(bd, F32)
    proj3 = proj.reshape(batch, seq, PROJ_W)
    out = pl.pallas_call(
        _gla_body,
        out_shape=jax.ShapeDtypeStruct((batch, seq, GLA_V_W), BF16),
        grid=(batch // pair, ns),
        in_specs=[
            pl.BlockSpec((pair, ts, 2 * GLA_QK_W), lambda b, s: (b, s, _BLK_GQK)),
            pl.BlockSpec((pair, ts, GLA_V_W), lambda b, s: (b, s, _BLK_GV)),
            pl.BlockSpec((pair, ts, GLA_V_W), lambda b, s: (b, s, _BLK_GO)),
            pl.BlockSpec((pair, ts, GLA_QK_W), lambda b, s: (b, s, 0)),
            pl.BlockSpec((1, GLA_V_W), lambda b, s: (0, 0)),
            pl.BlockSpec((ts, ts), lambda b, s: (0, 0)),
            pl.BlockSpec((GLA_V_W, GLA_QK_W), lambda b, s: (0, 0)),
        ],
        out_specs=pl.BlockSpec((pair, ts, GLA_V_W), lambda b, s: (b, s, 0)),
        scratch_shapes=[pltpu.VMEM((pair, GLA_V_W, GLA_QK_W), F32),
                        pltpu.VMEM((pair, ts, GLA_V_W), F32)],
        compiler_params=_cparams("parallel", "arbitrary"),
        name="gla",
    )(proj3, proj3, proj3, gk.reshape(batch, seq, GLA_QK_W), norm, tril, bd)
    return out.reshape(t, GLA_V_W)


def _merge_body(h_ref, ya_ref, yp_ref, yg_ref, wga_ref, wgp_ref, wgg_ref, tga_ref, tgp_ref, tgg_ref,
                bga_ref, bgp_ref, bgg_ref, wa_ref, wp_ref, wg_ref, o_ref, wgate_ref):
    tn = o_ref.shape[1]

    @pl.when(pl.program_id(1) == 0)
    def _():
        for br, (win_ref, tail_ref) in enumerate(((wga_ref, tga_ref), (wgp_ref, tgp_ref),
                                                  (wgg_ref, tgg_ref))):
            win = jnp.concatenate([win_ref[...], tail_ref[...]], axis=1).astype(F32)
            wgate_ref[br] = win[:, GATE_LEAD:GATE_LEAD + tn].astype(BF16)

    h = h_ref[...]
    acc = None
    for br, (y_ref, bgate_ref, wbr_ref) in enumerate(((ya_ref, bga_ref, wa_ref),
                                                     (yp_ref, bgp_ref, wp_ref),
                                                     (yg_ref, bgg_ref, wg_ref))):
        gate = jax.nn.sigmoid(jnp.dot(h, wgate_ref[br], preferred_element_type=F32) + bgate_ref[...])
        term = gate * jnp.dot(y_ref[...], wbr_ref[...], preferred_element_type=F32)
        acc = term if acc is None else acc + term
    o_ref[...] = acc.astype(o_ref.dtype)


def _merge(h, y_att, y_pool, y_gla, w_in, b_gate, w_br_att, w_br_pool, w_br_gla, layer):
    t = h.shape[0]
    tm, tn = MERGE_TM, MERGE_TN
    nn = D_MODEL // tn
    row = lambda j, i: (i, 0)
    y_spec = pl.BlockSpec((tm, ATT_Q_W), row)
    gate_base = _IN_GATE - GATE_LEAD
    gate_specs = [pl.BlockSpec((pl.Squeezed(), pl.Element(D_MODEL), pl.Element(tn)),
                               functools.partial(
                                   lambda j, i, br: (layer, 0, pl.multiple_of(
                                       gate_base + br * D_MODEL + j * tn, LANES)), br=br))
                  for br in range(N_BRANCH)]
    tail_specs = [pl.BlockSpec((None, D_MODEL, LANES),
                               functools.partial(
                                   lambda j, i, br: (layer, 0, (gate_base + br * D_MODEL) // LANES
                                                     + (j + 1) * (tn // LANES)), br=br))
                  for br in range(N_BRANCH)]
    bias_specs = [pl.BlockSpec((1, tn), functools.partial(lambda j, i, br: (0, br * nn + j), br=br))
                  for br in range(N_BRANCH)]
    br_spec = pl.BlockSpec((None, ATT_Q_W, tn), lambda j, i: (layer, 0, j))
    return pl.pallas_call(
        _merge_body,
        out_shape=jax.ShapeDtypeStruct((t, D_MODEL), BF16),
        grid=(nn, t // tm),
        in_specs=[pl.BlockSpec((tm, D_MODEL), row), y_spec, y_spec, y_spec,
                  *gate_specs, *tail_specs, *bias_specs, br_spec, br_spec, br_spec],
        out_specs=pl.BlockSpec((tm, tn), lambda j, i: (i, j)),
        scratch_shapes=[pltpu.VMEM((N_BRANCH, D_MODEL, tn), BF16)],
        compiler_params=_cparams("arbitrary", "arbitrary"),
        name="gated_merge",
    )(h, y_att, y_pool, y_gla, w_in, w_in, w_in, w_in, w_in, w_in, b_gate, b_gate, b_gate,
      w_br_att, w_br_pool, w_br_gla)


def _outproj_body(m_ref, w_ref, x_ref, o_ref):
    o_ref[...] = x_ref[...] + jnp.dot(m_ref[...], w_ref[...], preferred_element_type=F32)


def _outproj(merged, w_out, x, layer):
    t = x.shape[0]
    tm, tn = OUT_TM, OUT_TN
    return pl.pallas_call(
        _outproj_body,
        out_shape=jax.ShapeDtypeStruct((t, D_MODEL), F32),
        grid=(t // tm, D_MODEL // tn),
        in_specs=[
            pl.BlockSpec((tm, D_MODEL), lambda i, j: (i, 0)),
            pl.BlockSpec((None, D_MODEL, tn), lambda i, j: (layer, 0, j)),
            pl.BlockSpec((tm, tn), lambda i, j: (i, j)),
        ],
        out_specs=pl.BlockSpec((tm, tn), lambda i, j: (i, j)),
        compiler_params=_cparams("parallel", "arbitrary"),
        name="outproj",
    )(merged, w_out, x)


def _prep_mixer_weights(w_in, w_pool, w_gla_a2, w_br_att, w_br_pool, w_br_gla, w_out):
    depth = w_in.shape[0]
    w_in_bf = w_in.astype(BF16)
    w_a2 = jnp.pad(w_gla_a2.astype(BF16), ((0, 0), (0, LANES - GLA_LOWRANK), (0, 0)))
    groups = len(POOL_WINDOWS)
    eye = jnp.eye(groups, dtype=BF16)
    w_bd = (w_pool.astype(BF16)[:, :, :, None, :] * eye[None, :, None, :, None]).reshape(
        depth, POOL_WIDTH, POOL_WIDTH)
    return dict(w_in=w_in_bf, w_a2=w_a2, w_bd=w_bd,
                w_br_att=w_br_att.astype(BF16), w_br_pool=w_br_pool.astype(BF16),
                w_br_gla=w_br_gla.astype(BF16), w_out=w_out.astype(BF16))


def _mixer(x, cos_t, sin_t, batch, seq, layer, mw, norm_mix, b_gate, att_sinks, pool_scale,
           b_gla_a, gla_norm):
    proj, gk, h, y_pool, y_att = _inproj(x, norm_mix[None, :], mw["w_in"], mw["w_a2"],
                                         b_gla_a[None, :], cos_t, sin_t, mw["w_bd"],
                                         pool_scale[None, :], att_sinks, seq, layer)
    y_gla = _gla(proj, gk, gla_norm[None, :], batch, seq)
    merged = _merge(h, y_att, y_pool, y_gla, mw["w_in"], b_gate[None, :],
                    mw["w_br_att"], mw["w_br_pool"], mw["w_br_gla"], layer)
    return _outproj(merged, mw["w_out"], x, layer)


def kernel(x, positions, norm_ffn1, ffn1_wi, ffn1_wo, norm_mix, w_in, b_gate, att_sinks, w_pool, pool_scale, w_gla_a2, b_gla_a, gla_norm, w_br_att, w_br_pool, w_br_gla, w_out, norm_ffn2, ffn2_wi, ffn2_wo, norm_final):
    batch, seq, d = x.shape
    depth = norm_ffn1.shape[0]
    assert d == D_MODEL and seq % max(INPROJ_TM, GLA_TS, ATT_BLOCK) == 0
    t = batch * seq
    assert t % max(FFN_TM, MERGE_TM, OUT_TM, ROPE_TM, INPROJ_TM) == 0
    xt = x.reshape(t, d)
    cos_t, sin_t = _rope_tables(positions)
    gf = norm_final[None, :]
    mw = _prep_mixer_weights(w_in, w_pool, w_gla_a2, w_br_att, w_br_pool, w_br_gla, w_out)
    for l in range(depth):
        xt = _ffn(xt, norm_ffn1[l][None, :], ffn1_wi, ffn1_wo, gf, l, False)
        xt = _mixer(xt, cos_t, sin_t, batch, seq, l, mw, norm_mix[l], b_gate[l], att_sinks[l],
                    pool_scale[l], b_gla_a[l], gla_norm[l])
        xt = _ffn(xt, norm_ffn2[l][None, :], ffn2_wi, ffn2_wo, gf, l, l == depth - 1)
    return xt.reshape(batch, seq, d)
```

```python
import functools

import numpy as np
import jax
import jax.numpy as jnp
from jax import lax
from jax.experimental import pallas as pl
from jax.experimental.pallas import tpu as pltpu

F32 = jnp.float32
BF16 = jnp.bfloat16

D_MODEL = 2048
D_FF = 5632
ATT_HEADS = 12
ATT_KV_HEADS = 4
HEAD_DIM = 64
ATT_BLOCK = 128
ROPE_THETA = 10000.0
ATT_Q_W = ATT_HEADS * HEAD_DIM
ATT_KV_W = ATT_KV_HEADS * HEAD_DIM
POOL_WINDOWS = (2, 4, 8, 16)
POOL_GROUP_DIM = 192
POOL_WIDTH = 768
GLA_HEADS = 4
GLA_DK = 96
GLA_DV = 192
GLA_QK_W = GLA_HEADS * GLA_DK
GLA_V_W = GLA_HEADS * GLA_DV
GLA_LOWRANK = 16
GLA_TAU = 16.0
GLA_CHUNK = 64
N_BRANCH = 3
EPS = 1e-6

_IN_QA, _IN_KA, _IN_VA, _IN_PU = 0, 768, 1024, 1280
_IN_QG, _IN_KG, _IN_VG, _IN_OG = 2048, 2432, 2816, 3584
_IN_LR = 4352
_IN_GATE = 4368

PROJ_SRC_W = 4352
PROJ_W = 2304
_BLK_GQK, _BLK_GV, _BLK_GO = 0, 1, 2
_PROJ_SEGMENTS = ((0, _IN_QG, 768), (768, _IN_VG, 1536))
W_IN_COLS = _IN_GATE + N_BRANCH * D_MODEL
GATE_LEAD = _IN_GATE % 128

LANES = 128
POOL_HALO = 16
VMEM_LIMIT = 58 * 1024 * 1024

FFN_TM, FFN_TF = 1024, 256
FFN_UNROLL = 4
INPROJ_TM = 512
GLA_TS = 256
GLA_PAIR = 2
MERGE_TM, MERGE_TN = 1024, 512
OUT_TM, OUT_TN = 512, 2048
ROPE_TM = 1024


def _cparams(*sem):
    return pltpu.CompilerParams(dimension_semantics=sem, vmem_limit_bytes=VMEM_LIMIT)


def _rms(x, g):
    return x * lax.rsqrt(jnp.mean(x * x, axis=-1, keepdims=True) + EPS) * g


def _ffn_body(x_ref, g_ref, gf_ref, wi_hbm, wo_hbm, o_ref, h_ref, u_ref, wa_buf, wb_buf, wo_buf, sem,
              *, layer, n_tiles, final_norm):
    tf = FFN_TF
    nch = D_FF // tf
    i = pl.program_id(0)

    def up_copies(c, slot):
        col = pl.multiple_of(c * tf, tf)
        return (pltpu.make_async_copy(wi_hbm.at[layer, :, pl.ds(col, tf)], wa_buf.at[slot], sem.at[0, slot]),
                pltpu.make_async_copy(wi_hbm.at[layer, :, pl.ds(D_FF + col, tf)], wb_buf.at[slot],
                                      sem.at[1, slot]))

    def down_copy(c, slot):
        row = pl.multiple_of(c * tf, tf)
        return pltpu.make_async_copy(wo_hbm.at[layer, pl.ds(row, tf), :], wo_buf.at[slot], sem.at[2, slot])

    def up(slot):
        h = h_ref[...]
        a = jnp.dot(h, wa_buf[slot].astype(BF16), preferred_element_type=F32)
        b = jnp.dot(h, wb_buf[slot].astype(BF16), preferred_element_type=F32)
        u_ref[slot] = (a * jax.nn.sigmoid(a) * (0.5 * b)).astype(BF16)

    def down(slot):
        o_ref[...] += jnp.dot(u_ref[slot], wo_buf[slot].astype(BF16), preferred_element_type=F32)

    def start_up(c, slot):
        ca, cb = up_copies(c, slot)
        ca.start(priority=1)
        cb.start(priority=1)

    def stage(c, slot):
        for cp in up_copies(c, slot):
            cp.wait()
        down_copy(c - 1, 1 - slot).wait()
        down_copy(c, slot).start(priority=1)
        down(1 - slot)
        up(slot)

    @pl.when(i == 0)
    def _():
        start_up(0, 0)

    x = x_ref[...]
    h_ref[...] = _rms(x, g_ref[...]).astype(BF16)
    o_ref[...] = x
    for cp in up_copies(0, 0):
        cp.wait()
    start_up(1, 1)
    down_copy(0, 0).start(priority=1)
    up(0)

    def stages(k, carry):
        c0 = FFN_UNROLL * k + 1
        for s in range(FFN_UNROLL):
            slot = (1 + s) % 2
            start_up(c0 + s + 1, 1 - slot)
            stage(c0 + s, slot)
        return carry

    lax.fori_loop(0, (nch - 2) // FFN_UNROLL, stages, 0)

    stage(nch - 1, 1)

    @pl.when(i + 1 < n_tiles)
    def _():
        start_up(0, 0)

    down_copy(nch - 1, 1).wait()
    down(1)
    if final_norm:
        o_ref[...] = _rms(o_ref[...], gf_ref[...])


def _ffn(x, g, wi, wo, gf, layer, final_norm):
    t = x.shape[0]
    tm, tf = FFN_TM, FFN_TF
    assert FFN_UNROLL % 2 == 0 and (D_FF // tf - 2) % FFN_UNROLL == 0
    n_tiles = t // tm
    return pl.pallas_call(
        functools.partial(_ffn_body, layer=layer, n_tiles=n_tiles, final_norm=final_norm),
        out_shape=jax.ShapeDtypeStruct((t, D_MODEL), F32),
        grid=(n_tiles,),
        in_specs=[
            pl.BlockSpec((tm, D_MODEL), lambda i: (i, 0)),
            pl.BlockSpec((1, D_MODEL), lambda i: (0, 0)),
            pl.BlockSpec((1, D_MODEL), lambda i: (0, 0)),
            pl.BlockSpec(memory_space=pl.ANY),
            pl.BlockSpec(memory_space=pl.ANY),
        ],
        out_specs=pl.BlockSpec((tm, D_MODEL), lambda i: (i, 0)),
        scratch_shapes=[
            pltpu.VMEM((tm, D_MODEL), BF16),
            pltpu.VMEM((2, tm, tf), BF16),
            pltpu.VMEM((2, D_MODEL, tf), F32),
            pltpu.VMEM((2, D_MODEL, tf), F32),
            pltpu.VMEM((2, tf, D_MODEL), F32),
            pltpu.SemaphoreType.DMA((3, 2)),
        ],
        compiler_params=_cparams("arbitrary"),
        name="ffn_final" if final_norm else "ffn",
    )(x, g, gf, wi, wo)


def _rope_table_body(pos_ref, freq_ref, sign_ref, cos_ref, sin_ref):
    ang = pos_ref[...].astype(F32) * freq_ref[...]
    cos_ref[...] = jnp.cos(ang)
    sin_ref[...] = jnp.sin(ang) * sign_ref[...]


def _rope_tables(positions):
    t = positions.size
    half = HEAD_DIM // 2
    inv_freq = ROPE_THETA ** (-jnp.arange(half, dtype=F32) / half)
    freq = jnp.tile(inv_freq, LANES // half)[None, :]
    sign = jnp.tile(jnp.concatenate([-jnp.ones((half,), F32), jnp.ones((half,), F32)]),
                    LANES // HEAD_DIM)[None, :]
    tm = ROPE_TM
    return pl.pallas_call(
        _rope_table_body,
        out_shape=(jax.ShapeDtypeStruct((t, LANES), F32), jax.ShapeDtypeStruct((t, LANES), F32)),
        grid=(t // tm,),
        in_specs=[
            pl.BlockSpec((tm, 1), lambda i: (i, 0)),
            pl.BlockSpec((1, LANES), lambda i: (0, 0)),
            pl.BlockSpec((1, LANES), lambda i: (0, 0)),
        ],
        out_specs=(pl.BlockSpec((tm, LANES), lambda i: (i, 0)),
                   pl.BlockSpec((tm, LANES), lambda i: (i, 0))),
        compiler_params=_cparams("parallel"),
        name="rope_tables",
    )(positions.reshape(t, 1), freq, sign)


def _swap_halves(x):
    lane = lax.broadcasted_iota(jnp.int32, x.shape, 1)
    first = (lane % HEAD_DIM) < (HEAD_DIM // 2)
    return jnp.where(first, pltpu.roll(x, LANES - HEAD_DIM // 2, 1), pltpu.roll(x, HEAD_DIM // 2, 1))


def _rope(x, cos, sin):
    cols = []
    for c in range(x.shape[1] // LANES):
        xc = x[:, c * LANES:(c + 1) * LANES]
        cols.append(xc * cos + _swap_halves(xc) * sin)
    return jnp.concatenate(cols, axis=1)


def _pool_mix(u, halo, tok0):
    ts = u.shape[0]
    a1 = jnp.concatenate([halo, u], axis=0)
    length = ts + POOL_HALO
    a2 = a1[1:] + a1[:length - 1]
    a4 = a2[2:] + a2[:length - 3]
    a8 = a4[4:] + a4[:length - 7]
    a16 = a8[8:] + a8[:length - 15]
    sums = {2: a2[POOL_HALO - 1:], 4: a4[POOL_HALO - 3:], 8: a8[POOL_HALO - 7:], 16: a16[POOL_HALO - 15:]}

    tok = tok0 + lax.broadcasted_iota(jnp.int32, (ts, 1), 0)
    chan = lax.broadcasted_iota(jnp.int32, (1, POOL_WIDTH), 1)
    d = jnp.zeros((ts, POOL_WIDTH), F32)
    for gi, w in enumerate(POOL_WINDOWS):
        inv_cnt = 1.0 / jnp.minimum(tok + 1, w).astype(F32)
        in_group = (chan >= gi * POOL_GROUP_DIM) & (chan < (gi + 1) * POOL_GROUP_DIM)
        d = jnp.where(in_group, sums[w] * inv_cnt, d)
    return d - u


def _inproj_body(sink_ref, x_ref, g_ref, w_ref, wlr_ref, wa2_ref, ba_ref, cos_ref, sin_ref, wbd_ref,
                 ps_ref, proj_ref, gk_ref, h_ref, ypool_ref, yatt_ref, halo_ref, kprev_ref, vprev_ref,
                 *, tiles_per_seq):
    tm = x_ref.shape[0]
    blk = ATT_BLOCK
    h = _rms(x_ref[...], g_ref[...]).astype(BF16)
    h_ref[...] = h
    s_idx = pl.program_id(0) % tiles_per_seq

    @pl.when(s_idx == 0)
    def _():
        halo_ref[...] = jnp.zeros_like(halo_ref)
        kprev_ref[...] = jnp.zeros_like(kprev_ref)
        vprev_ref[...] = jnp.zeros_like(vprev_ref)

    pu = jnp.dot(h, w_ref[:, _IN_PU:_IN_PU + POOL_WIDTH], preferred_element_type=F32)
    d = _pool_mix(pu, halo_ref[...], s_idx * tm).astype(BF16)
    halo_ref[...] = pu[tm - POOL_HALO:]

    lr = jnp.dot(h, wlr_ref[...], preferred_element_type=F32).astype(BF16)

    cos, sin = cos_ref[...], sin_ref[...]
    scale = HEAD_DIM ** -0.5
    q = _rope(jnp.dot(h, w_ref[:, _IN_QA:_IN_QA + ATT_Q_W], preferred_element_type=F32),
              cos * scale, sin * scale).astype(BF16)
    k = _rope(jnp.dot(h, w_ref[:, _IN_KA:_IN_KA + ATT_KV_W], preferred_element_type=F32),
              cos, sin).astype(BF16)
    v = jnp.dot(h, w_ref[:, _IN_VA:_IN_VA + ATT_KV_W], preferred_element_type=F32).astype(BF16)
    k = jnp.concatenate([kprev_ref[...], k], axis=0)
    v = jnp.concatenate([vprev_ref[...], v], axis=0)
    kprev_ref[...] = k[tm:]
    vprev_ref[...] = v[tm:]

    for dst, src, width in _PROJ_SEGMENTS:
        p = jnp.dot(h, w_ref[:, src:src + width], preferred_element_type=F32)
        proj_ref[:, dst:dst + width] = p.astype(proj_ref.dtype)

    for r in range(tm // blk):
        has_prev = (s_idx > 0) if r == 0 else True
        o = _attend_block(q[r * blk:(r + 1) * blk], k[r * blk:(r + 2) * blk], v[r * blk:(r + 2) * blk],
                          has_prev, sink_ref)
        yatt_ref[r * blk:(r + 1) * blk, :] = o.astype(yatt_ref.dtype)

    z = jnp.dot(lr, wa2_ref[...], preferred_element_type=F32) + ba_ref[...]
    gk_ref[...] = (jnp.minimum(z, 0.0) - jnp.log1p(jnp.exp(-jnp.abs(z)))) * (1.0 / GLA_TAU)
    y = jnp.dot(d, wbd_ref[...], preferred_element_type=F32)
    ypool_ref[...] = (y * ps_ref[...]).astype(ypool_ref.dtype)


def _inproj(x, g, w_in, w_a2, b_a, cos_t, sin_t, w_bd, pool_scale, sinks, seq, layer):
    t = x.shape[0]
    tm = INPROJ_TM
    assert seq % tm == 0 and tm % ATT_BLOCK == 0
    const = lambda i: (0, 0)
    lconst = lambda i: (layer, 0, 0)
    rows = lambda i: (i, 0)
    return pl.pallas_call(
        functools.partial(_inproj_body, tiles_per_seq=seq // tm),
        out_shape=(jax.ShapeDtypeStruct((t, PROJ_W), BF16),
                   jax.ShapeDtypeStruct((t, GLA_QK_W), F32),
                   jax.ShapeDtypeStruct((t, D_MODEL), BF16),
                   jax.ShapeDtypeStruct((t, POOL_WIDTH), BF16),
                   jax.ShapeDtypeStruct((t, ATT_Q_W), BF16)),
        grid=(t // tm,),
        in_specs=[
            pl.BlockSpec(memory_space=pltpu.SMEM),
            pl.BlockSpec((tm, D_MODEL), rows),
            pl.BlockSpec((1, D_MODEL), const),
            pl.BlockSpec((None, D_MODEL, PROJ_SRC_W), lconst, pipeline_mode=pl.Buffered(1)),
            pl.BlockSpec((None, D_MODEL, LANES), lambda i: (layer, 0, _IN_LR // LANES)),
            pl.BlockSpec((None, LANES, GLA_QK_W), lconst),
            pl.BlockSpec((1, GLA_QK_W), const),
            pl.BlockSpec((tm, LANES), rows),
            pl.BlockSpec((tm, LANES), rows),
            pl.BlockSpec((None, POOL_WIDTH, POOL_WIDTH), lconst),
            pl.BlockSpec((1, POOL_WIDTH), const),
        ],
        out_specs=(pl.BlockSpec((tm, PROJ_W), rows),
                   pl.BlockSpec((tm, GLA_QK_W), rows),
                   pl.BlockSpec((tm, D_MODEL), rows),
                   pl.BlockSpec((tm, POOL_WIDTH), rows),
                   pl.BlockSpec((tm, ATT_Q_W), rows)),
        scratch_shapes=[pltpu.VMEM((POOL_HALO, POOL_WIDTH), F32),
                        pltpu.VMEM((ATT_BLOCK, ATT_KV_W), BF16),
                        pltpu.VMEM((ATT_BLOCK, ATT_KV_W), BF16)],
        compiler_params=_cparams("arbitrary"),
        name="inproj",
    )(sinks, x, g, w_in, w_in, w_a2, b_a, cos_t, sin_t, w_bd, pool_scale)


def _both_halves(x, col, half):
    xc = x[:, col * LANES:(col + 1) * LANES].astype(F32)
    sw = pltpu.roll(xc, LANES // 2, 1)
    first = lax.broadcasted_iota(jnp.int32, xc.shape, 1) < LANES // 2
    dup = jnp.where(first, xc, sw) if half == 0 else jnp.where(first, sw, xc)
    return dup.astype(BF16)


def _attend_block(q, k, v, has_prev, sink_ref):
    blk = ATT_BLOCK
    group = ATT_HEADS // ATT_KV_HEADS

    qi = lax.broadcasted_iota(jnp.int32, (blk, 2 * blk), 0)
    si = lax.broadcasted_iota(jnp.int32, (blk, 2 * blk), 1)
    rel = blk + qi - si
    mask = (rel >= 0) & (rel < blk) & ((si >= blk) | has_prev)
    bias = jnp.where(mask, 0.0, -1e30)
    first = lax.broadcasted_iota(jnp.int32, (blk, LANES), 1) < LANES // 2
    zero = jnp.zeros((blk, LANES), BF16)

    heads = [None] * ATT_HEADS
    for g in range(ATT_KV_HEADS):
        kk = _both_halves(k, g // 2, g % 2)
        vv = _both_halves(v, g // 2, g % 2)
        hs = range(g * group, (g + 1) * group)
        qm = jnp.concatenate(
            [jnp.where(first if hd % 2 == 0 else ~first, q[:, (hd // 2) * LANES:(hd // 2 + 1) * LANES], zero)
             for hd in hs], axis=0)
        s3 = lax.dot_general(qm, kk, (((1,), (1,)), ((), ())), preferred_element_type=F32)
        es, denoms = [], []
        for r, hd in enumerate(hs):
            s = s3[r * blk:(r + 1) * blk] + bias
            sink = sink_ref[hd]
            m = jnp.maximum(jnp.max(s, axis=1, keepdims=True), sink)
            e = jnp.exp(s - m)
            denoms.append(jnp.sum(e, axis=1, keepdims=True) + jnp.exp(sink - m))
            es.append(e.astype(BF16))
        o3 = jnp.dot(jnp.concatenate(es, axis=0), vv, preferred_element_type=F32)
        for r, hd in enumerate(hs):
            heads[hd] = o3[r * blk:(r + 1) * blk] / denoms[r]
    cols = [jnp.where(first, heads[2 * c], heads[2 * c + 1]) for c in range(ATT_HEADS // 2)]
    return jnp.concatenate(cols, axis=1)


def _gla_body(qk_ref, v_ref, og_ref, gk_ref, norm_ref, tril_ref, bd_ref, o_ref, st_ref, oacc_ref):
    s_idx = pl.program_id(1)
    npair, ts = qk_ref.shape[0], qk_ref.shape[1]
    c = GLA_CHUNK

    @pl.when(s_idx == 0)
    def _():
        st_ref[...] = jnp.zeros_like(st_ref)

    tril = tril_ref[...]
    prep = []
    for p in range(npair):
        qk = qk_ref[p].astype(F32)
        q = qk[:, :GLA_QK_W] * (GLA_DK ** -0.5)
        k = qk[:, GLA_QK_W:]
        gk = gk_ref[p]
        g1 = gk.astype(BF16)
        r1 = gk - g1.astype(F32)
        g2 = r1.astype(BF16)
        g3 = (r1 - g2.astype(F32)).astype(BF16)
        b = (jnp.dot(tril, g1, preferred_element_type=F32) + jnp.dot(tril, g2, preferred_element_type=F32)
             + jnp.dot(tril, g3, preferred_element_type=F32))
        prep.append(((q * jnp.exp(b)).astype(BF16), (k * jnp.exp(-b)).astype(BF16), k, b, v_ref[p]))

    kcol = lax.broadcasted_iota(jnp.int32, (1, GLA_QK_W), 1)
    vcol = lax.broadcasted_iota(jnp.int32, (1, GLA_V_W), 1)
    ri = lax.broadcasted_iota(jnp.int32, (c, c), 0)
    ci = lax.broadcasted_iota(jnp.int32, (c, c), 1)
    causal = ri >= ci
    bd = bd_ref[...]

    units = [(n, p) for n in range(ts // c) for p in range(npair)]
    scores = {}
    for n, p in units:
        rows = slice(n * c, (n + 1) * c)
        q_n = prep[p][0][rows]
        q4 = jnp.concatenate(
            [jnp.where((kcol >= h * GLA_DK) & (kcol < (h + 1) * GLA_DK), q_n, jnp.zeros_like(q_n))
             for h in range(GLA_HEADS)], axis=0)
        scores[n, p] = lax.dot_general(q4, prep[p][1][rows], (((1,), (1,)), ((), ())),
                                       preferred_element_type=F32)
    for n, p in units:
        rows = slice(n * c, (n + 1) * c)
        v_n = prep[p][4][rows]
        o_n = jnp.zeros((c, GLA_V_W), F32)
        for h in range(GLA_HEADS):
            a_h = jnp.where(causal, scores[n, p][h * c:(h + 1) * c], 0.0).astype(BF16)
            o_h = jnp.dot(a_h, v_n, preferred_element_type=F32)
            o_n = jnp.where((vcol >= h * GLA_DV) & (vcol < (h + 1) * GLA_DV), o_h, o_n)
        oacc_ref[p, rows, :] = o_n
    for n, p in units:
        rows = slice(n * c, (n + 1) * c)
        q_t, _, k, b, v = prep[p]
        b_n = b[rows]
        b_last = b_n[c - 1:c, :]
        k_s = (k[rows] * jnp.exp(b_last - b_n)).astype(BF16)
        st = st_ref[p]
        oacc_ref[p, rows, :] += lax.dot_general(q_t[rows], st.astype(BF16), (((1,), (1,)), ((), ())),
                                                preferred_element_type=F32)
        kv = lax.dot_general(v[rows], k_s, (((0,), (0,)), ((), ())), preferred_element_type=F32)
        st_ref[p] = st * jnp.exp(b_last) + kv * bd

    for p in range(npair):
        o = oacc_ref[p]
        o2 = o * o
        inv = jnp.zeros_like(o)
        for h in range(GLA_HEADS):
            in_head = (vcol >= h * GLA_DV) & (vcol < (h + 1) * GLA_DV)
            ms = jnp.sum(jnp.where(in_head, o2, 0.0), axis=1, keepdims=True) * (1.0 / GLA_DV)
            inv = jnp.where(in_head, lax.rsqrt(ms + EPS), inv)
        og = og_ref[p].astype(F32)
        o_ref[p] = (o * inv * norm_ref[...] * (og * jax.nn.sigmoid(og))).astype(o_ref.dtype)


def _gla(proj, gk, norm, batch, seq):
    t = proj.shape[0]
    ts = GLA_TS
    ns = seq // ts
    pair = GLA_PAIR
    assert batch % pair == 0
    r = np.arange(ts)
    tril = ((r[:, None] // GLA_CHUNK == r[None, :] // GLA_CHUNK) & (r[:, None] >= r[None, :]))
    tril = jnp.asarray(tril, BF16)
    bd = (np.arange(GLA_V_W)[:, None] // GLA_DV) == (np.arange(GLA_QK_W)[None, :] // GLA_DK)
    bd = jnp.asarray(bd, F32)
    proj3 = proj.reshape(batch, seq, PROJ_W)
    out = pl.pallas_call(
        _gla_body,
        out_shape=jax.ShapeDtypeStruct((batch, seq, GLA_V_W), BF16),
        grid=(batch // pair, ns),
        in_specs=[
            pl.BlockSpec((pair, ts, 2 * GLA_QK_W), lambda b, s: (b, s, _BLK_GQK)),
            pl.BlockSpec((pair, ts, GLA_V_W), lambda b, s: (b, s, _BLK_GV)),
            pl.BlockSpec((pair, ts, GLA_V_W), lambda b, s: (b, s, _BLK_GO)),
            pl.BlockSpec((pair, ts, GLA_QK_W), lambda b, s: (b, s, 0)),
            pl.BlockSpec((1, GLA_V_W), lambda b, s: (0, 0)),
            pl.BlockSpec((ts, ts), lambda b, s: (0, 0)),
            pl.BlockSpec((GLA_V_W, GLA_QK_W), lambda b, s: (0, 0)),
        ],
        out_specs=pl.BlockSpec((pair, ts, GLA_V_W), lambda b, s: (b, s, 0)),
        scratch_shapes=[pltpu.VMEM((pair, GLA_V_W, GLA_QK_W), F32),
                        pltpu.VMEM((pair, ts, GLA_V_W), F32)],
        compiler_params=_cparams("parallel", "arbitrary"),
        name="gla",
    )(proj3, proj3, proj3, gk.reshape(batch, seq, GLA_QK_W), norm, tril, bd)
    return out.reshape(t, GLA_V_W)


def _merge_body(h_ref, ya_ref, yp_ref, yg_ref, wga_ref, wgp_ref, wgg_ref, tga_ref, tgp_ref, tgg_ref,
                bga_ref, bgp_ref, bgg_ref, wa_ref, wp_ref, wg_ref, o_ref, wgate_ref):
    tn = o_ref.shape[1]

    @pl.when(pl.program_id(1) == 0)
    def _():
        for br, (win_ref, tail_ref) in enumerate(((wga_ref, tga_ref), (wgp_ref, tgp_ref),
                                                  (wgg_ref, tgg_ref))):
            win = jnp.concatenate([win_ref[...], tail_ref[...]], axis=1).astype(F32)
            wgate_ref[br] = win[:, GATE_LEAD:GATE_LEAD + tn].astype(BF16)

    h = h_ref[...]
    acc = None
    for br, (y_ref, bgate_ref, wbr_ref) in enumerate(((ya_ref, bga_ref, wa_ref),
                                                     (yp_ref, bgp_ref, wp_ref),
                                                     (yg_ref, bgg_ref, wg_ref))):
        gate = jax.nn.sigmoid(jnp.dot(h, wgate_ref[br], preferred_element_type=F32) + bgate_ref[...])
        term = gate * jnp.dot(y_ref[...], wbr_ref[...], preferred_element_type=F32)
        acc = term if acc is None else acc + term
    o_ref[...] = acc.astype(o_ref.dtype)


def _merge(h, y_att, y_pool, y_gla, w_in, b_gate, w_br_att, w_br_pool, w_br_gla, layer):
    t = h.shape[0]
    tm, tn = MERGE_TM, MERGE_TN
    nn = D_MODEL // tn
    row = lambda j, i: (i, 0)
    y_spec = pl.BlockSpec((tm, ATT_Q_W), row)
    gate_base = _IN_GATE - GATE_LEAD
    gate_specs = [pl.BlockSpec((pl.Squeezed(), pl.Element(D_MODEL), pl.Element(tn)),
                               functools.partial(
                                   lambda j, i, br: (layer, 0, pl.multiple_of(
                                       gate_base + br * D_MODEL + j * tn, LANES)), br=br))
                  for br in range(N_BRANCH)]
    tail_specs = [pl.BlockSpec((None, D_MODEL, LANES),
                               functools.partial(
                                   lambda j, i, br: (layer, 0, (gate_base + br * D_MODEL) // LANES
                                                     + (j + 1) * (tn // LANES)), br=br))
                  for br in range(N_BRANCH)]
    bias_specs = [pl.BlockSpec((1, tn), functools.partial(lambda j, i, br: (0, br * nn + j), br=br))
                  for br in range(N_BRANCH)]
    br_spec = pl.BlockSpec((None, ATT_Q_W, tn), lambda j, i: (layer, 0, j))
    return pl.pallas_call(
        _merge_body,
        out_shape=jax.ShapeDtypeStruct((t, D_MODEL), BF16),
        grid=(nn, t // tm),
        in_specs=[pl.BlockSpec((tm, D_MODEL), row), y_spec, y_spec, y_spec,
                  *gate_specs, *tail_specs, *bias_specs, br_spec, br_spec, br_spec],
        out_specs=pl.BlockSpec((tm, tn), lambda j, i: (i, j)),
        scratch_shapes=[pltpu.VMEM((N_BRANCH, D_MODEL, tn), BF16)],
        compiler_params=_cparams("arbitrary", "arbitrary"),
        name="gated_merge",
    )(h, y_att, y_pool, y_gla, w_in, w_in, w_in, w_in, w_in, w_in, b_gate, b_gate, b_gate,
      w_br_att, w_br_pool, w_br_gla)


def _outproj_body(m_ref, w_ref, x_ref, o_ref):
    o_ref[...] = x_ref[...] + jnp.dot(m_ref[...], w_ref[...], preferred_element_type=F32)


def _outproj(merged, w_out, x, layer):
    t = x.shape[0]
    tm, tn = OUT_TM, OUT_TN
    return pl.pallas_call(
        _outproj_body,
        out_shape=jax.ShapeDtypeStruct((t, D_MODEL), F32),
        grid=(t // tm, D_MODEL // tn),
        in_specs=[
            pl.BlockSpec((tm, D_MODEL), lambda i, j: (i, 0)),
            pl.BlockSpec((None, D_MODEL, tn), lambda i, j: (layer, 0, j)),
            pl.BlockSpec((tm, tn), lambda i, j: (i, j)),
        ],
        out_specs=pl.BlockSpec((tm, tn), lambda i, j: (i, j)),
        compiler_params=_cparams("parallel", "arbitrary"),
        name="outproj",
    )(merged, w_out, x)


def _prep_mixer_weights(w_in, w_pool, w_gla_a2, w_br_att, w_br_pool, w_br_gla, w_out):
    depth = w_in.shape[0]
    w_in_bf = w_in.astype(BF16)
    w_a2 = jnp.pad(w_gla_a2.astype(BF16), ((0, 0), (0, LANES - GLA_LOWRANK), (0, 0)))
    groups = len(POOL_WINDOWS)
    eye = jnp.eye(groups, dtype=BF16)
    w_bd = (w_pool.astype(BF16)[:, :, :, None, :] * eye[None, :, None, :, None]).reshape(
        depth, POOL_WIDTH, POOL_WIDTH)
    return dict(w_in=w_in_bf, w_a2=w_a2, w_bd=w_bd,
                w_br_att=w_br_att.astype(BF16), w_br_pool=w_br_pool.astype(BF16),
                w_br_gla=w_br_gla.astype(BF16), w_out=w_out.astype(BF16))


def _mixer(x, cos_t, sin_t, batch, seq, layer, mw, norm_mix, b_gate, att_sinks, pool_scale,
           b_gla_a, gla_norm):
    proj, gk, h, y_pool, y_att = _inproj(x, norm_mix[None, :], mw["w_in"], mw["w_a2"],
                                         b_gla_a[None, :], cos_t, sin_t, mw["w_bd"],
                                         pool_scale[None, :], att_sinks, seq, layer)
    y_gla = _gla(proj, gk, gla_norm[None, :], batch, seq)
    merged = _merge(h, y_att, y_pool, y_gla, mw["w_in"], b_gate[None, :],
                    mw["w_br_att"], mw["w_br_pool"], mw["w_br_gla"], layer)
    return _outproj(merged, mw["w_out"], x, layer)


def kernel(x, positions, norm_ffn1, ffn1_wi, ffn1_wo, norm_mix, w_in, b_gate, att_sinks, w_pool, pool_scale, w_gla_a2, b_gla_a, gla_norm, w_br_att, w_br_pool, w_br_gla, w_out, norm_ffn2, ffn2_wi, ffn2_wo, norm_final):
    batch, seq, d = x.shape
    depth = norm_ffn1.shape[0]
    assert d == D_MODEL and seq % max(INPROJ_TM, GLA_TS, ATT_BLOCK) == 0
    t = batch * seq
    assert t % max(FFN_TM, MERGE_TM, OUT_TM, ROPE_TM, INPROJ_TM) == 0
    xt = x.reshape(t, d)
    cos_t, sin_t = _rope_tables(positions)
    gf = norm_final[None, :]
    mw = _prep_mixer_weights(w_in, w_pool, w_gla_a2, w_br_att, w_br_pool, w_br_gla, w_out)
    for l in range(depth):
        xt = _ffn(xt, norm_ffn1[l][None, :], ffn1_wi, ffn1_wo, gf, l, False)
        xt = _mixer(xt, cos_t, sin_t, batch, seq, l, mw, norm_mix[l], b_gate[l], att_sinks[l],
                    pool_scale[l], b_gla_a[l], gla_norm[l])
        xt = _ffn(xt, norm_ffn2[l][None, :], ffn2_wi, ffn2_wo, gf, l, l == depth - 1)
    return xt.reshape(batch, seq, d)
```

```python
import functools

import numpy as np
import jax
import jax.numpy as jnp
from jax import lax
from jax.experimental import pallas as pl
from jax.experimental.pallas import tpu as pltpu

F32 = jnp.float32
BF16 = jnp.bfloat16

D_MODEL = 2048
D_FF = 5632
ATT_HEADS = 12
ATT_KV_HEADS = 4
HEAD_DIM = 64
ATT_BLOCK = 128
ROPE_THETA = 10000.0
ATT_Q_W = ATT_HEADS * HEAD_DIM
ATT_KV_W = ATT_KV_HEADS * HEAD_DIM
POOL_WINDOWS = (2, 4, 8, 16)
POOL_GROUP_DIM = 192
POOL_WIDTH = 768
GLA_HEADS = 4
GLA_DK = 96
GLA_DV = 192
GLA_QK_W = GLA_HEADS * GLA_DK
GLA_V_W = GLA_HEADS * GLA_DV
GLA_LOWRANK = 16
GLA_TAU = 16.0
GLA_CHUNK = 64
N_BRANCH = 3
EPS = 1e-6

_IN_QA, _IN_KA, _IN_VA, _IN_PU = 0, 768, 1024, 1280
_IN_QG, _IN_KG, _IN_VG, _IN_OG = 2048, 2432, 2816, 3584
_IN_LR = 4352
_IN_GATE = 4368

PROJ_SRC_W = 4352
PROJ_W = 2304
_BLK_GQK, _BLK_GV, _BLK_GO = 0, 1, 2
_PROJ_SEGMENTS = ((0, _IN_QG, 768), (768, _IN_VG, 1536))
W_IN_COLS = _IN_GATE + N_BRANCH * D_MODEL
GATE_LEAD = _IN_GATE % 128

LANES = 128
POOL_HALO = 16
VMEM_LIMIT = 58 * 1024 * 1024

FFN_TM, FFN_TF = 1024, 256
FFN_UNROLL = 4
INPROJ_TM = 512
GLA_TS = 256
GLA_PAIR = 2
MERGE_TM, MERGE_TN = 1024, 512
OUT_TM, OUT_TN = 512, 2048
ROPE_TM = 1024


def _cparams(*sem):
    return pltpu.CompilerParams(dimension_semantics=sem, vmem_limit_bytes=VMEM_LIMIT)


def _rms(x, g):
    return x * lax.rsqrt(jnp.mean(x * x, axis=-1, keepdims=True) + EPS) * g


def _ffn_body(x_ref, g_ref, gf_ref, wi_hbm, wo_hbm, o_ref, h_ref, u_ref, wa_buf, wb_buf, wo_buf, sem,
              *, layer, n_tiles, final_norm):
    tf = FFN_TF
    nch = D_FF // tf
    i = pl.program_id(0)

    def up_copies(c, slot):
        col = pl.multiple_of(c * tf, tf)
        return (pltpu.make_async_copy(wi_hbm.at[layer, :, pl.ds(col, tf)], wa_buf.at[slot], sem.at[0, slot]),
                pltpu.make_async_copy(wi_hbm.at[layer, :, pl.ds(D_FF + col, tf)], wb_buf.at[slot],
                                      sem.at[1, slot]))

    def down_copy(c, slot):
        row = pl.multiple_of(c * tf, tf)
        return pltpu.make_async_copy(wo_hbm.at[layer, pl.ds(row, tf), :], wo_buf.at[slot], sem.at[2, slot])

    def up(slot):
        h = h_ref[...]
        a = jnp.dot(h, wa_buf[slot].astype(BF16), preferred_element_type=F32)
        b = jnp.dot(h, wb_buf[slot].astype(BF16), preferred_element_type=F32)
        u_ref[slot] = (a * jax.nn.sigmoid(a) * (0.5 * b)).astype(BF16)

    def down(slot):
        o_ref[...] += jnp.dot(u_ref[slot], wo_buf[slot].astype(BF16), preferred_element_type=F32)

    def start_up(c, slot):
        ca, cb = up_copies(c, slot)
        ca.start(priority=1)
        cb.start(priority=1)

    def stage(c, slot):
        for cp in up_copies(c, slot):
            cp.wait()
        down_copy(c - 1, 1 - slot).wait()
        down_copy(c, slot).start(priority=1)
        down(1 - slot)
        up(slot)

    @pl.when(i == 0)
    def _():
        start_up(0, 0)

    x = x_ref[...]
    h_ref[...] = _rms(x, g_ref[...]).astype(BF16)
    o_ref[...] = x
    for cp in up_copies(0, 0):
        cp.wait()
    start_up(1, 1)
    down_copy(0, 0).start(priority=1)
    up(0)

    def stages(k, carry):
        c0 = FFN_UNROLL * k + 1
        for s in range(FFN_UNROLL):
            slot = (1 + s) % 2
            start_up(c0 + s + 1, 1 - slot)
            stage(c0 + s, slot)
        return carry

    lax.fori_loop(0, (nch - 2) // FFN_UNROLL, stages, 0)

    stage(nch - 1, 1)

    @pl.when(i + 1 < n_tiles)
    def _():
        start_up(0, 0)

    down_copy(nch - 1, 1).wait()
    down(1)
    if final_norm:
        o_ref[...] = _rms(o_ref[...], gf_ref[...])


def _ffn(x, g, wi, wo, gf, layer, final_norm):
    t = x.shape[0]
    tm, tf = FFN_TM, FFN_TF
    assert FFN_UNROLL % 2 == 0 and (D_FF // tf - 2) % FFN_UNROLL == 0
    n_tiles = t // tm
    return pl.pallas_call(
        functools.partial(_ffn_body, layer=layer, n_tiles=n_tiles, final_norm=final_norm),
        out_shape=jax.ShapeDtypeStruct((t, D_MODEL), F32),
        grid=(n_tiles,),
        in_specs=[
            pl.BlockSpec((tm, D_MODEL), lambda i: (i, 0)),
            pl.BlockSpec((1, D_MODEL), lambda i: (0, 0)),
            pl.BlockSpec((1, D_MODEL), lambda i: (0, 0)),
            pl.BlockSpec(memory_space=pl.ANY),
            pl.BlockSpec(memory_space=pl.ANY),
        ],
        out_specs=pl.BlockSpec((tm, D_MODEL), lambda i: (i, 0)),
        scratch_shapes=[
            pltpu.VMEM((tm, D_MODEL), BF16),
            pltpu.VMEM((2, tm, tf), BF16),
            pltpu.VMEM((2, D_MODEL, tf), F32),
            pltpu.VMEM((2, D_MODEL, tf), F32),
            pltpu.VMEM((2, tf, D_MODEL), F32),
            pltpu.SemaphoreType.DMA((3, 2)),
        ],
        compiler_params=_cparams("arbitrary"),
        name="ffn_final" if final_norm else "ffn",
    )(x, g, gf, wi, wo)


def _rope_table_body(pos_ref, freq_ref, sign_ref, cos_ref, sin_ref):
    ang = pos_ref[...].astype(F32) * freq_ref[...]
    cos_ref[...] = jnp.cos(ang)
    sin_ref[...] = jnp.sin(ang) * sign_ref[...]


def _rope_tables(positions):
    t = positions.size
    half = HEAD_DIM // 2
    inv_freq = ROPE_THETA ** (-jnp.arange(half, dtype=F32) / half)
    freq = jnp.tile(inv_freq, LANES // half)[None, :]
    sign = jnp.tile(jnp.concatenate([-jnp.ones((half,), F32), jnp.ones((half,), F32)]),
                    LANES // HEAD_DIM)[None, :]
    tm = ROPE_TM
    return pl.pallas_call(
        _rope_table_body,
        out_shape=(jax.ShapeDtypeStruct((t, LANES), F32), jax.ShapeDtypeStruct((t, LANES), F32)),
        grid=(t // tm,),
        in_specs=[
            pl.BlockSpec((tm, 1), lambda i: (i, 0)),
            pl.BlockSpec((1, LANES), lambda i: (0, 0)),
            pl.BlockSpec((1, LANES), lambda i: (0, 0)),
        ],
        out_specs=(pl.BlockSpec((tm, LANES), lambda i: (i, 0)),
                   pl.BlockSpec((tm, LANES), lambda i: (i, 0))),
        compiler_params=_cparams("parallel"),
        name="rope_tables",
    )(positions.reshape(t, 1), freq, sign)


def _swap_halves(x):
    lane = lax.broadcasted_iota(jnp.int32, x.shape, 1)
    first = (lane % HEAD_DIM) < (HEAD_DIM // 2)
    return jnp.where(first, pltpu.roll(x, LANES - HEAD_DIM // 2, 1), pltpu.roll(x, HEAD_DIM // 2, 1))


def _rope(x, cos, sin):
    cols = []
    for c in range(x.shape[1] // LANES):
        xc = x[:, c * LANES:(c + 1) * LANES]
        cols.append(xc * cos + _swap_halves(xc) * sin)
    return jnp.concatenate(cols, axis=1)


def _pool_mix(u, halo, tok0):
    ts = u.shape[0]
    a1 = jnp.concatenate([halo, u], axis=0)
    length = ts + POOL_HALO
    a2 = a1[1:] + a1[:length - 1]
    a4 = a2[2:] + a2[:length - 3]
    a8 = a4[4:] + a4[:length - 7]
    a16 = a8[8:] + a8[:length - 15]
    sums = {2: a2[POOL_HALO - 1:], 4: a4[POOL_HALO - 3:], 8: a8[POOL_HALO - 7:], 16: a16[POOL_HALO - 15:]}

    tok = tok0 + lax.broadcasted_iota(jnp.int32, (ts, 1), 0)
    chan = lax.broadcasted_iota(jnp.int32, (1, POOL_WIDTH), 1)
    d = jnp.zeros((ts, POOL_WIDTH), F32)
    for gi, w in enumerate(POOL_WINDOWS):
        inv_cnt = 1.0 / jnp.minimum(tok + 1, w).astype(F32)
        in_group = (chan >= gi * POOL_GROUP_DIM) & (chan < (gi + 1) * POOL_GROUP_DIM)
        d = jnp.where(in_group, sums[w] * inv_cnt, d)
    return d - u


def _inproj_body(sink_ref, x_ref, g_ref, w_ref, wlr_ref, wa2_ref, ba_ref, cos_ref, sin_ref, wbd_ref,
                 ps_ref, proj_ref, gk_ref, h_ref, ypool_ref, yatt_ref, halo_ref, kprev_ref, vprev_ref,
                 *, tiles_per_seq):
    tm = x_ref.shape[0]
    blk = ATT_BLOCK
    h = _rms(x_ref[...], g_ref[...]).astype(BF16)
    h_ref[...] = h
    s_idx = pl.program_id(0) % tiles_per_seq

    @pl.when(s_idx == 0)
    def _():
        halo_ref[...] = jnp.zeros_like(halo_ref)
        kprev_ref[...] = jnp.zeros_like(kprev_ref)
        vprev_ref[...] = jnp.zeros_like(vprev_ref)

    pu = jnp.dot(h, w_ref[:, _IN_PU:_IN_PU + POOL_WIDTH], preferred_element_type=F32)
    d = _pool_mix(pu, halo_ref[...], s_idx * tm).astype(BF16)
    halo_ref[...] = pu[tm - POOL_HALO:]

    lr = jnp.dot(h, wlr_ref[...], preferred_element_type=F32).astype(BF16)

    cos, sin = cos_ref[...], sin_ref[...]
    scale = HEAD_DIM ** -0.5
    q = _rope(jnp.dot(h, w_ref[:, _IN_QA:_IN_QA + ATT_Q_W], preferred_element_type=F32),
              cos * scale, sin * scale).astype(BF16)
    k = _rope(jnp.dot(h, w_ref[:, _IN_KA:_IN_KA + ATT_KV_W], preferred_element_type=F32),
              cos, sin).astype(BF16)
    v = jnp.dot(h, w_ref[:, _IN_VA:_IN_VA + ATT_KV_W], preferred_element_type=F32).astype(BF16)
    k = jnp.concatenate([kprev_ref[...], k], axis=0)
    v = jnp.concatenate([vprev_ref[...], v], axis=0)
    kprev_ref[...] = k[tm:]
    vprev_ref[...] = v[tm:]

    for dst, src, width in _PROJ_SEGMENTS:
        p = jnp.dot(h, w_ref[:, src:src + width], preferred_element_type=F32)
        proj_ref[:, dst:dst + width] = p.astype(proj_ref.dtype)

    for r in range(tm // blk):
        has_prev = (s_idx > 0) if r == 0 else True
        o = _attend_block(q[r * blk:(r + 1) * blk], k[r * blk:(r + 2) * blk], v[r * blk:(r + 2) * blk],
                          has_prev, sink_ref)
        yatt_ref[r * blk:(r + 1) * blk, :] = o.astype(yatt_ref.dtype)

    z = jnp.dot(lr, wa2_ref[...], preferred_element_type=F32) + ba_ref[...]
    gk_ref[...] = (jnp.minimum(z, 0.0) - jnp.log1p(jnp.exp(-jnp.abs(z)))) * (1.0 / GLA_TAU)
    y = jnp.dot(d, wbd_ref[...], preferred_element_type=F32)
    ypool_ref[...] = (y * ps_ref[...]).astype(ypool_ref.dtype)


def _inproj(x, g, w_in, w_a2, b_a, cos_t, sin_t, w_bd, pool_scale, sinks, seq, layer):
    t = x.shape[0]
    tm = INPROJ_TM
    assert seq % tm == 0 and tm % ATT_BLOCK == 0
    const = lambda i: (0, 0)
    lconst = lambda i: (layer, 0, 0)
    rows = lambda i: (i, 0)
    return pl.pallas_call(
        functools.partial(_inproj_body, tiles_per_seq=seq // tm),
        out_shape=(jax.ShapeDtypeStruct((t, PROJ_W), BF16),
                   jax.ShapeDtypeStruct((t, GLA_QK_W), F32),
                   jax.ShapeDtypeStruct((t, D_MODEL), BF16),
                   jax.ShapeDtypeStruct((t, POOL_WIDTH), BF16),
                   jax.ShapeDtypeStruct((t, ATT_Q_W), BF16)),
        grid=(t // tm,),
        in_specs=[
            pl.BlockSpec(memory_space=pltpu.SMEM),
            pl.BlockSpec((tm, D_MODEL), rows),
            pl.BlockSpec((1, D_MODEL), const),
            pl.BlockSpec((D_MODEL, PROJ_SRC_W), const, pipeline_mode=pl.Buffered(1)),
            pl.BlockSpec((D_MODEL, LANES), lambda i: (0, _IN_LR // LANES)),
            pl.BlockSpec((None, LANES, GLA_QK_W), lconst),
            pl.BlockSpec((1, GLA_QK_W), const),
            pl.BlockSpec((tm, LANES), rows),
            pl.BlockSpec((tm, LANES), rows),
            pl.BlockSpec((None, POOL_WIDTH, POOL_WIDTH), lconst),
            pl.BlockSpec((1, POOL_WIDTH), const),
        ],
        out_specs=(pl.BlockSpec((tm, PROJ_W), rows),
                   pl.BlockSpec((tm, GLA_QK_W), rows),
                   pl.BlockSpec((tm, D_MODEL), rows),
                   pl.BlockSpec((tm, POOL_WIDTH), rows),
                   pl.BlockSpec((tm, ATT_Q_W), rows)),
        scratch_shapes=[pltpu.VMEM((POOL_HALO, POOL_WIDTH), F32),
                        pltpu.VMEM((ATT_BLOCK, ATT_KV_W), BF16),
                        pltpu.VMEM((ATT_BLOCK, ATT_KV_W), BF16)],
        compiler_params=_cparams("arbitrary"),
        name="inproj",
    )(sinks, x, g, w_in, w_in, w_a2, b_a, cos_t, sin_t, w_bd, pool_scale)


def _both_halves(x, col, half):
    xc = x[:, col * LANES:(col + 1) * LANES].astype(F32)
    sw = pltpu.roll(xc, LANES // 2, 1)
    first = lax.broadcasted_iota(jnp.int32, xc.shape, 1) < LANES // 2
    dup = jnp.where(first, xc, sw) if half == 0 else jnp.where(first, sw, xc)
    return dup.astype(BF16)


def _attend_block(q, k, v, has_prev, sink_ref):
    blk = ATT_BLOCK
    group = ATT_HEADS // ATT_KV_HEADS

    qi = lax.broadcasted_iota(jnp.int32, (blk, 2 * blk), 0)
    si = lax.broadcasted_iota(jnp.int32, (blk, 2 * blk), 1)
    rel = blk + qi - si
    mask = (rel >= 0) & (rel < blk) & ((si >= blk) | has_prev)
    bias = jnp.where(mask, 0.0, -1e30)
    first = lax.broadcasted_iota(jnp.int32, (blk, LANES), 1) < LANES // 2
    zero = jnp.zeros((blk, LANES), BF16)

    heads = [None] * ATT_HEADS
    for g in range(ATT_KV_HEADS):
        kk = _both_halves(k, g // 2, g % 2)
        vv = _both_halves(v, g // 2, g % 2)
        hs = range(g * group, (g + 1) * group)
        qm = jnp.concatenate(
            [jnp.where(first if hd % 2 == 0 else ~first, q[:, (hd // 2) * LANES:(hd // 2 + 1) * LANES], zero)
             for hd in hs], axis=0)
        s3 = lax.dot_general(qm, kk, (((1,), (1,)), ((), ())), preferred_element_type=F32)
        es, denoms = [], []
        for r, hd in enumerate(hs):
            s = s3[r * blk:(r + 1) * blk] + bias
            sink = sink_ref[hd]
            m = jnp.maximum(jnp.max(s, axis=1, keepdims=True), sink)
            e = jnp.exp(s - m)
            denoms.append(jnp.sum(e, axis=1, keepdims=True) + jnp.exp(sink - m))
            es.append(e.astype(BF16))
        o3 = jnp.dot(jnp.concatenate(es, axis=0), vv, preferred_element_type=F32)
        for r, hd in enumerate(hs):
            heads[hd] = o3[r * blk:(r + 1) * blk] / denoms[r]
    cols = [jnp.where(first, heads[2 * c], heads[2 * c + 1]) for c in range(ATT_HEADS // 2)]
    return jnp.concatenate(cols, axis=1)


def _gla_body(qk_ref, v_ref, og_ref, gk_ref, norm_ref, tril_ref, bd_ref, o_ref, st_ref, oacc_ref):
    s_idx = pl.program_id(1)
    npair, ts = qk_ref.shape[0], qk_ref.shape[1]
    c = GLA_CHUNK

    @pl.when(s_idx == 0)
    def _():
        st_ref[...] = jnp.zeros_like(st_ref)

    tril = tril_ref[...]
    prep = []
    for p in range(npair):
        qk = qk_ref[p].astype(F32)
        q = qk[:, :GLA_QK_W] * (GLA_DK ** -0.5)
        k = qk[:, GLA_QK_W:]
        gk = gk_ref[p]
        g1 = gk.astype(BF16)
        r1 = gk - g1.astype(F32)
        g2 = r1.astype(BF16)
        g3 = (r1 - g2.astype(F32)).astype(BF16)
        b = (jnp.dot(tril, g1, preferred_element_type=F32) + jnp.dot(tril, g2, preferred_element_type=F32)
             + jnp.dot(tril, g3, preferred_element_type=F32))
        prep.append(((q * jnp.exp(b)).astype(BF16), (k * jnp.exp(-b)).astype(BF16), k, b, v_ref[p]))

    kcol = lax.broadcasted_iota(jnp.int32, (1, GLA_QK_W), 1)
    vcol = lax.broadcasted_iota(jnp.int32, (1, GLA_V_W), 1)
    ri = lax.broadcasted_iota(jnp.int32, (c, c), 0)
    ci = lax.broadcasted_iota(jnp.int32, (c, c), 1)
    causal = ri >= ci
    bd = bd_ref[...]

    units = [(n, p) for n in range(ts // c) for p in range(npair)]
    scores = {}
    for n, p in units:
        rows = slice(n * c, (n + 1) * c)
        q_n = prep[p][0][rows]
        q4 = jnp.concatenate(
            [jnp.where((kcol >= h * GLA_DK) & (kcol < (h + 1) * GLA_DK), q_n, jnp.zeros_like(q_n))
             for h in range(GLA_HEADS)], axis=0)
        scores[n, p] = lax.dot_general(q4, prep[p][1][rows], (((1,), (1,)), ((), ())),
                                       preferred_element_type=F32)
    for n, p in units:
        rows = slice(n * c, (n + 1) * c)
        v_n = prep[p][4][rows]
        o_n = jnp.zeros((c, GLA_V_W), F32)
        for h in range(GLA_HEADS):
            a_h = jnp.where(causal, scores[n, p][h * c:(h + 1) * c], 0.0).astype(BF16)
            o_h = jnp.dot(a_h, v_n, preferred_element_type=F32)
            o_n = jnp.where((vcol >= h * GLA_DV) & (vcol < (h + 1) * GLA_DV), o_h, o_n)
        oacc_ref[p, rows, :] = o_n
    for n, p in units:
        rows = slice(n * c, (n + 1) * c)
        q_t, _, k, b, v = prep[p]
        b_n = b[rows]
        b_last = b_n[c - 1:c, :]
        k_s = (k[rows] * jnp.exp(b_last - b_n)).astype(BF16)
        st = st_ref[p]
        oacc_ref[p, rows, :] += lax.dot_general(q_t[rows], st.astype(BF16), (((1,), (1,)), ((), ())),
                                                preferred_element_type=F32)
        kv = lax.dot_general(v[rows], k_s, (((0,), (0,)), ((), ())), preferred_element_type=F32)
        st_ref[p] = st * jnp.exp(b_last) + kv * bd

    for p in range(npair):
        o = oacc_ref[p]
        o2 = o * o
        inv = jnp.zeros_like(o)
        for h in range(GLA_HEADS):
            in_head = (vcol >= h * GLA_DV) & (vcol < (h + 1) * GLA_DV)
            ms = jnp.sum(jnp.where(in_head, o2, 0.0), axis=1, keepdims=True) * (1.0 / GLA_DV)
            inv = jnp.where(in_head, lax.rsqrt(ms + EPS), inv)
        og = og_ref[p].astype(F32)
        o_ref[p] = (o * inv * norm_ref[...] * (og * jax.nn.sigmoid(og))).astype(o_ref.dtype)


def _gla(proj, gk, norm, batch, seq):
    t = proj.shape[0]
    ts = GLA_TS
    ns = seq // ts
    pair = GLA_PAIR
    assert batch % pair == 0
    r = np.arange(ts)
    tril = ((r[:, None] // GLA_CHUNK == r[None, :] // GLA_CHUNK) & (r[:, None] >= r[None, :]))
    tril = jnp.asarray(tril, BF16)
    bd = (np.arange(GLA_V_W)[:, None] // GLA_DV) == (np.arange(GLA_QK_W)[None, :] // GLA_DK)
    bd = jnp.asarray(bd, F32)
    proj3 = proj.reshape(batch, seq, PROJ_W)
    out = pl.pallas_call(
        _gla_body,
        out_shape=jax.ShapeDtypeStruct((batch, seq, GLA_V_W), BF16),
        grid=(batch // pair, ns),
        in_specs=[
            pl.BlockSpec((pair, ts, 2 * GLA_QK_W), lambda b, s: (b, s, _BLK_GQK)),
            pl.BlockSpec((pair, ts, GLA_V_W), lambda b, s: (b, s, _BLK_GV)),
            pl.BlockSpec((pair, ts, GLA_V_W), lambda b, s: (b, s, _BLK_GO)),
            pl.BlockSpec((pair, ts, GLA_QK_W), lambda b, s: (b, s, 0)),
            pl.BlockSpec((1, GLA_V_W), lambda b, s: (0, 0)),
            pl.BlockSpec((ts, ts), lambda b, s: (0, 0)),
            pl.BlockSpec((GLA_V_W, GLA_QK_W), lambda b, s: (0, 0)),
        ],
        out_specs=pl.BlockSpec((pair, ts, GLA_V_W), lambda b, s: (b, s, 0)),
        scratch_shapes=[pltpu.VMEM((pair, GLA_V_W, GLA_QK_W), F32),
                        pltpu.VMEM((pair, ts, GLA_V_W), F32)],
        compiler_params=_cparams("parallel", "arbitrary"),
        name="gla",
    )(proj3, proj3, proj3, gk.reshape(batch, seq, GLA_QK_W), norm, tril, bd)
    return out.reshape(t, GLA_V_W)


def _merge_body(h_ref, ya_ref, yp_ref, yg_ref, wga_ref, wgp_ref, wgg_ref, tga_ref, tgp_ref, tgg_ref,
                bga_ref, bgp_ref, bgg_ref, wa_ref, wp_ref, wg_ref, *rest, cast_next):
    if cast_next:
        wnext_ref, o_ref, wnext_bf_ref, wgate_ref = rest
        wnext_bf_ref[...] = wnext_ref[...].astype(BF16)
    else:
        o_ref, wgate_ref = rest
    tn = o_ref.shape[1]

    @pl.when(pl.program_id(1) == 0)
    def _():
        for br, (win_ref, tail_ref) in enumerate(((wga_ref, tga_ref), (wgp_ref, tgp_ref),
                                                  (wgg_ref, tgg_ref))):
            win = jnp.concatenate([win_ref[...], tail_ref[...]], axis=1).astype(F32)
            wgate_ref[br] = win[:, GATE_LEAD:GATE_LEAD + tn].astype(BF16)

    h = h_ref[...]
    acc = None
    for br, (y_ref, bgate_ref, wbr_ref) in enumerate(((ya_ref, bga_ref, wa_ref),
                                                     (yp_ref, bgp_ref, wp_ref),
                                                     (yg_ref, bgg_ref, wg_ref))):
        gate = jax.nn.sigmoid(jnp.dot(h, wgate_ref[br], preferred_element_type=F32) + bgate_ref[...])
        term = gate * jnp.dot(y_ref[...], wbr_ref[...], preferred_element_type=F32)
        acc = term if acc is None else acc + term
    o_ref[...] = acc.astype(o_ref.dtype)


def _merge(h, y_att, y_pool, y_gla, w_in, b_gate, w_br_att, w_br_pool, w_br_gla, w_in_f32, layer):
    t = h.shape[0]
    tm, tn = MERGE_TM, MERGE_TN
    nn = D_MODEL // tn
    ni = t // tm
    cast_next = layer + 1 < w_in_f32.shape[0]
    row = lambda j, i: (i, 0)
    y_spec = pl.BlockSpec((tm, ATT_Q_W), row)
    gate_base = _IN_GATE - GATE_LEAD
    gate_specs = [pl.BlockSpec((pl.Element(D_MODEL), pl.Element(tn)),
                               functools.partial(
                                   lambda j, i, br: (0, pl.multiple_of(
                                       gate_base + br * D_MODEL + j * tn, LANES)), br=br),
                               pipeline_mode=pl.Buffered(1))
                  for br in range(N_BRANCH)]
    tail_specs = [pl.BlockSpec((D_MODEL, LANES),
                               functools.partial(
                                   lambda j, i, br: (0, (gate_base + br * D_MODEL) // LANES
                                                     + (j + 1) * (tn // LANES)), br=br),
                               pipeline_mode=pl.Buffered(1))
                  for br in range(N_BRANCH)]
    bias_specs = [pl.BlockSpec((1, tn), functools.partial(lambda j, i, br: (0, br * nn + j), br=br))
                  for br in range(N_BRANCH)]
    br_spec = pl.BlockSpec((None, ATT_Q_W, tn), lambda j, i: (layer, 0, j))
    in_specs = [pl.BlockSpec((tm, D_MODEL), row), y_spec, y_spec, y_spec,
                *gate_specs, *tail_specs, *bias_specs, br_spec, br_spec, br_spec]
    args = [h, y_att, y_pool, y_gla, w_in, w_in, w_in, w_in, w_in, w_in, b_gate, b_gate, b_gate,
            w_br_att, w_br_pool, w_br_gla]
    out_shape = [jax.ShapeDtypeStruct((t, D_MODEL), BF16)]
    out_specs = [pl.BlockSpec((tm, tn), lambda j, i: (i, j))]
    if cast_next:
        cast_rows = D_MODEL // (nn * ni)
        assert cast_rows * nn * ni == D_MODEL and cast_rows % 16 == 0
        in_specs.append(pl.BlockSpec((None, cast_rows, W_IN_COLS), lambda j, i: (layer + 1, j * ni + i, 0)))
        args.append(w_in_f32)
        out_shape.append(jax.ShapeDtypeStruct((D_MODEL, W_IN_COLS), BF16))
        out_specs.append(pl.BlockSpec((cast_rows, W_IN_COLS), lambda j, i: (j * ni + i, 0)))
    outs = pl.pallas_call(
        functools.partial(_merge_body, cast_next=cast_next),
        out_shape=out_shape,
        grid=(nn, ni),
        in_specs=in_specs,
        out_specs=out_specs,
        scratch_shapes=[pltpu.VMEM((N_BRANCH, D_MODEL, tn), BF16)],
        compiler_params=_cparams("arbitrary", "arbitrary"),
        name="gated_merge",
    )(*args)
    return (outs[0], outs[1]) if cast_next else (outs[0], None)


def _outproj_body(m_ref, w_ref, x_ref, o_ref):
    o_ref[...] = x_ref[...] + jnp.dot(m_ref[...], w_ref[...], preferred_element_type=F32)


def _outproj(merged, w_out, x, layer):
    t = x.shape[0]
    tm, tn = OUT_TM, OUT_TN
    return pl.pallas_call(
        _outproj_body,
        out_shape=jax.ShapeDtypeStruct((t, D_MODEL), F32),
        grid=(t // tm, D_MODEL // tn),
        in_specs=[
            pl.BlockSpec((tm, D_MODEL), lambda i, j: (i, 0)),
            pl.BlockSpec((None, D_MODEL, tn), lambda i, j: (layer, 0, j)),
            pl.BlockSpec((tm, tn), lambda i, j: (i, j)),
        ],
        out_specs=pl.BlockSpec((tm, tn), lambda i, j: (i, j)),
        compiler_params=_cparams("parallel", "arbitrary"),
        name="outproj",
    )(merged, w_out, x)


def _prep_mixer_weights(w_pool, w_gla_a2, w_br_att, w_br_pool, w_br_gla, w_out):
    depth = w_pool.shape[0]
    w_a2 = jnp.pad(w_gla_a2.astype(BF16), ((0, 0), (0, LANES - GLA_LOWRANK), (0, 0)))
    groups = len(POOL_WINDOWS)
    eye = jnp.eye(groups, dtype=BF16)
    w_bd = (w_pool.astype(BF16)[:, :, :, None, :] * eye[None, :, None, :, None]).reshape(
        depth, POOL_WIDTH, POOL_WIDTH)
    return dict(w_a2=w_a2, w_bd=w_bd,
                w_br_att=w_br_att.astype(BF16), w_br_pool=w_br_pool.astype(BF16),
                w_br_gla=w_br_gla.astype(BF16), w_out=w_out.astype(BF16))


def _mixer(x, cos_t, sin_t, batch, seq, layer, mw, w_in_bf, w_in, norm_mix, b_gate, att_sinks,
           pool_scale, b_gla_a, gla_norm):
    proj, gk, h, y_pool, y_att = _inproj(x, norm_mix[None, :], w_in_bf, mw["w_a2"],
                                         b_gla_a[None, :], cos_t, sin_t, mw["w_bd"],
                                         pool_scale[None, :], att_sinks, seq, layer)
    y_gla = _gla(proj, gk, gla_norm[None, :], batch, seq)
    merged, w_in_bf_next = _merge(h, y_att, y_pool, y_gla, w_in_bf, b_gate[None, :],
                                  mw["w_br_att"], mw["w_br_pool"], mw["w_br_gla"], w_in, layer)
    return _outproj(merged, mw["w_out"], x, layer), w_in_bf_next


def kernel(x, positions, norm_ffn1, ffn1_wi, ffn1_wo, norm_mix, w_in, b_gate, att_sinks, w_pool, pool_scale, w_gla_a2, b_gla_a, gla_norm, w_br_att, w_br_pool, w_br_gla, w_out, norm_ffn2, ffn2_wi, ffn2_wo, norm_final):
    batch, seq, d = x.shape
    depth = norm_ffn1.shape[0]
    assert d == D_MODEL and seq % max(INPROJ_TM, GLA_TS, ATT_BLOCK) == 0
    t = batch * seq
    assert t % max(FFN_TM, MERGE_TM, OUT_TM, ROPE_TM, INPROJ_TM) == 0
    xt = x.reshape(t, d)
    cos_t, sin_t = _rope_tables(positions)
    gf = norm_final[None, :]
    mw = _prep_mixer_weights(w_pool, w_gla_a2, w_br_att, w_br_pool, w_br_gla, w_out)
    w_in_bf = w_in[0].astype(BF16)
    for l in range(depth):
        xt = _ffn(xt, norm_ffn1[l][None, :], ffn1_wi, ffn1_wo, gf, l, False)
        xt, w_in_bf = _mixer(xt, cos_t, sin_t, batch, seq, l, mw, w_in_bf, w_in, norm_mix[l],
                             b_gate[l], att_sinks[l], pool_scale[l], b_gla_a[l], gla_norm[l])
        xt = _ffn(xt, norm_ffn2[l][None, :], ffn2_wi, ffn2_wo, gf, l, l == depth - 1)
    return xt.reshape(batch, seq, d)
```

```python
import functools

import numpy as np
import jax
import jax.numpy as jnp
from jax import lax
from jax.experimental import pallas as pl
from jax.experimental.pallas import tpu as pltpu

F32 = jnp.float32
BF16 = jnp.bfloat16

D_MODEL = 2048
D_FF = 5632
ATT_HEADS = 12
ATT_KV_HEADS = 4
HEAD_DIM = 64
ATT_BLOCK = 128
ROPE_THETA = 10000.0
ATT_Q_W = ATT_HEADS * HEAD_DIM
ATT_KV_W = ATT_KV_HEADS * HEAD_DIM
POOL_WINDOWS = (2, 4, 8, 16)
POOL_GROUP_DIM = 192
POOL_WIDTH = 768
GLA_HEADS = 4
GLA_DK = 96
GLA_DV = 192
GLA_QK_W = GLA_HEADS * GLA_DK
GLA_V_W = GLA_HEADS * GLA_DV
GLA_LOWRANK = 16
GLA_TAU = 16.0
GLA_CHUNK = 64
N_BRANCH = 3
EPS = 1e-6

_IN_QA, _IN_KA, _IN_VA, _IN_PU = 0, 768, 1024, 1280
_IN_QG, _IN_KG, _IN_VG, _IN_OG = 2048, 2432, 2816, 3584
_IN_LR = 4352
_IN_GATE = 4368

PROJ_SRC_W = _IN_LR
PROJ_W = 2 * GLA_QK_W + 2 * GLA_V_W
_BLK_GQK, _BLK_GV, _BLK_GO = 0, 1, 2
_PROJ_SEGMENTS = ((0, _IN_QG, 2 * GLA_QK_W), (2 * GLA_QK_W, _IN_VG, 2 * GLA_V_W))
W_IN_COLS = _IN_GATE + N_BRANCH * D_MODEL
GATE_LEAD = _IN_GATE % 128

LANES = 128
POOL_HALO = 16
VMEM_LIMIT = 58 * 1024 * 1024

FFN_TM, FFN_TF = 1024, 256
FFN_UNROLL = 4
INPROJ_TM = 512
GLA_TS = 256
GLA_PAIR = 4
MERGE_TM, MERGE_TN = 1024, 512
OUT_TM, OUT_TN = 1024, 2048
ROPE_TM = 1024


def _cparams(*sem):
    return pltpu.CompilerParams(dimension_semantics=sem, vmem_limit_bytes=VMEM_LIMIT)


def _rms(x, g):
    return x * lax.rsqrt(jnp.mean(x * x, axis=-1, keepdims=True) + EPS) * g


def _ffn_body(x_ref, g_ref, gf_ref, wi_hbm, wo_hbm, o_ref, h_ref, u_ref, wa_buf, wb_buf, wo_buf, sem,
              *, layer, n_tiles, final_norm):
    tf = FFN_TF
    nch = D_FF // tf
    i = pl.program_id(0)

    def up_copies(c, slot):
        col = pl.multiple_of(c * tf, tf)
        return (pltpu.make_async_copy(wi_hbm.at[layer, :, pl.ds(col, tf)], wa_buf.at[slot], sem.at[0, slot]),
                pltpu.make_async_copy(wi_hbm.at[layer, :, pl.ds(D_FF + col, tf)], wb_buf.at[slot],
                                      sem.at[1, slot]))

    def down_copy(c, slot):
        row = pl.multiple_of(c * tf, tf)
        return pltpu.make_async_copy(wo_hbm.at[layer, pl.ds(row, tf), :], wo_buf.at[slot], sem.at[2, slot])

    def up(slot):
        h = h_ref[...]
        a = jnp.dot(h, wa_buf[slot].astype(BF16), preferred_element_type=F32)
        b = jnp.dot(h, wb_buf[slot].astype(BF16), preferred_element_type=F32)
        u_ref[slot] = (a * jax.nn.sigmoid(a) * (0.5 * b)).astype(BF16)

    def down(slot):
        o_ref[...] += jnp.dot(u_ref[slot], wo_buf[slot].astype(BF16), preferred_element_type=F32)

    def start_up(c, slot):
        ca, cb = up_copies(c, slot)
        ca.start(priority=1)
        cb.start(priority=1)

    def stage(c, slot):
        for cp in up_copies(c, slot):
            cp.wait()
        down_copy(c - 1, 1 - slot).wait()
        down_copy(c, slot).start(priority=1)
        down(1 - slot)
        up(slot)

    @pl.when(i == 0)
    def _():
        start_up(0, 0)

    x = x_ref[...]
    h_ref[...] = _rms(x, g_ref[...]).astype(BF16)
    o_ref[...] = x
    for cp in up_copies(0, 0):
        cp.wait()
    start_up(1, 1)
    down_copy(0, 0).start(priority=1)
    up(0)

    def stages(k, carry):
        c0 = FFN_UNROLL * k + 1
        for s in range(FFN_UNROLL):
            slot = (1 + s) % 2
            start_up(c0 + s + 1, 1 - slot)
            stage(c0 + s, slot)
        return carry

    lax.fori_loop(0, (nch - 2) // FFN_UNROLL, stages, 0)

    stage(nch - 1, 1)

    @pl.when(i + 1 < n_tiles)
    def _():
        start_up(0, 0)

    down_copy(nch - 1, 1).wait()
    down(1)
    if final_norm:
        o_ref[...] = _rms(o_ref[...], gf_ref[...])


def _ffn(x, g, wi, wo, gf, layer, final_norm):
    t = x.shape[0]
    tm, tf = FFN_TM, FFN_TF
    assert FFN_UNROLL % 2 == 0 and (D_FF // tf - 2) % FFN_UNROLL == 0
    n_tiles = t // tm
    return pl.pallas_call(
        functools.partial(_ffn_body, layer=layer, n_tiles=n_tiles, final_norm=final_norm),
        out_shape=jax.ShapeDtypeStruct((t, D_MODEL), F32),
        grid=(n_tiles,),
        in_specs=[
            pl.BlockSpec((tm, D_MODEL), lambda i: (i, 0)),
            pl.BlockSpec((1, D_MODEL), lambda i: (0, 0)),
            pl.BlockSpec((1, D_MODEL), lambda i: (0, 0)),
            pl.BlockSpec(memory_space=pl.ANY),
            pl.BlockSpec(memory_space=pl.ANY),
        ],
        out_specs=pl.BlockSpec((tm, D_MODEL), lambda i: (i, 0)),
        scratch_shapes=[
            pltpu.VMEM((tm, D_MODEL), BF16),
            pltpu.VMEM((2, tm, tf), BF16),
            pltpu.VMEM((2, D_MODEL, tf), F32),
            pltpu.VMEM((2, D_MODEL, tf), F32),
            pltpu.VMEM((2, tf, D_MODEL), F32),
            pltpu.SemaphoreType.DMA((3, 2)),
        ],
        compiler_params=_cparams("arbitrary"),
        name="ffn_final" if final_norm else "ffn",
    )(x, g, gf, wi, wo)


def _rope_table_body(pos_ref, freq_ref, sign_ref, cos_ref, sin_ref):
    ang = pos_ref[...].astype(F32) * freq_ref[...]
    cos_ref[...] = jnp.cos(ang)
    sin_ref[...] = jnp.sin(ang) * sign_ref[...]


def _rope_tables(positions):
    t = positions.size
    half = HEAD_DIM // 2
    inv_freq = ROPE_THETA ** (-jnp.arange(half, dtype=F32) / half)
    freq = jnp.tile(inv_freq, LANES // half)[None, :]
    sign = jnp.tile(jnp.concatenate([-jnp.ones((half,), F32), jnp.ones((half,), F32)]),
                    LANES // HEAD_DIM)[None, :]
    tm = ROPE_TM
    return pl.pallas_call(
        _rope_table_body,
        out_shape=(jax.ShapeDtypeStruct((t, LANES), F32), jax.ShapeDtypeStruct((t, LANES), F32)),
        grid=(t // tm,),
        in_specs=[
            pl.BlockSpec((tm, 1), lambda i: (i, 0)),
            pl.BlockSpec((1, LANES), lambda i: (0, 0)),
            pl.BlockSpec((1, LANES), lambda i: (0, 0)),
        ],
        out_specs=(pl.BlockSpec((tm, LANES), lambda i: (i, 0)),
                   pl.BlockSpec((tm, LANES), lambda i: (i, 0))),
        compiler_params=_cparams("parallel"),
        name="rope_tables",
    )(positions.reshape(t, 1), freq, sign)


def _swap_halves(x):
    lane = lax.broadcasted_iota(jnp.int32, x.shape, 1)
    first = (lane % HEAD_DIM) < (HEAD_DIM // 2)
    return jnp.where(first, pltpu.roll(x, LANES - HEAD_DIM // 2, 1), pltpu.roll(x, HEAD_DIM // 2, 1))


def _rope(x, cos, sin):
    cols = []
    for c in range(x.shape[1] // LANES):
        xc = x[:, c * LANES:(c + 1) * LANES]
        cols.append(xc * cos + _swap_halves(xc) * sin)
    return jnp.concatenate(cols, axis=1)


def _pool_mix(u, halo, tok0):
    ts = u.shape[0]
    a1 = jnp.concatenate([halo, u], axis=0)
    length = ts + POOL_HALO
    a2 = a1[1:] + a1[:length - 1]
    a4 = a2[2:] + a2[:length - 3]
    a8 = a4[4:] + a4[:length - 7]
    a16 = a8[8:] + a8[:length - 15]
    sums = {2: a2[POOL_HALO - 1:], 4: a4[POOL_HALO - 3:], 8: a8[POOL_HALO - 7:], 16: a16[POOL_HALO - 15:]}

    tok = tok0 + lax.broadcasted_iota(jnp.int32, (ts, 1), 0)
    chan = lax.broadcasted_iota(jnp.int32, (1, POOL_WIDTH), 1)
    d = jnp.zeros((ts, POOL_WIDTH), F32)
    for gi, w in enumerate(POOL_WINDOWS):
        inv_cnt = 1.0 / jnp.minimum(tok + 1, w).astype(F32)
        in_group = (chan >= gi * POOL_GROUP_DIM) & (chan < (gi + 1) * POOL_GROUP_DIM)
        d = jnp.where(in_group, sums[w] * inv_cnt, d)
    return d - u


def _inproj_body(sink_ref, x_ref, g_ref, w_ref, wlr_ref, wa2_ref, ba_ref, cos_ref, sin_ref, wbd_ref,
                 ps_ref, proj_ref, gk_ref, h_ref, ypool_ref, yatt_ref, halo_ref, kprev_ref, vprev_ref,
                 *, tiles_per_seq):
    tm = x_ref.shape[0]
    blk = ATT_BLOCK
    h = _rms(x_ref[...], g_ref[...]).astype(BF16)
    h_ref[...] = h
    s_idx = pl.program_id(0) % tiles_per_seq

    @pl.when(s_idx == 0)
    def _():
        halo_ref[...] = jnp.zeros_like(halo_ref)
        kprev_ref[...] = jnp.zeros_like(kprev_ref)
        vprev_ref[...] = jnp.zeros_like(vprev_ref)

    pu = jnp.dot(h, w_ref[:, _IN_PU:_IN_PU + POOL_WIDTH], preferred_element_type=F32)
    d = _pool_mix(pu, halo_ref[...], s_idx * tm).astype(BF16)
    halo_ref[...] = pu[tm - POOL_HALO:]

    lr = jnp.dot(h, wlr_ref[...], preferred_element_type=F32).astype(BF16)

    cos, sin = cos_ref[...], sin_ref[...]
    scale = HEAD_DIM ** -0.5
    q = _rope(jnp.dot(h, w_ref[:, _IN_QA:_IN_QA + ATT_Q_W], preferred_element_type=F32),
              cos * scale, sin * scale).astype(BF16)
    k = _rope(jnp.dot(h, w_ref[:, _IN_KA:_IN_KA + ATT_KV_W], preferred_element_type=F32),
              cos, sin).astype(BF16)
    v = jnp.dot(h, w_ref[:, _IN_VA:_IN_VA + ATT_KV_W], preferred_element_type=F32).astype(BF16)
    k = jnp.concatenate([kprev_ref[...], k], axis=0)
    v = jnp.concatenate([vprev_ref[...], v], axis=0)
    kprev_ref[...] = k[tm:]
    vprev_ref[...] = v[tm:]

    for dst, src, width in _PROJ_SEGMENTS:
        p = jnp.dot(h, w_ref[:, src:src + width], preferred_element_type=F32)
        proj_ref[:, dst:dst + width] = p.astype(proj_ref.dtype)

    for r in range(tm // blk):
        has_prev = (s_idx > 0) if r == 0 else True
        o = _attend_block(q[r * blk:(r + 1) * blk], k[r * blk:(r + 2) * blk], v[r * blk:(r + 2) * blk],
                          has_prev, sink_ref)
        yatt_ref[r * blk:(r + 1) * blk, :] = o.astype(yatt_ref.dtype)

    z = jnp.dot(lr, wa2_ref[...], preferred_element_type=F32) + ba_ref[...]
    gk_ref[...] = (jnp.minimum(z, 0.0) - jnp.log1p(jnp.exp(-jnp.abs(z)))) * (1.0 / GLA_TAU)
    y = jnp.dot(d, wbd_ref[...], preferred_element_type=F32)
    ypool_ref[...] = (y * ps_ref[...]).astype(ypool_ref.dtype)


def _inproj(x, g, w_in, w_a2, b_a, cos_t, sin_t, w_bd, pool_scale, sinks, seq, layer):
    t = x.shape[0]
    tm = INPROJ_TM
    assert seq % tm == 0 and tm % ATT_BLOCK == 0
    const = lambda i: (0, 0)
    lconst = lambda i: (layer, 0, 0)
    rows = lambda i: (i, 0)
    return pl.pallas_call(
        functools.partial(_inproj_body, tiles_per_seq=seq // tm),
        out_shape=(jax.ShapeDtypeStruct((t, PROJ_W), BF16),
                   jax.ShapeDtypeStruct((t, GLA_QK_W), F32),
                   jax.ShapeDtypeStruct((t, D_MODEL), BF16),
                   jax.ShapeDtypeStruct((t, POOL_WIDTH), BF16),
                   jax.ShapeDtypeStruct((t, ATT_Q_W), BF16)),
        grid=(t // tm,),
        in_specs=[
            pl.BlockSpec(memory_space=pltpu.SMEM),
            pl.BlockSpec((tm, D_MODEL), rows),
            pl.BlockSpec((1, D_MODEL), const),
            pl.BlockSpec((None, D_MODEL, PROJ_SRC_W), lconst, pipeline_mode=pl.Buffered(1)),
            pl.BlockSpec((None, D_MODEL, LANES), lambda i: (layer, 0, _IN_LR // LANES)),
            pl.BlockSpec((None, LANES, GLA_QK_W), lconst),
            pl.BlockSpec((1, GLA_QK_W), const),
            pl.BlockSpec((tm, LANES), rows),
            pl.BlockSpec((tm, LANES), rows),
            pl.BlockSpec((None, POOL_WIDTH, POOL_WIDTH), lconst),
            pl.BlockSpec((1, POOL_WIDTH), const),
        ],
        out_specs=(pl.BlockSpec((tm, PROJ_W), rows),
                   pl.BlockSpec((tm, GLA_QK_W), rows),
                   pl.BlockSpec((tm, D_MODEL), rows),
                   pl.BlockSpec((tm, POOL_WIDTH), rows),
                   pl.BlockSpec((tm, ATT_Q_W), rows)),
        scratch_shapes=[pltpu.VMEM((POOL_HALO, POOL_WIDTH), F32),
                        pltpu.VMEM((ATT_BLOCK, ATT_KV_W), BF16),
                        pltpu.VMEM((ATT_BLOCK, ATT_KV_W), BF16)],
        compiler_params=_cparams("arbitrary"),
        name="inproj",
    )(sinks, x, g, w_in, w_in, w_a2, b_a, cos_t, sin_t, w_bd, pool_scale)


def _both_halves(x, col, half):
    xc = x[:, col * LANES:(col + 1) * LANES].astype(F32)
    sw = pltpu.roll(xc, LANES // 2, 1)
    first = lax.broadcasted_iota(jnp.int32, xc.shape, 1) < LANES // 2
    dup = jnp.where(first, xc, sw) if half == 0 else jnp.where(first, sw, xc)
    return dup.astype(BF16)


def _attend_block(q, k, v, has_prev, sink_ref):
    blk = ATT_BLOCK
    group = ATT_HEADS // ATT_KV_HEADS

    qi = lax.broadcasted_iota(jnp.int32, (blk, 2 * blk), 0)
    si = lax.broadcasted_iota(jnp.int32, (blk, 2 * blk), 1)
    rel = blk + qi - si
    mask = (rel >= 0) & (rel < blk) & ((si >= blk) | has_prev)
    bias = jnp.where(mask, 0.0, -1e30)
    first = lax.broadcasted_iota(jnp.int32, (blk, LANES), 1) < LANES // 2
    zero = jnp.zeros((blk, LANES), BF16)

    heads = [None] * ATT_HEADS
    for g in range(ATT_KV_HEADS):
        kk = _both_halves(k, g // 2, g % 2)
        vv = _both_halves(v, g // 2, g % 2)
        hs = range(g * group, (g + 1) * group)
        qm = jnp.concatenate(
            [jnp.where(first if hd % 2 == 0 else ~first, q[:, (hd // 2) * LANES:(hd // 2 + 1) * LANES], zero)
             for hd in hs], axis=0)
        s3 = lax.dot_general(qm, kk, (((1,), (1,)), ((), ())), preferred_element_type=F32)
        es, denoms = [], []
        for r, hd in enumerate(hs):
            s = s3[r * blk:(r + 1) * blk] + bias
            sink = sink_ref[hd]
            m = jnp.maximum(jnp.max(s, axis=1, keepdims=True), sink)
            e = jnp.exp(s - m)
            denoms.append(jnp.sum(e, axis=1, keepdims=True) + jnp.exp(sink - m))
            es.append(e.astype(BF16))
        o3 = jnp.dot(jnp.concatenate(es, axis=0), vv, preferred_element_type=F32)
        for r, hd in enumerate(hs):
            heads[hd] = o3[r * blk:(r + 1) * blk] / denoms[r]
    cols = [jnp.where(first, heads[2 * c], heads[2 * c + 1]) for c in range(ATT_HEADS // 2)]
    return jnp.concatenate(cols, axis=1)


def _gla_body(qk_ref, v_ref, og_ref, gk_ref, norm_ref, tril_ref, bd_ref, o_ref, st_ref, oacc_ref):
    s_idx = pl.program_id(1)
    npair, ts = qk_ref.shape[0], qk_ref.shape[1]
    c = GLA_CHUNK

    @pl.when(s_idx == 0)
    def _():
        st_ref[...] = jnp.zeros_like(st_ref)

    tril = tril_ref[...]
    prep = []
    for p in range(npair):
        qk = qk_ref[p].astype(F32)
        q = qk[:, :GLA_QK_W] * (GLA_DK ** -0.5)
        k = qk[:, GLA_QK_W:]
        gk = gk_ref[p]
        g1 = gk.astype(BF16)
        r1 = gk - g1.astype(F32)
        g2 = r1.astype(BF16)
        g3 = (r1 - g2.astype(F32)).astype(BF16)
        b = (jnp.dot(tril, g1, preferred_element_type=F32) + jnp.dot(tril, g2, preferred_element_type=F32)
             + jnp.dot(tril, g3, preferred_element_type=F32))
        prep.append(((q * jnp.exp(b)).astype(BF16), (k * jnp.exp(-b)).astype(BF16), k, b, v_ref[p]))

    kcol = lax.broadcasted_iota(jnp.int32, (1, GLA_QK_W), 1)
    vcol = lax.broadcasted_iota(jnp.int32, (1, GLA_V_W), 1)
    ri = lax.broadcasted_iota(jnp.int32, (c, c), 0)
    ci = lax.broadcasted_iota(jnp.int32, (c, c), 1)
    causal = ri >= ci
    bd = bd_ref[...]

    units = [(n, p) for n in range(ts // c) for p in range(npair)]
    scores = {}
    for n, p in units:
        rows = slice(n * c, (n + 1) * c)
        q_n = prep[p][0][rows]
        q4 = jnp.concatenate(
            [jnp.where((kcol >= h * GLA_DK) & (kcol < (h + 1) * GLA_DK), q_n, jnp.zeros_like(q_n))
             for h in range(GLA_HEADS)], axis=0)
        scores[n, p] = lax.dot_general(q4, prep[p][1][rows], (((1,), (1,)), ((), ())),
                                       preferred_element_type=F32)
    for n, p in units:
        rows = slice(n * c, (n + 1) * c)
        v_n = prep[p][4][rows]
        o_n = jnp.zeros((c, GLA_V_W), F32)
        for h in range(GLA_HEADS):
            a_h = jnp.where(causal, scores[n, p][h * c:(h + 1) * c], 0.0).astype(BF16)
            o_h = jnp.dot(a_h, v_n, preferred_element_type=F32)
            o_n = jnp.where((vcol >= h * GLA_DV) & (vcol < (h + 1) * GLA_DV), o_h, o_n)
        oacc_ref[p, rows, :] = o_n
    for n, p in units:
        rows = slice(n * c, (n + 1) * c)
        q_t, _, k, b, v = prep[p]
        b_n = b[rows]
        b_last = b_n[c - 1:c, :]
        k_s = (k[rows] * jnp.exp(b_last - b_n)).astype(BF16)
        st = st_ref[p]
        oacc_ref[p, rows, :] += lax.dot_general(q_t[rows], st.astype(BF16), (((1,), (1,)), ((), ())),
                                                preferred_element_type=F32)
        kv = lax.dot_general(v[rows], k_s, (((0,), (0,)), ((), ())), preferred_element_type=F32)
        st_ref[p] = st * jnp.exp(b_last) + kv * bd

    for p in range(npair):
        o = oacc_ref[p]
        o2 = o * o
        inv = jnp.zeros_like(o)
        for h in range(GLA_HEADS):
            in_head = (vcol >= h * GLA_DV) & (vcol < (h + 1) * GLA_DV)
            ms = jnp.sum(jnp.where(in_head, o2, 0.0), axis=1, keepdims=True) * (1.0 / GLA_DV)
            inv = jnp.where(in_head, lax.rsqrt(ms + EPS), inv)
        og = og_ref[p].astype(F32)
        o_ref[p] = (o * inv * norm_ref[...] * (og * jax.nn.sigmoid(og))).astype(o_ref.dtype)


def _gla(proj, gk, norm, batch, seq):
    t = proj.shape[0]
    ts = GLA_TS
    ns = seq // ts
    pair = GLA_PAIR
    assert batch % pair == 0
    r = np.arange(ts)
    tril = ((r[:, None] // GLA_CHUNK == r[None, :] // GLA_CHUNK) & (r[:, None] >= r[None, :]))
    tril = jnp.asarray(tril, BF16)
    bd = (np.arange(GLA_V_W)[:, None] // GLA_DV) == (np.arange(GLA_QK_W)[None, :] // GLA_DK)
    bd = jnp.asarray(bd, F32)
    proj3 = proj.reshape(batch, seq, PROJ_W)
    out = pl.pallas_call(
        _gla_body,
        out_shape=jax.ShapeDtypeStruct((batch, seq, GLA_V_W), BF16),
        grid=(batch // pair, ns),
        in_specs=[
            pl.BlockSpec((pair, ts, 2 * GLA_QK_W), lambda b, s: (b, s, _BLK_GQK)),
            pl.BlockSpec((pair, ts, GLA_V_W), lambda b, s: (b, s, _BLK_GV)),
            pl.BlockSpec((pair, ts, GLA_V_W), lambda b, s: (b, s, _BLK_GO)),
            pl.BlockSpec((pair, ts, GLA_QK_W), lambda b, s: (b, s, 0)),
            pl.BlockSpec((1, GLA_V_W), lambda b, s: (0, 0)),
            pl.BlockSpec((ts, ts), lambda b, s: (0, 0)),
            pl.BlockSpec((GLA_V_W, GLA_QK_W), lambda b, s: (0, 0)),
        ],
        out_specs=pl.BlockSpec((pair, ts, GLA_V_W), lambda b, s: (b, s, 0)),
        scratch_shapes=[pltpu.VMEM((pair, GLA_V_W, GLA_QK_W), F32),
                        pltpu.VMEM((pair, ts, GLA_V_W), F32)],
        compiler_params=_cparams("parallel", "arbitrary"),
        name="gla",
    )(proj3, proj3, proj3, gk.reshape(batch, seq, GLA_QK_W), norm, tril, bd)
    return out.reshape(t, GLA_V_W)


def _merge_body(h_ref, ya_ref, yp_ref, yg_ref, wga_ref, wgp_ref, wgg_ref, tga_ref, tgp_ref, tgg_ref,
                bga_ref, bgp_ref, bgg_ref, wa_ref, wp_ref, wg_ref, o_ref, wgate_ref):
    tn = o_ref.shape[1]

    @pl.when(pl.program_id(1) == 0)
    def _():
        for br, (win_ref, tail_ref) in enumerate(((wga_ref, tga_ref), (wgp_ref, tgp_ref),
                                                  (wgg_ref, tgg_ref))):
            win = jnp.concatenate([win_ref[...], tail_ref[...]], axis=1).astype(F32)
            wgate_ref[br] = win[:, GATE_LEAD:GATE_LEAD + tn].astype(BF16)

    h = h_ref[...]
    acc = None
    for br, (y_ref, bgate_ref, wbr_ref) in enumerate(((ya_ref, bga_ref, wa_ref),
                                                     (yp_ref, bgp_ref, wp_ref),
                                                     (yg_ref, bgg_ref, wg_ref))):
        gate = jax.nn.sigmoid(jnp.dot(h, wgate_ref[br], preferred_element_type=F32) + bgate_ref[...])
        term = gate * jnp.dot(y_ref[...], wbr_ref[...], preferred_element_type=F32)
        acc = term if acc is None else acc + term
    o_ref[...] = acc.astype(o_ref.dtype)


def _merge(h, y_att, y_pool, y_gla, w_in, b_gate, w_br_att, w_br_pool, w_br_gla, layer):
    t = h.shape[0]
    tm, tn = MERGE_TM, MERGE_TN
    nn = D_MODEL // tn
    row = lambda j, i: (i, 0)
    y_spec = pl.BlockSpec((tm, ATT_Q_W), row)
    gate_base = _IN_GATE - GATE_LEAD
    gate_specs = [pl.BlockSpec((pl.Squeezed(), pl.Element(D_MODEL), pl.Element(tn)),
                               functools.partial(
                                   lambda j, i, br: (layer, 0, pl.multiple_of(
                                       gate_base + br * D_MODEL + j * tn, LANES)), br=br))
                  for br in range(N_BRANCH)]
    tail_specs = [pl.BlockSpec((None, D_MODEL, LANES),
                               functools.partial(
                                   lambda j, i, br: (layer, 0, (gate_base + br * D_MODEL) // LANES
                                                     + (j + 1) * (tn // LANES)), br=br))
                  for br in range(N_BRANCH)]
    bias_specs = [pl.BlockSpec((1, tn), functools.partial(lambda j, i, br: (0, br * nn + j), br=br))
                  for br in range(N_BRANCH)]
    br_spec = pl.BlockSpec((None, ATT_Q_W, tn), lambda j, i: (layer, 0, j))
    return pl.pallas_call(
        _merge_body,
        out_shape=jax.ShapeDtypeStruct((t, D_MODEL), BF16),
        grid=(nn, t // tm),
        in_specs=[pl.BlockSpec((tm, D_MODEL), row), y_spec, y_spec, y_spec,
                  *gate_specs, *tail_specs, *bias_specs, br_spec, br_spec, br_spec],
        out_specs=pl.BlockSpec((tm, tn), lambda j, i: (i, j)),
        scratch_shapes=[pltpu.VMEM((N_BRANCH, D_MODEL, tn), BF16)],
        compiler_params=_cparams("arbitrary", "arbitrary"),
        name="gated_merge",
    )(h, y_att, y_pool, y_gla, w_in, w_in, w_in, w_in, w_in, w_in, b_gate, b_gate, b_gate,
      w_br_att, w_br_pool, w_br_gla)


def _outproj_body(m_ref, w_ref, x_ref, o_ref):
    o_ref[...] = x_ref[...] + jnp.dot(m_ref[...], w_ref[...], preferred_element_type=F32)


def _outproj(merged, w_out, x, layer):
    t = x.shape[0]
    tm, tn = OUT_TM, OUT_TN
    return pl.pallas_call(
        _outproj_body,
        out_shape=jax.ShapeDtypeStruct((t, D_MODEL), F32),
        grid=(t // tm, D_MODEL // tn),
        in_specs=[
            pl.BlockSpec((tm, D_MODEL), lambda i, j: (i, 0)),
            pl.BlockSpec((None, D_MODEL, tn), lambda i, j: (layer, 0, j),
                         pipeline_mode=pl.Buffered(1) if tn == D_MODEL else None),
            pl.BlockSpec((tm, tn), lambda i, j: (i, j)),
        ],
        out_specs=pl.BlockSpec((tm, tn), lambda i, j: (i, j)),
        compiler_params=_cparams("parallel", "arbitrary"),
        name="outproj",
    )(merged, w_out, x)


def _prep_mixer_weights(w_in, w_pool, w_gla_a2, w_br_att, w_br_pool, w_br_gla, w_out):
    depth = w_in.shape[0]
    w_in_bf = w_in.astype(BF16)
    w_a2 = jnp.pad(w_gla_a2.astype(BF16), ((0, 0), (0, LANES - GLA_LOWRANK), (0, 0)))
    groups = len(POOL_WINDOWS)
    eye = jnp.eye(groups, dtype=BF16)
    w_bd = (w_pool.astype(BF16)[:, :, :, None, :] * eye[None, :, None, :, None]).reshape(
        depth, POOL_WIDTH, POOL_WIDTH)
    return dict(w_in=w_in_bf, w_a2=w_a2, w_bd=w_bd,
                w_br_att=w_br_att.astype(BF16), w_br_pool=w_br_pool.astype(BF16),
                w_br_gla=w_br_gla.astype(BF16), w_out=w_out.astype(BF16))


def _mixer(x, cos_t, sin_t, batch, seq, layer, mw, norm_mix, b_gate, att_sinks, pool_scale,
           b_gla_a, gla_norm):
    proj, gk, h, y_pool, y_att = _inproj(x, norm_mix[None, :], mw["w_in"], mw["w_a2"],
                                         b_gla_a[None, :], cos_t, sin_t, mw["w_bd"],
                                         pool_scale[None, :], att_sinks, seq, layer)
    y_gla = _gla(proj, gk, gla_norm[None, :], batch, seq)
    merged = _merge(h, y_att, y_pool, y_gla, mw["w_in"], b_gate[None, :],
                    mw["w_br_att"], mw["w_br_pool"], mw["w_br_gla"], layer)
    return _outproj(merged, mw["w_out"], x, layer)


def kernel(x, positions, norm_ffn1, ffn1_wi, ffn1_wo, norm_mix, w_in, b_gate, att_sinks, w_pool, pool_scale, w_gla_a2, b_gla_a, gla_norm, w_br_att, w_br_pool, w_br_gla, w_out, norm_ffn2, ffn2_wi, ffn2_wo, norm_final):
    batch, seq, d = x.shape
    depth = norm_ffn1.shape[0]
    assert d == D_MODEL and seq % max(INPROJ_TM, GLA_TS, ATT_BLOCK) == 0
    t = batch * seq
    assert t % max(FFN_TM, MERGE_TM, OUT_TM, ROPE_TM, INPROJ_TM) == 0
    xt = x.reshape(t, d)
    cos_t, sin_t = _rope_tables(positions)
    gf = norm_final[None, :]
    mw = _prep_mixer_weights(w_in, w_pool, w_gla_a2, w_br_att, w_br_pool, w_br_gla, w_out)
    for l in range(depth):
        xt = _ffn(xt, norm_ffn1[l][None, :], ffn1_wi, ffn1_wo, gf, l, False)
        xt = _mixer(xt, cos_t, sin_t, batch, seq, l, mw, norm_mix[l], b_gate[l], att_sinks[l],
                    pool_scale[l], b_gla_a[l], gla_norm[l])
        xt = _ffn(xt, norm_ffn2[l][None, :], ffn2_wi, ffn2_wo, gf, l, l == depth - 1)
    return xt.reshape(batch, seq, d)
```

```python
import functools

import numpy as np
import jax
import jax.numpy as jnp
from jax import lax
from jax.experimental import pallas as pl
from jax.experimental.pallas import tpu as pltpu

F32 = jnp.float32
BF16 = jnp.bfloat16

D_MODEL = 2048
D_FF = 5632
ATT_HEADS = 12
ATT_KV_HEADS = 4
HEAD_DIM = 64
ATT_BLOCK = 128
ROPE_THETA = 10000.0
ATT_Q_W = ATT_HEADS * HEAD_DIM
ATT_KV_W = ATT_KV_HEADS * HEAD_DIM
POOL_WINDOWS = (2, 4, 8, 16)
POOL_GROUP_DIM = 192
POOL_WIDTH = 768
GLA_HEADS = 4
GLA_DK = 96
GLA_DV = 192
GLA_QK_W = GLA_HEADS * GLA_DK
GLA_V_W = GLA_HEADS * GLA_DV
GLA_LOWRANK = 16
GLA_TAU = 16.0
GLA_CHUNK = 64
N_BRANCH = 3
EPS = 1e-6

_IN_QA, _IN_KA, _IN_VA, _IN_PU = 0, 768, 1024, 1280
_IN_QG, _IN_KG, _IN_VG, _IN_OG = 2048, 2432, 2816, 3584
_IN_LR = 4352
_IN_GATE = 4368

PROJ_SRC_W = _IN_LR
PROJ_W = 2 * GLA_QK_W + 2 * GLA_V_W
_BLK_GQK, _BLK_GV, _BLK_GO = 0, 1, 2
_PROJ_SEGMENTS = ((0, _IN_QG, 2 * GLA_QK_W), (2 * GLA_QK_W, _IN_VG, 2 * GLA_V_W))
W_IN_COLS = _IN_GATE + N_BRANCH * D_MODEL
GATE_LEAD = _IN_GATE % 128

LANES = 128
POOL_HALO = 16
VMEM_LIMIT = 58 * 1024 * 1024

FFN_TM, FFN_TF = 1024, 256
FFN_UNROLL = 4
INPROJ_TM = 512
GLA_TS = 256
GLA_PAIR = 4
MERGE_TM, MERGE_TN = 1024, 512
OUT_TM, OUT_TN = 1024, 2048
ROPE_TM = 1024


def _cparams(*sem):
    return pltpu.CompilerParams(dimension_semantics=sem, vmem_limit_bytes=VMEM_LIMIT)


def _rms(x, g):
    return x * lax.rsqrt(jnp.mean(x * x, axis=-1, keepdims=True) + EPS) * g


def _ffn_body(x_ref, g_ref, gf_ref, wi_hbm, wo_hbm, o_ref, h_ref, u_ref, wa_buf, wb_buf, wo_buf, sem,
              *, layer, n_tiles, final_norm):
    tf = FFN_TF
    nch = D_FF // tf
    i = pl.program_id(0)

    def up_copies(c, slot):
        col = pl.multiple_of(c * tf, tf)
        return (pltpu.make_async_copy(wi_hbm.at[layer, :, pl.ds(col, tf)], wa_buf.at[slot], sem.at[0, slot]),
                pltpu.make_async_copy(wi_hbm.at[layer, :, pl.ds(D_FF + col, tf)], wb_buf.at[slot],
                                      sem.at[1, slot]))

    def down_copy(c, slot):
        row = pl.multiple_of(c * tf, tf)
        return pltpu.make_async_copy(wo_hbm.at[layer, pl.ds(row, tf), :], wo_buf.at[slot], sem.at[2, slot])

    def up(slot):
        h = h_ref[...]
        a = jnp.dot(h, wa_buf[slot].astype(BF16), preferred_element_type=F32)
        b = jnp.dot(h, wb_buf[slot].astype(BF16), preferred_element_type=F32)
        u_ref[slot] = (a * jax.nn.sigmoid(a) * (0.5 * b)).astype(BF16)

    def down(slot):
        o_ref[...] += jnp.dot(u_ref[slot], wo_buf[slot].astype(BF16), preferred_element_type=F32)

    def start_up(c, slot):
        ca, cb = up_copies(c, slot)
        ca.start(priority=1)
        cb.start(priority=1)

    def stage(c, slot):
        for cp in up_copies(c, slot):
            cp.wait()
        down_copy(c - 1, 1 - slot).wait()
        down_copy(c, slot).start(priority=1)
        down(1 - slot)
        up(slot)

    @pl.when(i == 0)
    def _():
        start_up(0, 0)

    x = x_ref[...]
    h_ref[...] = _rms(x, g_ref[...]).astype(BF16)
    o_ref[...] = x
    for cp in up_copies(0, 0):
        cp.wait()
    start_up(1, 1)
    down_copy(0, 0).start(priority=1)
    up(0)

    def stages(k, carry):
        c0 = FFN_UNROLL * k + 1
        for s in range(FFN_UNROLL):
            slot = (1 + s) % 2
            start_up(c0 + s + 1, 1 - slot)
            stage(c0 + s, slot)
        return carry

    lax.fori_loop(0, (nch - 2) // FFN_UNROLL, stages, 0)

    stage(nch - 1, 1)

    @pl.when(i + 1 < n_tiles)
    def _():
        start_up(0, 0)

    down_copy(nch - 1, 1).wait()
    down(1)
    if final_norm:
        o_ref[...] = _rms(o_ref[...], gf_ref[...])


def _ffn(x, g, wi, wo, gf, layer, final_norm):
    t = x.shape[0]
    tm, tf = FFN_TM, FFN_TF
    assert FFN_UNROLL % 2 == 0 and (D_FF // tf - 2) % FFN_UNROLL == 0
    n_tiles = t // tm
    return pl.pallas_call(
        functools.partial(_ffn_body, layer=layer, n_tiles=n_tiles, final_norm=final_norm),
        out_shape=jax.ShapeDtypeStruct((t, D_MODEL), F32),
        grid=(n_tiles,),
        in_specs=[
            pl.BlockSpec((tm, D_MODEL), lambda i: (i, 0)),
            pl.BlockSpec((1, D_MODEL), lambda i: (0, 0)),
            pl.BlockSpec((1, D_MODEL), lambda i: (0, 0)),
            pl.BlockSpec(memory_space=pl.ANY),
            pl.BlockSpec(memory_space=pl.ANY),
        ],
        out_specs=pl.BlockSpec((tm, D_MODEL), lambda i: (i, 0)),
        scratch_shapes=[
            pltpu.VMEM((tm, D_MODEL), BF16),
            pltpu.VMEM((2, tm, tf), BF16),
            pltpu.VMEM((2, D_MODEL, tf), F32),
            pltpu.VMEM((2, D_MODEL, tf), F32),
            pltpu.VMEM((2, tf, D_MODEL), F32),
            pltpu.SemaphoreType.DMA((3, 2)),
        ],
        compiler_params=_cparams("arbitrary"),
        name="ffn_final" if final_norm else "ffn",
    )(x, g, gf, wi, wo)


def _rope_table_body(pos_ref, freq_ref, sign_ref, cos_ref, sin_ref):
    ang = pos_ref[...].astype(F32) * freq_ref[...]
    cos_ref[...] = jnp.cos(ang)
    sin_ref[...] = jnp.sin(ang) * sign_ref[...]


def _rope_tables(positions):
    t = positions.size
    half = HEAD_DIM // 2
    inv_freq = ROPE_THETA ** (-jnp.arange(half, dtype=F32) / half)
    freq = jnp.tile(inv_freq, LANES // half)[None, :]
    sign = jnp.tile(jnp.concatenate([-jnp.ones((half,), F32), jnp.ones((half,), F32)]),
                    LANES // HEAD_DIM)[None, :]
    tm = ROPE_TM
    return pl.pallas_call(
        _rope_table_body,
        out_shape=(jax.ShapeDtypeStruct((t, LANES), F32), jax.ShapeDtypeStruct((t, LANES), F32)),
        grid=(t // tm,),
        in_specs=[
            pl.BlockSpec((tm, 1), lambda i: (i, 0)),
            pl.BlockSpec((1, LANES), lambda i: (0, 0)),
            pl.BlockSpec((1, LANES), lambda i: (0, 0)),
        ],
        out_specs=(pl.BlockSpec((tm, LANES), lambda i: (i, 0)),
                   pl.BlockSpec((tm, LANES), lambda i: (i, 0))),
        compiler_params=_cparams("parallel"),
        name="rope_tables",
    )(positions.reshape(t, 1), freq, sign)


def _swap_halves(x):
    lane = lax.broadcasted_iota(jnp.int32, x.shape, 1)
    first = (lane % HEAD_DIM) < (HEAD_DIM // 2)
    return jnp.where(first, pltpu.roll(x, LANES - HEAD_DIM // 2, 1), pltpu.roll(x, HEAD_DIM // 2, 1))


def _rope(x, cos, sin):
    cols = []
    for c in range(x.shape[1] // LANES):
        xc = x[:, c * LANES:(c + 1) * LANES]
        cols.append(xc * cos + _swap_halves(xc) * sin)
    return jnp.concatenate(cols, axis=1)


def _pool_mix(u, halo, tok0):
    ts = u.shape[0]
    a1 = jnp.concatenate([halo, u], axis=0)
    length = ts + POOL_HALO
    a2 = a1[1:] + a1[:length - 1]
    a4 = a2[2:] + a2[:length - 3]
    a8 = a4[4:] + a4[:length - 7]
    a16 = a8[8:] + a8[:length - 15]
    sums = {2: a2[POOL_HALO - 1:], 4: a4[POOL_HALO - 3:], 8: a8[POOL_HALO - 7:], 16: a16[POOL_HALO - 15:]}

    tok = tok0 + lax.broadcasted_iota(jnp.int32, (ts, 1), 0)
    chan = lax.broadcasted_iota(jnp.int32, (1, POOL_WIDTH), 1)
    d = jnp.zeros((ts, POOL_WIDTH), F32)
    for gi, w in enumerate(POOL_WINDOWS):
        inv_cnt = 1.0 / jnp.minimum(tok + 1, w).astype(F32)
        in_group = (chan >= gi * POOL_GROUP_DIM) & (chan < (gi + 1) * POOL_GROUP_DIM)
        d = jnp.where(in_group, sums[w] * inv_cnt, d)
    return d - u


def _inproj_body(sink_ref, x_ref, g_ref, w_ref, wlr_ref, wa2_ref, ba_ref, cos_ref, sin_ref, wbd_ref,
                 ps_ref, proj_ref, gk_ref, h_ref, ypool_ref, yatt_ref, halo_ref, kprev_ref, vprev_ref,
                 *, tiles_per_seq):
    tm = x_ref.shape[0]
    blk = ATT_BLOCK
    s_idx = pl.program_id(0) % tiles_per_seq

    @pl.when(s_idx == 0)
    def _():
        halo_ref[...] = jnp.zeros_like(halo_ref)
        kprev_ref[...] = jnp.zeros_like(kprev_ref)
        vprev_ref[...] = jnp.zeros_like(vprev_ref)

    hs, pus = [], []
    for r in range(tm // blk):
        hr = _rms(x_ref[r * blk:(r + 1) * blk, :], g_ref[...]).astype(BF16)
        hs.append(hr)
        pus.append(jnp.dot(hr, w_ref[:, _IN_PU:_IN_PU + POOL_WIDTH], preferred_element_type=F32))
    h = jnp.concatenate(hs, axis=0)
    h_ref[...] = h
    pu = jnp.concatenate(pus, axis=0)
    d = _pool_mix(pu, halo_ref[...], s_idx * tm).astype(BF16)
    halo_ref[...] = pu[tm - POOL_HALO:]

    lr = jnp.dot(h, wlr_ref[...], preferred_element_type=F32).astype(BF16)

    cos, sin = cos_ref[...], sin_ref[...]
    scale = HEAD_DIM ** -0.5
    q = _rope(jnp.dot(h, w_ref[:, _IN_QA:_IN_QA + ATT_Q_W], preferred_element_type=F32),
              cos * scale, sin * scale).astype(BF16)
    k = _rope(jnp.dot(h, w_ref[:, _IN_KA:_IN_KA + ATT_KV_W], preferred_element_type=F32),
              cos, sin).astype(BF16)
    v = jnp.dot(h, w_ref[:, _IN_VA:_IN_VA + ATT_KV_W], preferred_element_type=F32).astype(BF16)
    k = jnp.concatenate([kprev_ref[...], k], axis=0)
    v = jnp.concatenate([vprev_ref[...], v], axis=0)
    kprev_ref[...] = k[tm:]
    vprev_ref[...] = v[tm:]

    z = jnp.dot(lr, wa2_ref[...], preferred_element_type=F32) + ba_ref[...]
    gk_ref[...] = (jnp.minimum(z, 0.0) - jnp.log1p(jnp.exp(-jnp.abs(z)))) * (1.0 / GLA_TAU)
    y = jnp.dot(d, wbd_ref[...], preferred_element_type=F32)
    ypool_ref[...] = (y * ps_ref[...]).astype(ypool_ref.dtype)

    nblk = tm // blk
    mxu_w = 2 * LANES
    bounds = [mxu_w * ((PROJ_W // mxu_w) * r // nblk) for r in range(nblk)] + [PROJ_W]
    for r in range(nblk):
        has_prev = (s_idx > 0) if r == 0 else True
        o = _attend_block(q[r * blk:(r + 1) * blk], k[r * blk:(r + 2) * blk], v[r * blk:(r + 2) * blk],
                          has_prev, sink_ref)
        yatt_ref[r * blk:(r + 1) * blk, :] = o.astype(yatt_ref.dtype)
        for dst, src, width in _PROJ_SEGMENTS:
            lo, hi = max(dst, bounds[r]), min(dst + width, bounds[r + 1])
            if lo < hi:
                p = jnp.dot(h, w_ref[:, src + lo - dst:src + hi - dst], preferred_element_type=F32)
                proj_ref[:, lo:hi] = p.astype(proj_ref.dtype)


def _inproj(x, g, w_in, w_a2, b_a, cos_t, sin_t, w_bd, pool_scale, sinks, seq, layer):
    t = x.shape[0]
    tm = INPROJ_TM
    assert seq % tm == 0 and tm % ATT_BLOCK == 0
    const = lambda i: (0, 0)
    lconst = lambda i: (layer, 0, 0)
    rows = lambda i: (i, 0)
    return pl.pallas_call(
        functools.partial(_inproj_body, tiles_per_seq=seq // tm),
        out_shape=(jax.ShapeDtypeStruct((t, PROJ_W), BF16),
                   jax.ShapeDtypeStruct((t, GLA_QK_W), F32),
                   jax.ShapeDtypeStruct((t, D_MODEL), BF16),
                   jax.ShapeDtypeStruct((t, POOL_WIDTH), BF16),
                   jax.ShapeDtypeStruct((t, ATT_Q_W), BF16)),
        grid=(t // tm,),
        in_specs=[
            pl.BlockSpec(memory_space=pltpu.SMEM),
            pl.BlockSpec((tm, D_MODEL), rows),
            pl.BlockSpec((1, D_MODEL), const),
            pl.BlockSpec((None, D_MODEL, PROJ_SRC_W), lconst, pipeline_mode=pl.Buffered(1)),
            pl.BlockSpec((None, D_MODEL, LANES), lambda i: (layer, 0, _IN_LR // LANES)),
            pl.BlockSpec((None, LANES, GLA_QK_W), lconst),
            pl.BlockSpec((1, GLA_QK_W), const),
            pl.BlockSpec((tm, LANES), rows),
            pl.BlockSpec((tm, LANES), rows),
            pl.BlockSpec((None, POOL_WIDTH, POOL_WIDTH), lconst),
            pl.BlockSpec((1, POOL_WIDTH), const),
        ],
        out_specs=(pl.BlockSpec((tm, PROJ_W), rows),
                   pl.BlockSpec((tm, GLA_QK_W), rows),
                   pl.BlockSpec((tm, D_MODEL), rows),
                   pl.BlockSpec((tm, POOL_WIDTH), rows),
                   pl.BlockSpec((tm, ATT_Q_W), rows)),
        scratch_shapes=[pltpu.VMEM((POOL_HALO, POOL_WIDTH), F32),
                        pltpu.VMEM((ATT_BLOCK, ATT_KV_W), BF16),
                        pltpu.VMEM((ATT_BLOCK, ATT_KV_W), BF16)],
        compiler_params=_cparams("arbitrary"),
        name="inproj",
    )(sinks, x, g, w_in, w_in, w_a2, b_a, cos_t, sin_t, w_bd, pool_scale)


def _both_halves(x, col, half):
    xc = x[:, col * LANES:(col + 1) * LANES].astype(F32)
    sw = pltpu.roll(xc, LANES // 2, 1)
    first = lax.broadcasted_iota(jnp.int32, xc.shape, 1) < LANES // 2
    dup = jnp.where(first, xc, sw) if half == 0 else jnp.where(first, sw, xc)
    return dup.astype(BF16)


def _attend_block(q, k, v, has_prev, sink_ref):
    blk = ATT_BLOCK
    group = ATT_HEADS // ATT_KV_HEADS

    qi = lax.broadcasted_iota(jnp.int32, (blk, 2 * blk), 0)
    si = lax.broadcasted_iota(jnp.int32, (blk, 2 * blk), 1)
    rel = blk + qi - si
    mask = (rel >= 0) & (rel < blk) & ((si >= blk) | has_prev)
    bias = jnp.where(mask, 0.0, -1e30)
    first = lax.broadcasted_iota(jnp.int32, (blk, LANES), 1) < LANES // 2
    zero = jnp.zeros((blk, LANES), BF16)

    heads = [None] * ATT_HEADS
    for g in range(ATT_KV_HEADS):
        kk = _both_halves(k, g // 2, g % 2)
        vv = _both_halves(v, g // 2, g % 2)
        hs = range(g * group, (g + 1) * group)
        qm = jnp.concatenate(
            [jnp.where(first if hd % 2 == 0 else ~first, q[:, (hd // 2) * LANES:(hd // 2 + 1) * LANES], zero)
             for hd in hs], axis=0)
        s3 = lax.dot_general(qm, kk, (((1,), (1,)), ((), ())), preferred_element_type=F32)
        es, denoms = [], []
        for r, hd in enumerate(hs):
            s = s3[r * blk:(r + 1) * blk] + bias
            sink = sink_ref[hd]
            m = jnp.maximum(jnp.max(s, axis=1, keepdims=True), sink)
            e = jnp.exp(s - m)
            denoms.append(jnp.sum(e, axis=1, keepdims=True) + jnp.exp(sink - m))
            es.append(e.astype(BF16))
        o3 = jnp.dot(jnp.concatenate(es, axis=0), vv, preferred_element_type=F32)
        for r, hd in enumerate(hs):
            heads[hd] = o3[r * blk:(r + 1) * blk] / denoms[r]
    cols = [jnp.where(first, heads[2 * c], heads[2 * c + 1]) for c in range(ATT_HEADS // 2)]
    return jnp.concatenate(cols, axis=1)


def _gla_body(qk_ref, v_ref, og_ref, gk_ref, norm_ref, tril_ref, bd_ref, o_ref, st_ref, oacc_ref):
    s_idx = pl.program_id(1)
    npair, ts = qk_ref.shape[0], qk_ref.shape[1]
    c = GLA_CHUNK

    @pl.when(s_idx == 0)
    def _():
        st_ref[...] = jnp.zeros_like(st_ref)

    tril = tril_ref[...]
    prep = []
    for p in range(npair):
        qk = qk_ref[p].astype(F32)
        q = qk[:, :GLA_QK_W] * (GLA_DK ** -0.5)
        k = qk[:, GLA_QK_W:]
        gk = gk_ref[p]
        g1 = gk.astype(BF16)
        r1 = gk - g1.astype(F32)
        g2 = r1.astype(BF16)
        g3 = (r1 - g2.astype(F32)).astype(BF16)
        b = (jnp.dot(tril, g1, preferred_element_type=F32) + jnp.dot(tril, g2, preferred_element_type=F32)
             + jnp.dot(tril, g3, preferred_element_type=F32))
        prep.append(((q * jnp.exp(b)).astype(BF16), (k * jnp.exp(-b)).astype(BF16), k, b, v_ref[p]))

    kcol = lax.broadcasted_iota(jnp.int32, (1, GLA_QK_W), 1)
    vcol = lax.broadcasted_iota(jnp.int32, (1, GLA_V_W), 1)
    ri = lax.broadcasted_iota(jnp.int32, (c, c), 0)
    ci = lax.broadcasted_iota(jnp.int32, (c, c), 1)
    causal = ri >= ci
    bd = bd_ref[...]

    units = [(n, p) for n in range(ts // c) for p in range(npair)]
    scores = {}
    for n, p in units:
        rows = slice(n * c, (n + 1) * c)
        q_n = prep[p][0][rows]
        q4 = jnp.concatenate(
            [jnp.where((kcol >= h * GLA_DK) & (kcol < (h + 1) * GLA_DK), q_n, jnp.zeros_like(q_n))
             for h in range(GLA_HEADS)], axis=0)
        scores[n, p] = lax.dot_general(q4, prep[p][1][rows], (((1,), (1,)), ((), ())),
                                       preferred_element_type=F32)
    for n, p in units:
        rows = slice(n * c, (n + 1) * c)
        v_n = prep[p][4][rows]
        o_n = jnp.zeros((c, GLA_V_W), F32)
        for h in range(GLA_HEADS):
            a_h = jnp.where(causal, scores[n, p][h * c:(h + 1) * c], 0.0).astype(BF16)
            o_h = jnp.dot(a_h, v_n, preferred_element_type=F32)
            o_n = jnp.where((vcol >= h * GLA_DV) & (vcol < (h + 1) * GLA_DV), o_h, o_n)
        oacc_ref[p, rows, :] = o_n
    for n, p in units:
        rows = slice(n * c, (n + 1) * c)
        q_t, _, k, b, v = prep[p]
        b_n = b[rows]
        b_last = b_n[c - 1:c, :]
        k_s = (k[rows] * jnp.exp(b_last - b_n)).astype(BF16)
        st = st_ref[p]
        oacc_ref[p, rows, :] += lax.dot_general(q_t[rows], st.astype(BF16), (((1,), (1,)), ((), ())),
                                                preferred_element_type=F32)
        kv = lax.dot_general(v[rows], k_s, (((0,), (0,)), ((), ())), preferred_element_type=F32)
        st_ref[p] = st * jnp.exp(b_last) + kv * bd

    for p in range(npair):
        o = oacc_ref[p]
        o2 = o * o
        inv = jnp.zeros_like(o)
        for h in range(GLA_HEADS):
            in_head = (vcol >= h * GLA_DV) & (vcol < (h + 1) * GLA_DV)
            ms = jnp.sum(jnp.where(in_head, o2, 0.0), axis=1, keepdims=True) * (1.0 / GLA_DV)
            inv = jnp.where(in_head, lax.rsqrt(ms + EPS), inv)
        og = og_ref[p].astype(F32)
        o_ref[p] = (o * inv * norm_ref[...] * (og * jax.nn.sigmoid(og))).astype(o_ref.dtype)


def _gla(proj, gk, norm, batch, seq):
    t = proj.shape[0]
    ts = GLA_TS
    ns = seq // ts
    pair = GLA_PAIR
    assert batch % pair == 0
    r = np.arange(ts)
    tril = ((r[:, None] // GLA_CHUNK == r[None, :] // GLA_CHUNK) & (r[:, None] >= r[None, :]))
    tril = jnp.asarray(tril, BF16)
    bd = (np.arange(GLA_V_W)[:, None] // GLA_DV) == (np.arange(GLA_QK_W)[None, :] // GLA_DK)
    bd = jnp.asarray(bd, F32)
    proj3 = proj.reshape(batch, seq, PROJ_W)
    out = pl.pallas_call(
        _gla_body,
        out_shape=jax.ShapeDtypeStruct((batch, seq, GLA_V_W), BF16),
        grid=(batch // pair, ns),
        in_specs=[
            pl.BlockSpec((pair, ts, 2 * GLA_QK_W), lambda b, s: (b, s, _BLK_GQK)),
            pl.BlockSpec((pair, ts, GLA_V_W), lambda b, s: (b, s, _BLK_GV)),
            pl.BlockSpec((pair, ts, GLA_V_W), lambda b, s: (b, s, _BLK_GO)),
            pl.BlockSpec((pair, ts, GLA_QK_W), lambda b, s: (b, s, 0)),
            pl.BlockSpec((1, GLA_V_W), lambda b, s: (0, 0)),
            pl.BlockSpec((ts, ts), lambda b, s: (0, 0)),
            pl.BlockSpec((GLA_V_W, GLA_QK_W), lambda b, s: (0, 0)),
        ],
        out_specs=pl.BlockSpec((pair, ts, GLA_V_W), lambda b, s: (b, s, 0)),
        scratch_shapes=[pltpu.VMEM((pair, GLA_V_W, GLA_QK_W), F32),
                        pltpu.VMEM((pair, ts, GLA_V_W), F32)],
        compiler_params=_cparams("parallel", "arbitrary"),
        name="gla",
    )(proj3, proj3, proj3, gk.reshape(batch, seq, GLA_QK_W), norm, tril, bd)
    return out.reshape(t, GLA_V_W)


def _merge_body(h_ref, ya_ref, yp_ref, yg_ref, wga_ref, wgp_ref, wgg_ref, tga_ref, tgp_ref, tgg_ref,
                bga_ref, bgp_ref, bgg_ref, wa_ref, wp_ref, wg_ref, o_ref, wgate_ref):
    tn = o_ref.shape[1]

    @pl.when(pl.program_id(1) == 0)
    def _():
        for br, (win_ref, tail_ref) in enumerate(((wga_ref, tga_ref), (wgp_ref, tgp_ref),
                                                  (wgg_ref, tgg_ref))):
            win = jnp.concatenate([win_ref[...], tail_ref[...]], axis=1).astype(F32)
            wgate_ref[br] = win[:, GATE_LEAD:GATE_LEAD + tn].astype(BF16)

    h = h_ref[...]
    acc = None
    for br, (y_ref, bgate_ref, wbr_ref) in enumerate(((ya_ref, bga_ref, wa_ref),
                                                     (yp_ref, bgp_ref, wp_ref),
                                                     (yg_ref, bgg_ref, wg_ref))):
        gate = jax.nn.sigmoid(jnp.dot(h, wgate_ref[br], preferred_element_type=F32) + bgate_ref[...])
        term = gate * jnp.dot(y_ref[...], wbr_ref[...], preferred_element_type=F32)
        acc = term if acc is None else acc + term
    o_ref[...] = acc.astype(o_ref.dtype)


def _merge(h, y_att, y_pool, y_gla, w_in, b_gate, w_br_att, w_br_pool, w_br_gla, layer):
    t = h.shape[0]
    tm, tn = MERGE_TM, MERGE_TN
    nn = D_MODEL // tn
    row = lambda j, i: (i, 0)
    y_spec = pl.BlockSpec((tm, ATT_Q_W), row)
    gate_base = _IN_GATE - GATE_LEAD
    gate_specs = [pl.BlockSpec((pl.Squeezed(), pl.Element(D_MODEL), pl.Element(tn)),
                               functools.partial(
                                   lambda j, i, br: (layer, 0, pl.multiple_of(
                                       gate_base + br * D_MODEL + j * tn, LANES)), br=br))
                  for br in range(N_BRANCH)]
    tail_specs = [pl.BlockSpec((None, D_MODEL, LANES),
                               functools.partial(
                                   lambda j, i, br: (layer, 0, (gate_base + br * D_MODEL) // LANES
                                                     + (j + 1) * (tn // LANES)), br=br))
                  for br in range(N_BRANCH)]
    bias_specs = [pl.BlockSpec((1, tn), functools.partial(lambda j, i, br: (0, br * nn + j), br=br))
                  for br in range(N_BRANCH)]
    br_spec = pl.BlockSpec((None, ATT_Q_W, tn), lambda j, i: (layer, 0, j))
    return pl.pallas_call(
        _merge_body,
        out_shape=jax.ShapeDtypeStruct((t, D_MODEL), BF16),
        grid=(nn, t // tm),
        in_specs=[pl.BlockSpec((tm, D_MODEL), row), y_spec, y_spec, y_spec,
                  *gate_specs, *tail_specs, *bias_specs, br_spec, br_spec, br_spec],
        out_specs=pl.BlockSpec((tm, tn), lambda j, i: (i, j)),
        scratch_shapes=[pltpu.VMEM((N_BRANCH, D_MODEL, tn), BF16)],
        compiler_params=_cparams("arbitrary", "arbitrary"),
        name="gated_merge",
    )(h, y_att, y_pool, y_gla, w_in, w_in, w_in, w_in, w_in, w_in, b_gate, b_gate, b_gate,
      w_br_att, w_br_pool, w_br_gla)


def _outproj_body(m_ref, w_ref, x_ref, o_ref):
    o_ref[...] = x_ref[...] + jnp.dot(m_ref[...], w_ref[...], preferred_element_type=F32)


def _outproj(merged, w_out, x, layer):
    t = x.shape[0]
    tm, tn = OUT_TM, OUT_TN
    return pl.pallas_call(
        _outproj_body,
        out_shape=jax.ShapeDtypeStruct((t, D_MODEL), F32),
        grid=(t // tm, D_MODEL // tn),
        in_specs=[
            pl.BlockSpec((tm, D_MODEL), lambda i, j: (i, 0)),
            pl.BlockSpec((None, D_MODEL, tn), lambda i, j: (layer, 0, j),
                         pipeline_mode=pl.Buffered(1) if tn == D_MODEL else None),
            pl.BlockSpec((tm, tn), lambda i, j: (i, j)),
        ],
        out_specs=pl.BlockSpec((tm, tn), lambda i, j: (i, j)),
        compiler_params=_cparams("parallel", "arbitrary"),
        name="outproj",
    )(merged, w_out, x)


def _prep_mixer_weights(w_in, w_pool, w_gla_a2, w_br_att, w_br_pool, w_br_gla, w_out):
    depth = w_in.shape[0]
    w_in_bf = w_in.astype(BF16)
    w_a2 = jnp.pad(w_gla_a2.astype(BF16), ((0, 0), (0, LANES - GLA_LOWRANK), (0, 0)))
    groups = len(POOL_WINDOWS)
    eye = jnp.eye(groups, dtype=BF16)
    w_bd = (w_pool.astype(BF16)[:, :, :, None, :] * eye[None, :, None, :, None]).reshape(
        depth, POOL_WIDTH, POOL_WIDTH)
    return dict(w_in=w_in_bf, w_a2=w_a2, w_bd=w_bd,
                w_br_att=w_br_att.astype(BF16), w_br_pool=w_br_pool.astype(BF16),
                w_br_gla=w_br_gla.astype(BF16), w_out=w_out.astype(BF16))


def _mixer(x, cos_t, sin_t, batch, seq, layer, mw, norm_mix, b_gate, att_sinks, pool_scale,
           b_gla_a, gla_norm):
    proj, gk, h, y_pool, y_att = _inproj(x, norm_mix[None, :], mw["w_in"], mw["w_a2"],
                                         b_gla_a[None, :], cos_t, sin_t, mw["w_bd"],
                                         pool_scale[None, :], att_sinks, seq, layer)
    y_gla = _gla(proj, gk, gla_norm[None, :], batch, seq)
    merged = _merge(h, y_att, y_pool, y_gla, mw["w_in"], b_gate[None, :],
                    mw["w_br_att"], mw["w_br_pool"], mw["w_br_gla"], layer)
    return _outproj(merged, mw["w_out"], x, layer)


def kernel(x, positions, norm_ffn1, ffn1_wi, ffn1_wo, norm_mix, w_in, b_gate, att_sinks, w_pool, pool_scale, w_gla_a2, b_gla_a, gla_norm, w_br_att, w_br_pool, w_br_gla, w_out, norm_ffn2, ffn2_wi, ffn2_wo, norm_final):
    batch, seq, d = x.shape
    depth = norm_ffn1.shape[0]
    assert d == D_MODEL and seq % max(INPROJ_TM, GLA_TS, ATT_BLOCK) == 0
    t = batch * seq
    assert t % max(FFN_TM, MERGE_TM, OUT_TM, ROPE_TM, INPROJ_TM) == 0
    xt = x.reshape(t, d)
    cos_t, sin_t = _rope_tables(positions)
    gf = norm_final[None, :]
    mw = _prep_mixer_weights(w_in, w_pool, w_gla_a2, w_br_att, w_br_pool, w_br_gla, w_out)
    for l in range(depth):
        xt = _ffn(xt, norm_ffn1[l][None, :], ffn1_wi, ffn1_wo, gf, l, False)
        xt = _mixer(xt, cos_t, sin_t, batch, seq, l, mw, norm_mix[l], b_gate[l], att_sinks[l],
                    pool_scale[l], b_gla_a[l], gla_norm[l])
        xt = _ffn(xt, norm_ffn2[l][None, :], ffn2_wi, ffn2_wo, gf, l, l == depth - 1)
    return xt.reshape(batch, seq, d)
```

```python
import functools

import numpy as np
import jax
import jax.numpy as jnp
from jax import lax
from jax.experimental import pallas as pl
from jax.experimental.pallas import tpu as pltpu

F32 = jnp.float32
BF16 = jnp.bfloat16

D_MODEL = 2048
D_FF = 5632
ATT_HEADS = 12
ATT_KV_HEADS = 4
HEAD_DIM = 64
ATT_BLOCK = 128
ROPE_THETA = 10000.0
ATT_Q_W = ATT_HEADS * HEAD_DIM
ATT_KV_W = ATT_KV_HEADS * HEAD_DIM
POOL_WINDOWS = (2, 4, 8, 16)
POOL_GROUP_DIM = 192
POOL_WIDTH = 768
GLA_HEADS = 4
GLA_DK = 96
GLA_DV = 192
GLA_QK_W = GLA_HEADS * GLA_DK
GLA_V_W = GLA_HEADS * GLA_DV
GLA_LOWRANK = 16
GLA_TAU = 16.0
GLA_CHUNK = 64
N_BRANCH = 3
EPS = 1e-6

_IN_QA, _IN_KA, _IN_VA, _IN_PU = 0, 768, 1024, 1280
_IN_QG, _IN_KG, _IN_VG, _IN_OG = 2048, 2432, 2816, 3584
_IN_LR = 4352
_IN_GATE = 4368

PROJ_SRC_W = _IN_LR
PROJ_W = 2 * GLA_QK_W + 2 * GLA_V_W
_BLK_GQK, _BLK_GV, _BLK_GO = 0, 1, 2
_PROJ_SEGMENTS = ((0, _IN_QG, 2 * GLA_QK_W), (2 * GLA_QK_W, _IN_VG, 2 * GLA_V_W))
W_IN_COLS = _IN_GATE + N_BRANCH * D_MODEL
GATE_LEAD = _IN_GATE % 128

LANES = 128
POOL_HALO = 16
VMEM_LIMIT = 58 * 1024 * 1024

FFN_TM, FFN_TF = 1024, 256
FFN_UNROLL = 4
INPROJ_TM = 512
GLA_TS = 256
GLA_PAIR = 4
MERGE_TM, MERGE_TN = 1024, 512
OUT_TM, OUT_TN = 512, 2048
ROPE_TM = 1024


def _cparams(*sem):
    return pltpu.CompilerParams(dimension_semantics=sem, vmem_limit_bytes=VMEM_LIMIT)


def _rms(x, g):
    return x * lax.rsqrt(jnp.mean(x * x, axis=-1, keepdims=True) + EPS) * g


def _ffn_body(x_ref, g_ref, gf_ref, wi_hbm, wo_hbm, o_ref, h_ref, u_ref, wa_buf, wb_buf, wo_buf, sem,
              *, layer, n_tiles, final_norm):
    tf = FFN_TF
    nch = D_FF // tf
    i = pl.program_id(0)

    def up_copies(c, slot):
        col = pl.multiple_of(c * tf, tf)
        return (pltpu.make_async_copy(wi_hbm.at[layer, :, pl.ds(col, tf)], wa_buf.at[slot], sem.at[0, slot]),
                pltpu.make_async_copy(wi_hbm.at[layer, :, pl.ds(D_FF + col, tf)], wb_buf.at[slot],
                                      sem.at[1, slot]))

    def down_copy(c, slot):
        row = pl.multiple_of(c * tf, tf)
        return pltpu.make_async_copy(wo_hbm.at[layer, pl.ds(row, tf), :], wo_buf.at[slot], sem.at[2, slot])

    def up(slot):
        h = h_ref[...]
        a = jnp.dot(h, wa_buf[slot].astype(BF16), preferred_element_type=F32)
        b = jnp.dot(h, wb_buf[slot].astype(BF16), preferred_element_type=F32)
        u_ref[slot] = (a * jax.nn.sigmoid(a) * (0.5 * b)).astype(BF16)

    def down(slot):
        o_ref[...] += jnp.dot(u_ref[slot], wo_buf[slot].astype(BF16), preferred_element_type=F32)

    def start_up(c, slot):
        ca, cb = up_copies(c, slot)
        ca.start(priority=1)
        cb.start(priority=1)

    def stage(c, slot):
        for cp in up_copies(c, slot):
            cp.wait()
        down_copy(c - 1, 1 - slot).wait()
        down_copy(c, slot).start(priority=1)
        down(1 - slot)
        up(slot)

    @pl.when(i == 0)
    def _():
        start_up(0, 0)

    x = x_ref[...]
    h_ref[...] = _rms(x, g_ref[...]).astype(BF16)
    o_ref[...] = x
    for cp in up_copies(0, 0):
        cp.wait()
    start_up(1, 1)
    down_copy(0, 0).start(priority=1)
    up(0)

    def stages(k, carry):
        c0 = FFN_UNROLL * k + 1
        for s in range(FFN_UNROLL):
            slot = (1 + s) % 2
            start_up(c0 + s + 1, 1 - slot)
            stage(c0 + s, slot)
        return carry

    lax.fori_loop(0, (nch - 2) // FFN_UNROLL, stages, 0)

    @pl.when(i + 1 < n_tiles)
    def _():
        start_up(0, 0)

    stage(nch - 1, 1)
    down_copy(nch - 1, 1).wait()
    down(1)
    if final_norm:
        o_ref[...] = _rms(o_ref[...], gf_ref[...])


def _ffn(x, g, wi, wo, gf, layer, final_norm):
    t = x.shape[0]
    tm, tf = FFN_TM, FFN_TF
    assert FFN_UNROLL % 2 == 0 and (D_FF // tf - 2) % FFN_UNROLL == 0
    n_tiles = t // tm
    return pl.pallas_call(
        functools.partial(_ffn_body, layer=layer, n_tiles=n_tiles, final_norm=final_norm),
        out_shape=jax.ShapeDtypeStruct((t, D_MODEL), F32),
        grid=(n_tiles,),
        in_specs=[
            pl.BlockSpec((tm, D_MODEL), lambda i: (i, 0)),
            pl.BlockSpec((1, D_MODEL), lambda i: (0, 0)),
            pl.BlockSpec((1, D_MODEL), lambda i: (0, 0)),
            pl.BlockSpec(memory_space=pl.ANY),
            pl.BlockSpec(memory_space=pl.ANY),
        ],
        out_specs=pl.BlockSpec((tm, D_MODEL), lambda i: (i, 0)),
        scratch_shapes=[
            pltpu.VMEM((tm, D_MODEL), BF16),
            pltpu.VMEM((2, tm, tf), BF16),
            pltpu.VMEM((2, D_MODEL, tf), F32),
            pltpu.VMEM((2, D_MODEL, tf), F32),
            pltpu.VMEM((2, tf, D_MODEL), F32),
            pltpu.SemaphoreType.DMA((3, 2)),
        ],
        compiler_params=_cparams("arbitrary"),
        name="ffn_final" if final_norm else "ffn",
    )(x, g, gf, wi, wo)


def _rope_table_body(pos_ref, freq_ref, sign_ref, cos_ref, sin_ref):
    ang = pos_ref[...].astype(F32) * freq_ref[...]
    cos_ref[...] = jnp.cos(ang)
    sin_ref[...] = jnp.sin(ang) * sign_ref[...]


def _rope_tables(positions):
    t = positions.size
    half = HEAD_DIM // 2
    inv_freq = ROPE_THETA ** (-jnp.arange(half, dtype=F32) / half)
    freq = jnp.tile(inv_freq, LANES // half)[None, :]
    sign = jnp.tile(jnp.concatenate([-jnp.ones((half,), F32), jnp.ones((half,), F32)]),
                    LANES // HEAD_DIM)[None, :]
    tm = ROPE_TM
    return pl.pallas_call(
        _rope_table_body,
        out_shape=(jax.ShapeDtypeStruct((t, LANES), F32), jax.ShapeDtypeStruct((t, LANES), F32)),
        grid=(t // tm,),
        in_specs=[
            pl.BlockSpec((tm, 1), lambda i: (i, 0)),
            pl.BlockSpec((1, LANES), lambda i: (0, 0)),
            pl.BlockSpec((1, LANES), lambda i: (0, 0)),
        ],
        out_specs=(pl.BlockSpec((tm, LANES), lambda i: (i, 0)),
                   pl.BlockSpec((tm, LANES), lambda i: (i, 0))),
        compiler_params=_cparams("parallel"),
        name="rope_tables",
    )(positions.reshape(t, 1), freq, sign)


def _swap_halves(x):
    lane = lax.broadcasted_iota(jnp.int32, x.shape, 1)
    first = (lane % HEAD_DIM) < (HEAD_DIM // 2)
    return jnp.where(first, pltpu.roll(x, LANES - HEAD_DIM // 2, 1), pltpu.roll(x, HEAD_DIM // 2, 1))


def _rope(x, cos, sin):
    cols = []
    for c in range(x.shape[1] // LANES):
        xc = x[:, c * LANES:(c + 1) * LANES]
        cols.append(xc * cos + _swap_halves(xc) * sin)
    return jnp.concatenate(cols, axis=1)


def _pool_mix(u, halo, tok0):
    ts = u.shape[0]
    a1 = jnp.concatenate([halo, u], axis=0)
    length = ts + POOL_HALO
    a2 = a1[1:] + a1[:length - 1]
    a4 = a2[2:] + a2[:length - 3]
    a8 = a4[4:] + a4[:length - 7]
    a16 = a8[8:] + a8[:length - 15]
    sums = {2: a2[POOL_HALO - 1:], 4: a4[POOL_HALO - 3:], 8: a8[POOL_HALO - 7:], 16: a16[POOL_HALO - 15:]}

    tok = tok0 + lax.broadcasted_iota(jnp.int32, (ts, 1), 0)
    chan = lax.broadcasted_iota(jnp.int32, (1, POOL_WIDTH), 1)
    d = jnp.zeros((ts, POOL_WIDTH), F32)
    for gi, w in enumerate(POOL_WINDOWS):
        inv_cnt = 1.0 / jnp.minimum(tok + 1, w).astype(F32)
        in_group = (chan >= gi * POOL_GROUP_DIM) & (chan < (gi + 1) * POOL_GROUP_DIM)
        d = jnp.where(in_group, sums[w] * inv_cnt, d)
    return d - u


def _inproj_body(sink_ref, x_ref, g_ref, w_ref, wlr_ref, wa2_ref, ba_ref, cos_ref, sin_ref, wbd_ref,
                 ps_ref, proj_ref, gk_ref, h_ref, ypool_ref, yatt_ref, halo_ref, kprev_ref, vprev_ref,
                 *, tiles_per_seq):
    tm = x_ref.shape[0]
    blk = ATT_BLOCK
    s_idx = pl.program_id(0) % tiles_per_seq

    @pl.when(s_idx == 0)
    def _():
        halo_ref[...] = jnp.zeros_like(halo_ref)
        kprev_ref[...] = jnp.zeros_like(kprev_ref)
        vprev_ref[...] = jnp.zeros_like(vprev_ref)

    hs, pus = [], []
    for r in range(tm // blk):
        hr = _rms(x_ref[r * blk:(r + 1) * blk, :], g_ref[...]).astype(BF16)
        hs.append(hr)
        pus.append(jnp.dot(hr, w_ref[:, _IN_PU:_IN_PU + POOL_WIDTH], preferred_element_type=F32))
    h = jnp.concatenate(hs, axis=0)
    h_ref[...] = h
    pu = jnp.concatenate(pus, axis=0)
    d = _pool_mix(pu, halo_ref[...], s_idx * tm).astype(BF16)
    halo_ref[...] = pu[tm - POOL_HALO:]

    lr = jnp.dot(h, wlr_ref[...], preferred_element_type=F32).astype(BF16)

    cos, sin = cos_ref[...], sin_ref[...]
    scale = HEAD_DIM ** -0.5
    q = _rope(jnp.dot(h, w_ref[:, _IN_QA:_IN_QA + ATT_Q_W], preferred_element_type=F32),
              cos * scale, sin * scale).astype(BF16)
    k = _rope(jnp.dot(h, w_ref[:, _IN_KA:_IN_KA + ATT_KV_W], preferred_element_type=F32),
              cos, sin).astype(BF16)
    v = jnp.dot(h, w_ref[:, _IN_VA:_IN_VA + ATT_KV_W], preferred_element_type=F32).astype(BF16)
    k = jnp.concatenate([kprev_ref[...], k], axis=0)
    v = jnp.concatenate([vprev_ref[...], v], axis=0)
    kprev_ref[...] = k[tm:]
    vprev_ref[...] = v[tm:]

    z = jnp.dot(lr, wa2_ref[...], preferred_element_type=F32) + ba_ref[...]
    gk_ref[...] = (jnp.minimum(z, 0.0) - jnp.log1p(jnp.exp(-jnp.abs(z)))) * (1.0 / GLA_TAU)
    y = jnp.dot(d, wbd_ref[...], preferred_element_type=F32)
    ypool_ref[...] = (y * ps_ref[...]).astype(ypool_ref.dtype)

    nblk = tm // blk
    mxu_w = 2 * LANES
    bounds = [mxu_w * ((PROJ_W // mxu_w) * r // nblk) for r in range(nblk)] + [PROJ_W]
    for r in range(nblk):
        has_prev = (s_idx > 0) if r == 0 else True
        o = _attend_block(q[r * blk:(r + 1) * blk], k[r * blk:(r + 2) * blk], v[r * blk:(r + 2) * blk],
                          has_prev, sink_ref)
        yatt_ref[r * blk:(r + 1) * blk, :] = o.astype(yatt_ref.dtype)
        for dst, src, width in _PROJ_SEGMENTS:
            lo, hi = max(dst, bounds[r]), min(dst + width, bounds[r + 1])
            if lo < hi:
                p = jnp.dot(h, w_ref[:, src + lo - dst:src + hi - dst], preferred_element_type=F32)
                proj_ref[:, lo:hi] = p.astype(proj_ref.dtype)


def _inproj(x, g, w_in, w_a2, b_a, cos_t, sin_t, w_bd, pool_scale, sinks, seq, layer):
    t = x.shape[0]
    tm = INPROJ_TM
    assert seq % tm == 0 and tm % ATT_BLOCK == 0
    const = lambda i: (0, 0)
    lconst = lambda i: (layer, 0, 0)
    rows = lambda i: (i, 0)
    return pl.pallas_call(
        functools.partial(_inproj_body, tiles_per_seq=seq // tm),
        out_shape=(jax.ShapeDtypeStruct((t, PROJ_W), BF16),
                   jax.ShapeDtypeStruct((t, GLA_QK_W), F32),
                   jax.ShapeDtypeStruct((t, D_MODEL), BF16),
                   jax.ShapeDtypeStruct((t, POOL_WIDTH), BF16),
                   jax.ShapeDtypeStruct((t, ATT_Q_W), BF16)),
        grid=(t // tm,),
        in_specs=[
            pl.BlockSpec(memory_space=pltpu.SMEM),
            pl.BlockSpec((tm, D_MODEL), rows),
            pl.BlockSpec((1, D_MODEL), const),
            pl.BlockSpec((None, D_MODEL, PROJ_SRC_W), lconst, pipeline_mode=pl.Buffered(1)),
            pl.BlockSpec((None, D_MODEL, LANES), lambda i: (layer, 0, _IN_LR // LANES)),
            pl.BlockSpec((None, LANES, GLA_QK_W), lconst),
            pl.BlockSpec((1, GLA_QK_W), const),
            pl.BlockSpec((tm, LANES), rows),
            pl.BlockSpec((tm, LANES), rows),
            pl.BlockSpec((None, POOL_WIDTH, POOL_WIDTH), lconst),
            pl.BlockSpec((1, POOL_WIDTH), const),
        ],
        out_specs=(pl.BlockSpec((tm, PROJ_W), rows),
                   pl.BlockSpec((tm, GLA_QK_W), rows),
                   pl.BlockSpec((tm, D_MODEL), rows),
                   pl.BlockSpec((tm, POOL_WIDTH), rows),
                   pl.BlockSpec((tm, ATT_Q_W), rows)),
        scratch_shapes=[pltpu.VMEM((POOL_HALO, POOL_WIDTH), F32),
                        pltpu.VMEM((ATT_BLOCK, ATT_KV_W), BF16),
                        pltpu.VMEM((ATT_BLOCK, ATT_KV_W), BF16)],
        compiler_params=_cparams("arbitrary"),
        name="inproj",
    )(sinks, x, g, w_in, w_in, w_a2, b_a, cos_t, sin_t, w_bd, pool_scale)


def _both_halves(x, col, half):
    xc = x[:, col * LANES:(col + 1) * LANES].astype(F32)
    sw = pltpu.roll(xc, LANES // 2, 1)
    first = lax.broadcasted_iota(jnp.int32, xc.shape, 1) < LANES // 2
    dup = jnp.where(first, xc, sw) if half == 0 else jnp.where(first, sw, xc)
    return dup.astype(BF16)


def _attend_block(q, k, v, has_prev, sink_ref):
    blk = ATT_BLOCK
    group = ATT_HEADS // ATT_KV_HEADS

    qi = lax.broadcasted_iota(jnp.int32, (blk, 2 * blk), 0)
    si = lax.broadcasted_iota(jnp.int32, (blk, 2 * blk), 1)
    rel = blk + qi - si
    mask = (rel >= 0) & (rel < blk) & ((si >= blk) | has_prev)
    bias = jnp.where(mask, 0.0, -1e30)
    first = lax.broadcasted_iota(jnp.int32, (blk, LANES), 1) < LANES // 2
    zero = jnp.zeros((blk, LANES), BF16)

    heads = [None] * ATT_HEADS
    for g in range(ATT_KV_HEADS):
        kk = _both_halves(k, g // 2, g % 2)
        vv = _both_halves(v, g // 2, g % 2)
        hs = range(g * group, (g + 1) * group)
        qm = jnp.concatenate(
            [jnp.where(first if hd % 2 == 0 else ~first, q[:, (hd // 2) * LANES:(hd // 2 + 1) * LANES], zero)
             for hd in hs], axis=0)
        s3 = lax.dot_general(qm, kk, (((1,), (1,)), ((), ())), preferred_element_type=F32)
        es, denoms = [], []
        for r, hd in enumerate(hs):
            s = s3[r * blk:(r + 1) * blk] + bias
            sink = sink_ref[hd]
            m = jnp.maximum(jnp.max(s, axis=1, keepdims=True), sink)
            e = jnp.exp(s - m)
            denoms.append(jnp.sum(e, axis=1, keepdims=True) + jnp.exp(sink - m))
            es.append(e.astype(BF16))
        o3 = jnp.dot(jnp.concatenate(es, axis=0), vv, preferred_element_type=F32)
        for r, hd in enumerate(hs):
            heads[hd] = o3[r * blk:(r + 1) * blk] / denoms[r]
    cols = [jnp.where(first, heads[2 * c], heads[2 * c + 1]) for c in range(ATT_HEADS // 2)]
    return jnp.concatenate(cols, axis=1)


def _gla_body(qk_ref, v_ref, og_ref, gk_ref, norm_ref, tril_ref, bd_ref, o_ref, st_ref, oacc_ref):
    s_idx = pl.program_id(1)
    npair, ts = qk_ref.shape[0], qk_ref.shape[1]
    c = GLA_CHUNK

    @pl.when(s_idx == 0)
    def _():
        st_ref[...] = jnp.zeros_like(st_ref)

    tril = tril_ref[...]
    prep = []
    for p in range(npair):
        qk = qk_ref[p].astype(F32)
        q = qk[:, :GLA_QK_W] * (GLA_DK ** -0.5)
        k = qk[:, GLA_QK_W:]
        gk = gk_ref[p]
        g1 = gk.astype(BF16)
        r1 = gk - g1.astype(F32)
        g2 = r1.astype(BF16)
        g3 = (r1 - g2.astype(F32)).astype(BF16)
        b = (jnp.dot(tril, g1, preferred_element_type=F32) + jnp.dot(tril, g2, preferred_element_type=F32)
             + jnp.dot(tril, g3, preferred_element_type=F32))
        prep.append(((q * jnp.exp(b)).astype(BF16), (k * jnp.exp(-b)).astype(BF16), k, b, v_ref[p]))

    kcol = lax.broadcasted_iota(jnp.int32, (1, GLA_QK_W), 1)
    vcol = lax.broadcasted_iota(jnp.int32, (1, GLA_V_W), 1)
    ri = lax.broadcasted_iota(jnp.int32, (c, c), 0)
    ci = lax.broadcasted_iota(jnp.int32, (c, c), 1)
    causal = ri >= ci
    bd = bd_ref[...]

    units = [(n, p) for n in range(ts // c) for p in range(npair)]
    scores = {}
    for n, p in units:
        rows = slice(n * c, (n + 1) * c)
        q_n = prep[p][0][rows]
        q4 = jnp.concatenate(
            [jnp.where((kcol >= h * GLA_DK) & (kcol < (h + 1) * GLA_DK), q_n, jnp.zeros_like(q_n))
             for h in range(GLA_HEADS)], axis=0)
        scores[n, p] = lax.dot_general(q4, prep[p][1][rows], (((1,), (1,)), ((), ())),
                                       preferred_element_type=F32)
    for n, p in units:
        rows = slice(n * c, (n + 1) * c)
        v_n = prep[p][4][rows]
        o_n = jnp.zeros((c, GLA_V_W), F32)
        for h in range(GLA_HEADS):
            a_h = jnp.where(causal, scores[n, p][h * c:(h + 1) * c], 0.0).astype(BF16)
            o_h = jnp.dot(a_h, v_n, preferred_element_type=F32)
            o_n = jnp.where((vcol >= h * GLA_DV) & (vcol < (h + 1) * GLA_DV), o_h, o_n)
        oacc_ref[p, rows, :] = o_n
    for n, p in units:
        rows = slice(n * c, (n + 1) * c)
        q_t, _, k, b, v = prep[p]
        b_n = b[rows]
        b_last = b_n[c - 1:c, :]
        k_s = (k[rows] * jnp.exp(b_last - b_n)).astype(BF16)
        st = st_ref[p]
        oacc_ref[p, rows, :] += lax.dot_general(q_t[rows], st.astype(BF16), (((1,), (1,)), ((), ())),
                                                preferred_element_type=F32)
        kv = lax.dot_general(v[rows], k_s, (((0,), (0,)), ((), ())), preferred_element_type=F32)
        st_ref[p] = st * jnp.exp(b_last) + kv * bd

    for p in range(npair):
        o = oacc_ref[p]
        o2 = o * o
        inv = jnp.zeros_like(o)
        for h in range(GLA_HEADS):
            in_head = (vcol >= h * GLA_DV) & (vcol < (h + 1) * GLA_DV)
            ms = jnp.sum(jnp.where(in_head, o2, 0.0), axis=1, keepdims=True) * (1.0 / GLA_DV)
            inv = jnp.where(in_head, lax.rsqrt(ms + EPS), inv)
        og = og_ref[p].astype(F32)
        o_ref[p] = (o * inv * norm_ref[...] * (og * jax.nn.sigmoid(og))).astype(o_ref.dtype)


def _gla(proj, gk, norm, batch, seq):
    t = proj.shape[0]
    ts = GLA_TS
    ns = seq // ts
    pair = GLA_PAIR
    assert batch % pair == 0
    r = np.arange(ts)
    tril = ((r[:, None] // GLA_CHUNK == r[None, :] // GLA_CHUNK) & (r[:, None] >= r[None, :]))
    tril = jnp.asarray(tril, BF16)
    bd = (np.arange(GLA_V_W)[:, None] // GLA_DV) == (np.arange(GLA_QK_W)[None, :] // GLA_DK)
    bd = jnp.asarray(bd, F32)
    proj3 = proj.reshape(batch, seq, PROJ_W)
    out = pl.pallas_call(
        _gla_body,
        out_shape=jax.ShapeDtypeStruct((batch, seq, GLA_V_W), BF16),
        grid=(batch // pair, ns),
        in_specs=[
            pl.BlockSpec((pair, ts, 2 * GLA_QK_W), lambda b, s: (b, s, _BLK_GQK)),
            pl.BlockSpec((pair, ts, GLA_V_W), lambda b, s: (b, s, _BLK_GV)),
            pl.BlockSpec((pair, ts, GLA_V_W), lambda b, s: (b, s, _BLK_GO)),
            pl.BlockSpec((pair, ts, GLA_QK_W), lambda b, s: (b, s, 0)),
            pl.BlockSpec((1, GLA_V_W), lambda b, s: (0, 0)),
            pl.BlockSpec((ts, ts), lambda b, s: (0, 0)),
            pl.BlockSpec((GLA_V_W, GLA_QK_W), lambda b, s: (0, 0)),
        ],
        out_specs=pl.BlockSpec((pair, ts, GLA_V_W), lambda b, s: (b, s, 0)),
        scratch_shapes=[pltpu.VMEM((pair, GLA_V_W, GLA_QK_W), F32),
                        pltpu.VMEM((pair, ts, GLA_V_W), F32)],
        compiler_params=_cparams("parallel", "arbitrary"),
        name="gla",
    )(proj3, proj3, proj3, gk.reshape(batch, seq, GLA_QK_W), norm, tril, bd)
    return out.reshape(t, GLA_V_W)


def _merge_body(h_ref, ya_ref, yp_ref, yg_ref, wga_ref, wgp_ref, wgg_ref, tga_ref, tgp_ref, tgg_ref,
                bga_ref, bgp_ref, bgg_ref, wa_ref, wp_ref, wg_ref, o_ref, wgate_ref):
    tn = o_ref.shape[1]

    @pl.when(pl.program_id(1) == 0)
    def _():
        for br, (win_ref, tail_ref) in enumerate(((wga_ref, tga_ref), (wgp_ref, tgp_ref),
                                                  (wgg_ref, tgg_ref))):
            win = jnp.concatenate([win_ref[...], tail_ref[...]], axis=1).astype(F32)
            wgate_ref[br] = win[:, GATE_LEAD:GATE_LEAD + tn].astype(BF16)

    h = h_ref[...]
    acc = None
    for br, (y_ref, bgate_ref, wbr_ref) in enumerate(((ya_ref, bga_ref, wa_ref),
                                                     (yp_ref, bgp_ref, wp_ref),
                                                     (yg_ref, bgg_ref, wg_ref))):
        gate = jax.nn.sigmoid(jnp.dot(h, wgate_ref[br], preferred_element_type=F32) + bgate_ref[...])
        term = gate * jnp.dot(y_ref[...], wbr_ref[...], preferred_element_type=F32)
        acc = term if acc is None else acc + term
    o_ref[...] = acc.astype(o_ref.dtype)


def _merge(h, y_att, y_pool, y_gla, w_in, b_gate, w_br_att, w_br_pool, w_br_gla, layer):
    t = h.shape[0]
    tm, tn = MERGE_TM, MERGE_TN
    nn = D_MODEL // tn
    row = lambda j, i: (i, 0)
    y_spec = pl.BlockSpec((tm, ATT_Q_W), row)
    gate_base = _IN_GATE - GATE_LEAD
    gate_specs = [pl.BlockSpec((pl.Squeezed(), pl.Element(D_MODEL), pl.Element(tn)),
                               functools.partial(
                                   lambda j, i, br: (layer, 0, pl.multiple_of(
                                       gate_base + br * D_MODEL + j * tn, LANES)), br=br))
                  for br in range(N_BRANCH)]
    tail_specs = [pl.BlockSpec((None, D_MODEL, LANES),
                               functools.partial(
                                   lambda j, i, br: (layer, 0, (gate_base + br * D_MODEL) // LANES
                                                     + (j + 1) * (tn // LANES)), br=br))
                  for br in range(N_BRANCH)]
    bias_specs = [pl.BlockSpec((1, tn), functools.partial(lambda j, i, br: (0, br * nn + j), br=br))
                  for br in range(N_BRANCH)]
    br_spec = pl.BlockSpec((None, ATT_Q_W, tn), lambda j, i: (layer, 0, j))
    return pl.pallas_call(
        _merge_body,
        out_shape=jax.ShapeDtypeStruct((t, D_MODEL), BF16),
        grid=(nn, t // tm),
        in_specs=[pl.BlockSpec((tm, D_MODEL), row), y_spec, y_spec, y_spec,
                  *gate_specs, *tail_specs, *bias_specs, br_spec, br_spec, br_spec],
        out_specs=pl.BlockSpec((tm, tn), lambda j, i: (i, j)),
        scratch_shapes=[pltpu.VMEM((N_BRANCH, D_MODEL, tn), BF16)],
        compiler_params=_cparams("arbitrary", "arbitrary"),
        name="gated_merge",
    )(h, y_att, y_pool, y_gla, w_in, w_in, w_in, w_in, w_in, w_in, b_gate, b_gate, b_gate,
      w_br_att, w_br_pool, w_br_gla)


def _outproj_body(m_ref, w_ref, x_ref, o_ref):
    o_ref[...] = x_ref[...] + jnp.dot(m_ref[...], w_ref[...], preferred_element_type=F32)


def _outproj(merged, w_out, x, layer):
    t = x.shape[0]
    tm, tn = OUT_TM, OUT_TN
    return pl.pallas_call(
        _outproj_body,
        out_shape=jax.ShapeDtypeStruct((t, D_MODEL), F32),
        grid=(t // tm, D_MODEL // tn),
        in_specs=[
            pl.BlockSpec((tm, D_MODEL), lambda i, j: (i, 0)),
            pl.BlockSpec((None, D_MODEL, tn), lambda i, j: (layer, 0, j),
                         pipeline_mode=pl.Buffered(1) if tn == D_MODEL else None),
            pl.BlockSpec((tm, tn), lambda i, j: (i, j)),
        ],
        out_specs=pl.BlockSpec((tm, tn), lambda i, j: (i, j)),
        compiler_params=_cparams("parallel", "arbitrary"),
        name="outproj",
    )(merged, w_out, x)


def _prep_mixer_weights(w_in, w_pool, w_gla_a2, w_br_att, w_br_pool, w_br_gla, w_out):
    depth = w_in.shape[0]
    w_in_bf = w_in.astype(BF16)
    w_a2 = jnp.pad(w_gla_a2.astype(BF16), ((0, 0), (0, LANES - GLA_LOWRANK), (0, 0)))
    groups = len(POOL_WINDOWS)
    eye = jnp.eye(groups, dtype=BF16)
    w_bd = (w_pool.astype(BF16)[:, :, :, None, :] * eye[None, :, None, :, None]).reshape(
        depth, POOL_WIDTH, POOL_WIDTH)
    return dict(w_in=w_in_bf, w_a2=w_a2, w_bd=w_bd,
                w_br_att=w_br_att.astype(BF16), w_br_pool=w_br_pool.astype(BF16),
                w_br_gla=w_br_gla.astype(BF16), w_out=w_out.astype(BF16))


def _mixer(x, cos_t, sin_t, batch, seq, layer, mw, norm_mix, b_gate, att_sinks, pool_scale,
           b_gla_a, gla_norm):
    proj, gk, h, y_pool, y_att = _inproj(x, norm_mix[None, :], mw["w_in"], mw["w_a2"],
                                         b_gla_a[None, :], cos_t, sin_t, mw["w_bd"],
                                         pool_scale[None, :], att_sinks, seq, layer)
    y_gla = _gla(proj, gk, gla_norm[None, :], batch, seq)
    merged = _merge(h, y_att, y_pool, y_gla, mw["w_in"], b_gate[None, :],
                    mw["w_br_att"], mw["w_br_pool"], mw["w_br_gla"], layer)
    return _outproj(merged, mw["w_out"], x, layer)


def kernel(x, positions, norm_ffn1, ffn1_wi, ffn1_wo, norm_mix, w_in, b_gate, att_sinks, w_pool, pool_scale, w_gla_a2, b_gla_a, gla_norm, w_br_att, w_br_pool, w_br_gla, w_out, norm_ffn2, ffn2_wi, ffn2_wo, norm_final):
    batch, seq, d = x.shape
    depth = norm_ffn1.shape[0]
    assert d == D_MODEL and seq % max(INPROJ_TM, GLA_TS, ATT_BLOCK) == 0
    t = batch * seq
    assert t % max(FFN_TM, MERGE_TM, OUT_TM, ROPE_TM, INPROJ_TM) == 0
    xt = x.reshape(t, d)
    cos_t, sin_t = _rope_tables(positions)
    gf = norm_final[None, :]
    mw = _prep_mixer_weights(w_in, w_pool, w_gla_a2, w_br_att, w_br_pool, w_br_gla, w_out)
    for l in range(depth):
        xt = _ffn(xt, norm_ffn1[l][None, :], ffn1_wi, ffn1_wo, gf, l, False)
        xt = _mixer(xt, cos_t, sin_t, batch, seq, l, mw, norm_mix[l], b_gate[l], att_sinks[l],
                    pool_scale[l], b_gla_a[l], gla_norm[l])
        xt = _ffn(xt, norm_ffn2[l][None, :], ffn2_wi, ffn2_wo, gf, l, l == depth - 1)
    return xt.reshape(batch, seq, d)
```

```python
import functools

import numpy as np
import jax
import jax.numpy as jnp
from jax import lax
from jax.experimental import pallas as pl
from jax.experimental.pallas import tpu as pltpu

F32 = jnp.float32
BF16 = jnp.bfloat16

D_MODEL = 2048
D_FF = 5632
ATT_HEADS = 12
ATT_KV_HEADS = 4
HEAD_DIM = 64
ATT_BLOCK = 128
ROPE_THETA = 10000.0
ATT_Q_W = ATT_HEADS * HEAD_DIM
ATT_KV_W = ATT_KV_HEADS * HEAD_DIM
POOL_WINDOWS = (2, 4, 8, 16)
POOL_GROUP_DIM = 192
POOL_WIDTH = 768
GLA_HEADS = 4
GLA_DK = 96
GLA_DV = 192
GLA_QK_W = GLA_HEADS * GLA_DK
GLA_V_W = GLA_HEADS * GLA_DV
GLA_LOWRANK = 16
GLA_TAU = 16.0
GLA_CHUNK = 64
N_BRANCH = 3
EPS = 1e-6

_IN_QA, _IN_KA, _IN_VA, _IN_PU = 0, 768, 1024, 1280
_IN_QG, _IN_KG, _IN_VG, _IN_OG = 2048, 2432, 2816, 3584
_IN_LR = 4352
_IN_GATE = 4368

PROJ_SRC_W = _IN_LR
PROJ_W = 2 * GLA_QK_W + 2 * GLA_V_W
_BLK_GQK, _BLK_GV, _BLK_GO = 0, 1, 2
_PROJ_SEGMENTS = ((0, _IN_QG, 2 * GLA_QK_W), (2 * GLA_QK_W, _IN_VG, 2 * GLA_V_W))
W_IN_COLS = _IN_GATE + N_BRANCH * D_MODEL
GATE_LEAD = _IN_GATE % 128

LANES = 128
POOL_HALO = 16
VMEM_LIMIT = 58 * 1024 * 1024

FFN_TM, FFN_TF = 1024, 256
FFN_UNROLL = 4
INPROJ_TM = 512
GLA_TS = 256
GLA_PAIR = 4
MERGE_TM, MERGE_TN = 1024, 512
OUT_TM, OUT_TN = 512, 2048
ROPE_TM = 1024


def _cparams(*sem):
    return pltpu.CompilerParams(dimension_semantics=sem, vmem_limit_bytes=VMEM_LIMIT)


def _rms(x, g):
    return x * lax.rsqrt(jnp.mean(x * x, axis=-1, keepdims=True) + EPS) * g


def _ffn_body(x_ref, g_ref, gf_ref, wi_hbm, wo_hbm, o_ref, h_ref, u_ref, wa_buf, wb_buf, wo_buf, sem,
              *, layer, n_tiles, final_norm):
    tf = FFN_TF
    nch = D_FF // tf
    i = pl.program_id(0)

    def up_copies(c, slot):
        col = pl.multiple_of(c * tf, tf)
        return (pltpu.make_async_copy(wi_hbm.at[layer, :, pl.ds(col, tf)], wa_buf.at[slot], sem.at[0, slot]),
                pltpu.make_async_copy(wi_hbm.at[layer, :, pl.ds(D_FF + col, tf)], wb_buf.at[slot],
                                      sem.at[1, slot]))

    def down_copy(c, slot):
        row = pl.multiple_of(c * tf, tf)
        return pltpu.make_async_copy(wo_hbm.at[layer, pl.ds(row, tf), :], wo_buf.at[slot], sem.at[2, slot])

    def up(slot):
        h = h_ref[...]
        a = jnp.dot(h, wa_buf[slot].astype(BF16), preferred_element_type=F32)
        b = jnp.dot(h, wb_buf[slot].astype(BF16), preferred_element_type=F32)
        u_ref[slot] = (a * jax.nn.sigmoid(a) * (0.5 * b)).astype(BF16)

    def down(slot):
        o_ref[...] += jnp.dot(u_ref[slot], wo_buf[slot].astype(BF16), preferred_element_type=F32)

    def start_up(c, slot):
        ca, cb = up_copies(c, slot)
        ca.start(priority=1)
        cb.start(priority=1)

    def stage(c, slot):
        for cp in up_copies(c, slot):
            cp.wait()
        down_copy(c - 1, 1 - slot).wait()
        down_copy(c, slot).start(priority=1)
        down(1 - slot)
        up(slot)

    @pl.when(i == 0)
    def _():
        start_up(0, 0)

    x = x_ref[...]
    h_ref[...] = _rms(x, g_ref[...]).astype(BF16)
    o_ref[...] = x
    for cp in up_copies(0, 0):
        cp.wait()
    start_up(1, 1)
    down_copy(0, 0).start(priority=1)
    up(0)

    def stages(k, carry):
        c0 = FFN_UNROLL * k + 1
        for s in range(FFN_UNROLL):
            slot = (1 + s) % 2
            start_up(c0 + s + 1, 1 - slot)
            stage(c0 + s, slot)
        return carry

    lax.fori_loop(0, (nch - 2) // FFN_UNROLL, stages, 0)

    @pl.when(i + 1 < n_tiles)
    def _():
        start_up(0, 0)

    stage(nch - 1, 1)
    down_copy(nch - 1, 1).wait()
    down(1)
    if final_norm:
        o_ref[...] = _rms(o_ref[...], gf_ref[...])


def _ffn(x, g, wi, wo, gf, layer, final_norm):
    t = x.shape[0]
    tm, tf = FFN_TM, FFN_TF
    assert FFN_UNROLL % 2 == 0 and (D_FF // tf - 2) % FFN_UNROLL == 0
    n_tiles = t // tm
    return pl.pallas_call(
        functools.partial(_ffn_body, layer=layer, n_tiles=n_tiles, final_norm=final_norm),
        out_shape=jax.ShapeDtypeStruct((t, D_MODEL), F32),
        grid=(n_tiles,),
        in_specs=[
            pl.BlockSpec((tm, D_MODEL), lambda i: (i, 0)),
            pl.BlockSpec((1, D_MODEL), lambda i: (0, 0)),
            pl.BlockSpec((1, D_MODEL), lambda i: (0, 0)),
            pl.BlockSpec(memory_space=pl.ANY),
            pl.BlockSpec(memory_space=pl.ANY),
        ],
        out_specs=pl.BlockSpec((tm, D_MODEL), lambda i: (i, 0)),
        scratch_shapes=[
            pltpu.VMEM((tm, D_MODEL), BF16),
            pltpu.VMEM((2, tm, tf), BF16),
            pltpu.VMEM((2, D_MODEL, tf), F32),
            pltpu.VMEM((2, D_MODEL, tf), F32),
            pltpu.VMEM((2, tf, D_MODEL), F32),
            pltpu.SemaphoreType.DMA((3, 2)),
        ],
        compiler_params=_cparams("arbitrary"),
        name="ffn_final" if final_norm else "ffn",
    )(x, g, gf, wi, wo)


def _rope_table_body(pos_ref, freq_ref, sign_ref, cos_ref, sin_ref):
    ang = pos_ref[...].astype(F32) * freq_ref[...]
    cos_ref[...] = jnp.cos(ang)
    sin_ref[...] = jnp.sin(ang) * sign_ref[...]


def _rope_tables(positions):
    t = positions.size
    half = HEAD_DIM // 2
    inv_freq = ROPE_THETA ** (-jnp.arange(half, dtype=F32) / half)
    freq = jnp.tile(inv_freq, LANES // half)[None, :]
    sign = jnp.tile(jnp.concatenate([-jnp.ones((half,), F32), jnp.ones((half,), F32)]),
                    LANES // HEAD_DIM)[None, :]
    tm = ROPE_TM
    return pl.pallas_call(
        _rope_table_body,
        out_shape=(jax.ShapeDtypeStruct((t, LANES), F32), jax.ShapeDtypeStruct((t, LANES), F32)),
        grid=(t // tm,),
        in_specs=[
            pl.BlockSpec((tm, 1), lambda i: (i, 0)),
            pl.BlockSpec((1, LANES), lambda i: (0, 0)),
            pl.BlockSpec((1, LANES), lambda i: (0, 0)),
        ],
        out_specs=(pl.BlockSpec((tm, LANES), lambda i: (i, 0)),
                   pl.BlockSpec((tm, LANES), lambda i: (i, 0))),
        compiler_params=_cparams("parallel"),
        name="rope_tables",
    )(positions.reshape(t, 1), freq, sign)


def _swap_halves(x):
    lane = lax.broadcasted_iota(jnp.int32, x.shape, 1)
    first = (lane % HEAD_DIM) < (HEAD_DIM // 2)
    return jnp.where(first, pltpu.roll(x, LANES - HEAD_DIM // 2, 1), pltpu.roll(x, HEAD_DIM // 2, 1))


def _rope(x, cos, sin):
    cols = []
    for c in range(x.shape[1] // LANES):
        xc = x[:, c * LANES:(c + 1) * LANES]
        cols.append(xc * cos + _swap_halves(xc) * sin)
    return jnp.concatenate(cols, axis=1)


def _pool_mix(u, halo, tok0):
    ts = u.shape[0]
    a1 = jnp.concatenate([halo, u], axis=0)
    length = ts + POOL_HALO
    a2 = a1[1:] + a1[:length - 1]
    a4 = a2[2:] + a2[:length - 3]
    a8 = a4[4:] + a4[:length - 7]
    a16 = a8[8:] + a8[:length - 15]
    sums = {2: a2[POOL_HALO - 1:], 4: a4[POOL_HALO - 3:], 8: a8[POOL_HALO - 7:], 16: a16[POOL_HALO - 15:]}

    tok = tok0 + lax.broadcasted_iota(jnp.int32, (ts, 1), 0)
    chan = lax.broadcasted_iota(jnp.int32, (1, POOL_WIDTH), 1)
    d = jnp.zeros((ts, POOL_WIDTH), F32)
    for gi, w in enumerate(POOL_WINDOWS):
        inv_cnt = 1.0 / jnp.minimum(tok + 1, w).astype(F32)
        in_group = (chan >= gi * POOL_GROUP_DIM) & (chan < (gi + 1) * POOL_GROUP_DIM)
        d = jnp.where(in_group, sums[w] * inv_cnt, d)
    return d - u


def _inproj_body(sink_ref, x_ref, g_ref, w_ref, wlr_ref, wa2_ref, ba_ref, cos_ref, sin_ref, wbd_ref,
                 ps_ref, proj_ref, gk_ref, h_ref, ypool_ref, yatt_ref, halo_ref, kprev_ref, vprev_ref,
                 *, tiles_per_seq):
    tm = x_ref.shape[0]
    blk = ATT_BLOCK
    s_idx = pl.program_id(0) % tiles_per_seq

    @pl.when(s_idx == 0)
    def _():
        halo_ref[...] = jnp.zeros_like(halo_ref)
        kprev_ref[...] = jnp.zeros_like(kprev_ref)
        vprev_ref[...] = jnp.zeros_like(vprev_ref)

    hs, pus = [], []
    for r in range(tm // blk):
        hr = _rms(x_ref[r * blk:(r + 1) * blk, :], g_ref[...]).astype(BF16)
        hs.append(hr)
        pus.append(jnp.dot(hr, w_ref[:, _IN_PU:_IN_PU + POOL_WIDTH], preferred_element_type=F32))
    h = jnp.concatenate(hs, axis=0)
    h_ref[...] = h
    pu = jnp.concatenate(pus, axis=0)
    d = _pool_mix(pu, halo_ref[...], s_idx * tm).astype(BF16)
    halo_ref[...] = pu[tm - POOL_HALO:]

    lr = jnp.dot(h, wlr_ref[...], preferred_element_type=F32).astype(BF16)

    cos, sin = cos_ref[...], sin_ref[...]
    scale = HEAD_DIM ** -0.5
    q = _rope(jnp.dot(h, w_ref[:, _IN_QA:_IN_QA + ATT_Q_W], preferred_element_type=F32),
              cos * scale, sin * scale).astype(BF16)
    k = _rope(jnp.dot(h, w_ref[:, _IN_KA:_IN_KA + ATT_KV_W], preferred_element_type=F32),
              cos, sin).astype(BF16)
    v = jnp.dot(h, w_ref[:, _IN_VA:_IN_VA + ATT_KV_W], preferred_element_type=F32).astype(BF16)
    k = jnp.concatenate([kprev_ref[...], k], axis=0)
    v = jnp.concatenate([vprev_ref[...], v], axis=0)
    kprev_ref[...] = k[tm:]
    vprev_ref[...] = v[tm:]

    z = jnp.dot(lr, wa2_ref[...], preferred_element_type=F32) + ba_ref[...]
    gk_ref[...] = (jnp.minimum(z, 0.0) - jnp.log1p(jnp.exp(-jnp.abs(z)))) * (1.0 / GLA_TAU)
    y = jnp.dot(d, wbd_ref[...], preferred_element_type=F32)
    ypool_ref[...] = (y * ps_ref[...]).astype(ypool_ref.dtype)

    nblk = tm // blk
    mxu_w = 2 * LANES
    bounds = [mxu_w * ((PROJ_W // mxu_w) * r // nblk) for r in range(nblk)] + [PROJ_W]
    for r in range(nblk):
        has_prev = (s_idx > 0) if r == 0 else True
        o = _attend_block(q[r * blk:(r + 1) * blk], k[r * blk:(r + 2) * blk], v[r * blk:(r + 2) * blk],
                          has_prev, sink_ref)
        yatt_ref[r * blk:(r + 1) * blk, :] = o.astype(yatt_ref.dtype)
        for dst, src, width in _PROJ_SEGMENTS:
            lo, hi = max(dst, bounds[r]), min(dst + width, bounds[r + 1])
            if lo < hi:
                p = jnp.dot(h, w_ref[:, src + lo - dst:src + hi - dst], preferred_element_type=F32)
                proj_ref[:, lo:hi] = p.astype(proj_ref.dtype)


def _inproj(x, g, w_in, w_a2, b_a, cos_t, sin_t, w_bd, pool_scale, sinks, seq, layer):
    t = x.shape[0]
    tm = INPROJ_TM
    assert seq % tm == 0 and tm % ATT_BLOCK == 0
    const = lambda i: (0, 0)
    lconst = lambda i: (layer, 0, 0)
    rows = lambda i: (i, 0)
    return pl.pallas_call(
        functools.partial(_inproj_body, tiles_per_seq=seq // tm),
        out_shape=(jax.ShapeDtypeStruct((t, PROJ_W), BF16),
                   jax.ShapeDtypeStruct((t, GLA_QK_W), F32),
                   jax.ShapeDtypeStruct((t, D_MODEL), BF16),
                   jax.ShapeDtypeStruct((t, POOL_WIDTH), BF16),
                   jax.ShapeDtypeStruct((t, ATT_Q_W), BF16)),
        grid=(t // tm,),
        in_specs=[
            pl.BlockSpec(memory_space=pltpu.SMEM),
            pl.BlockSpec((tm, D_MODEL), rows),
            pl.BlockSpec((1, D_MODEL), const),
            pl.BlockSpec((None, D_MODEL, PROJ_SRC_W), lconst, pipeline_mode=pl.Buffered(1)),
            pl.BlockSpec((None, D_MODEL, LANES), lambda i: (layer, 0, _IN_LR // LANES)),
            pl.BlockSpec((None, LANES, GLA_QK_W), lconst),
            pl.BlockSpec((1, GLA_QK_W), const),
            pl.BlockSpec((tm, LANES), rows),
            pl.BlockSpec((tm, LANES), rows),
            pl.BlockSpec((None, POOL_WIDTH, POOL_WIDTH), lconst),
            pl.BlockSpec((1, POOL_WIDTH), const),
        ],
        out_specs=(pl.BlockSpec((tm, PROJ_W), rows),
                   pl.BlockSpec((tm, GLA_QK_W), rows),
                   pl.BlockSpec((tm, D_MODEL), rows),
                   pl.BlockSpec((tm, POOL_WIDTH), rows),
                   pl.BlockSpec((tm, ATT_Q_W), rows)),
        scratch_shapes=[pltpu.VMEM((POOL_HALO, POOL_WIDTH), F32),
                        pltpu.VMEM((ATT_BLOCK, ATT_KV_W), BF16),
                        pltpu.VMEM((ATT_BLOCK, ATT_KV_W), BF16)],
        compiler_params=_cparams("arbitrary"),
        name="inproj",
    )(sinks, x, g, w_in, w_in, w_a2, b_a, cos_t, sin_t, w_bd, pool_scale)


def _both_halves(x, col, half):
    xc = x[:, col * LANES:(col + 1) * LANES].astype(F32)
    sw = pltpu.roll(xc, LANES // 2, 1)
    first = lax.broadcasted_iota(jnp.int32, xc.shape, 1) < LANES // 2
    dup = jnp.where(first, xc, sw) if half == 0 else jnp.where(first, sw, xc)
    return dup.astype(BF16)


def _attend_block(q, k, v, has_prev, sink_ref):
    blk = ATT_BLOCK
    group = ATT_HEADS // ATT_KV_HEADS

    qi = lax.broadcasted_iota(jnp.int32, (blk, 2 * blk), 0)
    si = lax.broadcasted_iota(jnp.int32, (blk, 2 * blk), 1)
    rel = blk + qi - si
    mask = (rel >= 0) & (rel < blk) & ((si >= blk) | has_prev)
    bias = jnp.where(mask, 0.0, -1e30)
    first = lax.broadcasted_iota(jnp.int32, (blk, LANES), 1) < LANES // 2
    zero = jnp.zeros((blk, LANES), BF16)

    heads = [None] * ATT_HEADS
    for g in range(ATT_KV_HEADS):
        kk = _both_halves(k, g // 2, g % 2)
        vv = _both_halves(v, g // 2, g % 2)
        hs = range(g * group, (g + 1) * group)
        qm = jnp.concatenate(
            [jnp.where(first if hd % 2 == 0 else ~first, q[:, (hd // 2) * LANES:(hd // 2 + 1) * LANES], zero)
             for hd in hs], axis=0)
        s3 = lax.dot_general(qm, kk, (((1,), (1,)), ((), ())), preferred_element_type=F32)
        es, denoms = [], []
        for r, hd in enumerate(hs):
            s = s3[r * blk:(r + 1) * blk] + bias
            sink = sink_ref[hd]
            m = jnp.maximum(jnp.max(s, axis=1, keepdims=True), sink)
            e = jnp.exp(s - m)
            denoms.append(jnp.sum(e, axis=1, keepdims=True) + jnp.exp(sink - m))
            es.append(e.astype(BF16))
        o3 = jnp.dot(jnp.concatenate(es, axis=0), vv, preferred_element_type=F32)
        for r, hd in enumerate(hs):
            heads[hd] = o3[r * blk:(r + 1) * blk] / denoms[r]
    cols = [jnp.where(first, heads[2 * c], heads[2 * c + 1]) for c in range(ATT_HEADS // 2)]
    return jnp.concatenate(cols, axis=1)


def _gla_body(qk_ref, v_ref, og_ref, gk_ref, norm_ref, tril_ref, bd_ref, o_ref, st_ref, stbf_ref,
              oacc_ref):
    s_idx = pl.program_id(1)
    npair, ts = qk_ref.shape[0], qk_ref.shape[1]
    c = GLA_CHUNK

    @pl.when(s_idx == 0)
    def _():
        st_ref[...] = jnp.zeros_like(st_ref)
        stbf_ref[...] = jnp.zeros_like(stbf_ref)

    tril = tril_ref[...]
    prep = []
    for p in range(npair):
        qk = qk_ref[p].astype(F32)
        q = qk[:, :GLA_QK_W] * (GLA_DK ** -0.5)
        k = qk[:, GLA_QK_W:]
        gk = gk_ref[p]
        g1 = gk.astype(BF16)
        r1 = gk - g1.astype(F32)
        g2 = r1.astype(BF16)
        g3 = (r1 - g2.astype(F32)).astype(BF16)
        b = (jnp.dot(tril, g1, preferred_element_type=F32) + jnp.dot(tril, g2, preferred_element_type=F32)
             + jnp.dot(tril, g3, preferred_element_type=F32))
        prep.append(((q * jnp.exp(b)).astype(BF16), (k * jnp.exp(-b)).astype(BF16), k, b, v_ref[p]))

    kcol = lax.broadcasted_iota(jnp.int32, (1, GLA_QK_W), 1)
    vcol = lax.broadcasted_iota(jnp.int32, (1, GLA_V_W), 1)
    ri = lax.broadcasted_iota(jnp.int32, (c, c), 0)
    ci = lax.broadcasted_iota(jnp.int32, (c, c), 1)
    causal = ri >= ci
    bd = bd_ref[...]

    units = [(n, p) for n in range(ts // c) for p in range(npair)]
    scores = {}
    for n, p in units:
        rows = slice(n * c, (n + 1) * c)
        q_n = prep[p][0][rows]
        q4 = jnp.concatenate(
            [jnp.where((kcol >= h * GLA_DK) & (kcol < (h + 1) * GLA_DK), q_n, jnp.zeros_like(q_n))
             for h in range(GLA_HEADS)], axis=0)
        scores[n, p] = lax.dot_general(q4, prep[p][1][rows], (((1,), (1,)), ((), ())),
                                       preferred_element_type=F32)
    for n, p in units:
        rows = slice(n * c, (n + 1) * c)
        v_n = prep[p][4][rows]
        o_n = jnp.zeros((c, GLA_V_W), F32)
        for h in range(GLA_HEADS):
            a_h = jnp.where(causal, scores[n, p][h * c:(h + 1) * c], 0.0).astype(BF16)
            o_h = jnp.dot(a_h, v_n, preferred_element_type=F32)
            o_n = jnp.where((vcol >= h * GLA_DV) & (vcol < (h + 1) * GLA_DV), o_h, o_n)
        oacc_ref[p, rows, :] = o_n
    for n, p in units:
        rows = slice(n * c, (n + 1) * c)
        q_t, _, k, b, v = prep[p]
        b_n = b[rows]
        b_last = b_n[c - 1:c, :]
        k_s = (k[rows] * jnp.exp(b_last - b_n)).astype(BF16)
        oacc_ref[p, rows, :] += lax.dot_general(q_t[rows], stbf_ref[p], (((1,), (1,)), ((), ())),
                                                preferred_element_type=F32)
        kv = lax.dot_general(v[rows], k_s, (((0,), (0,)), ((), ())), preferred_element_type=F32)
        dec = jnp.exp(b_last)
        for h in range(GLA_HEADS):
            vr = slice(h * GLA_DV, (h + 1) * GLA_DV)
            lo = (h * GLA_DK) // LANES * LANES
            hi = -(-((h + 1) * GLA_DK) // LANES) * LANES
            new = st_ref[p, vr, lo:hi] * dec[:, lo:hi] + kv[vr, lo:hi] * bd[vr, lo:hi]
            st_ref[p, vr, lo:hi] = new
            stbf_ref[p, vr, lo:hi] = new.astype(BF16)

    for p in range(npair):
        o = oacc_ref[p]
        o2 = o * o
        inv = jnp.zeros_like(o)
        for h in range(GLA_HEADS):
            in_head = (vcol >= h * GLA_DV) & (vcol < (h + 1) * GLA_DV)
            ms = jnp.sum(jnp.where(in_head, o2, 0.0), axis=1, keepdims=True) * (1.0 / GLA_DV)
            inv = jnp.where(in_head, lax.rsqrt(ms + EPS), inv)
        og = og_ref[p].astype(F32)
        o_ref[p] = (o * inv * norm_ref[...] * (og * jax.nn.sigmoid(og))).astype(o_ref.dtype)


def _gla(proj, gk, norm, batch, seq):
    t = proj.shape[0]
    ts = GLA_TS
    ns = seq // ts
    pair = GLA_PAIR
    assert batch % pair == 0
    r = np.arange(ts)
    tril = ((r[:, None] // GLA_CHUNK == r[None, :] // GLA_CHUNK) & (r[:, None] >= r[None, :]))
    tril = jnp.asarray(tril, BF16)
    bd = (np.arange(GLA_V_W)[:, None] // GLA_DV) == (np.arange(GLA_QK_W)[None, :] // GLA_DK)
    bd = jnp.asarray(bd, F32)
    proj3 = proj.reshape(batch, seq, PROJ_W)
    out = pl.pallas_call(
        _gla_body,
        out_shape=jax.ShapeDtypeStruct((batch, seq, GLA_V_W), BF16),
        grid=(batch // pair, ns),
        in_specs=[
            pl.BlockSpec((pair, ts, 2 * GLA_QK_W), lambda b, s: (b, s, _BLK_GQK)),
            pl.BlockSpec((pair, ts, GLA_V_W), lambda b, s: (b, s, _BLK_GV)),
            pl.BlockSpec((pair, ts, GLA_V_W), lambda b, s: (b, s, _BLK_GO)),
            pl.BlockSpec((pair, ts, GLA_QK_W), lambda b, s: (b, s, 0)),
            pl.BlockSpec((1, GLA_V_W), lambda b, s: (0, 0)),
            pl.BlockSpec((ts, ts), lambda b, s: (0, 0)),
            pl.BlockSpec((GLA_V_W, GLA_QK_W), lambda b, s: (0, 0)),
        ],
        out_specs=pl.BlockSpec((pair, ts, GLA_V_W), lambda b, s: (b, s, 0)),
        scratch_shapes=[pltpu.VMEM((pair, GLA_V_W, GLA_QK_W), F32),
                        pltpu.VMEM((pair, GLA_V_W, GLA_QK_W), BF16),
                        pltpu.VMEM((pair, ts, GLA_V_W), F32)],
        compiler_params=_cparams("parallel", "arbitrary"),
        name="gla",
    )(proj3, proj3, proj3, gk.reshape(batch, seq, GLA_QK_W), norm, tril, bd)
    return out.reshape(t, GLA_V_W)


def _merge_body(h_ref, ya_ref, yp_ref, yg_ref, wga_ref, wgp_ref, wgg_ref, tga_ref, tgp_ref, tgg_ref,
                bga_ref, bgp_ref, bgg_ref, wa_ref, wp_ref, wg_ref, o_ref, wgate_ref):
    tn = o_ref.shape[1]

    @pl.when(pl.program_id(1) == 0)
    def _():
        for br, (win_ref, tail_ref) in enumerate(((wga_ref, tga_ref), (wgp_ref, tgp_ref),
                                                  (wgg_ref, tgg_ref))):
            win = jnp.concatenate([win_ref[...], tail_ref[...]], axis=1).astype(F32)
            wgate_ref[br] = win[:, GATE_LEAD:GATE_LEAD + tn].astype(BF16)

    h = h_ref[...]
    acc = None
    for br, (y_ref, bgate_ref, wbr_ref) in enumerate(((ya_ref, bga_ref, wa_ref),
                                                     (yp_ref, bgp_ref, wp_ref),
                                                     (yg_ref, bgg_ref, wg_ref))):
        gate = jax.nn.sigmoid(jnp.dot(h, wgate_ref[br], preferred_element_type=F32) + bgate_ref[...])
        term = gate * jnp.dot(y_ref[...], wbr_ref[...], preferred_element_type=F32)
        acc = term if acc is None else acc + term
    o_ref[...] = acc.astype(o_ref.dtype)


def _merge(h, y_att, y_pool, y_gla, w_in, b_gate, w_br_att, w_br_pool, w_br_gla, layer):
    t = h.shape[0]
    tm, tn = MERGE_TM, MERGE_TN
    nn = D_MODEL // tn
    row = lambda j, i: (i, 0)
    y_spec = pl.BlockSpec((tm, ATT_Q_W), row)
    gate_base = _IN_GATE - GATE_LEAD
    gate_specs = [pl.BlockSpec((pl.Squeezed(), pl.Element(D_MODEL), pl.Element(tn)),
                               functools.partial(
                                   lambda j, i, br: (layer, 0, pl.multiple_of(
                                       gate_base + br * D_MODEL + j * tn, LANES)), br=br))
                  for br in range(N_BRANCH)]
    tail_specs = [pl.BlockSpec((None, D_MODEL, LANES),
                               functools.partial(
                                   lambda j, i, br: (layer, 0, (gate_base + br * D_MODEL) // LANES
                                                     + (j + 1) * (tn // LANES)), br=br))
                  for br in range(N_BRANCH)]
    bias_specs = [pl.BlockSpec((1, tn), functools.partial(lambda j, i, br: (0, br * nn + j), br=br))
                  for br in range(N_BRANCH)]
    br_spec = pl.BlockSpec((None, ATT_Q_W, tn), lambda j, i: (layer, 0, j))
    return pl.pallas_call(
        _merge_body,
        out_shape=jax.ShapeDtypeStruct((t, D_MODEL), BF16),
        grid=(nn, t // tm),
        in_specs=[pl.BlockSpec((tm, D_MODEL), row), y_spec, y_spec, y_spec,
                  *gate_specs, *tail_specs, *bias_specs, br_spec, br_spec, br_spec],
        out_specs=pl.BlockSpec((tm, tn), lambda j, i: (i, j)),
        scratch_shapes=[pltpu.VMEM((N_BRANCH, D_MODEL, tn), BF16)],
        compiler_params=_cparams("arbitrary", "arbitrary"),
        name="gated_merge",
    )(h, y_att, y_pool, y_gla, w_in, w_in, w_in, w_in, w_in, w_in, b_gate, b_gate, b_gate,
      w_br_att, w_br_pool, w_br_gla)


def _outproj_body(m_ref, w_ref, x_ref, o_ref):
    o_ref[...] = x_ref[...] + jnp.dot(m_ref[...], w_ref[...], preferred_element_type=F32)


def _outproj(merged, w_out, x, layer):
    t = x.shape[0]
    tm, tn = OUT_TM, OUT_TN
    return pl.pallas_call(
        _outproj_body,
        out_shape=jax.ShapeDtypeStruct((t, D_MODEL), F32),
        grid=(t // tm, D_MODEL // tn),
        in_specs=[
            pl.BlockSpec((tm, D_MODEL), lambda i, j: (i, 0)),
            pl.BlockSpec((None, D_MODEL, tn), lambda i, j: (layer, 0, j),
                         pipeline_mode=pl.Buffered(1) if tn == D_MODEL else None),
            pl.BlockSpec((tm, tn), lambda i, j: (i, j)),
        ],
        out_specs=pl.BlockSpec((tm, tn), lambda i, j: (i, j)),
        compiler_params=_cparams("parallel", "arbitrary"),
        name="outproj",
    )(merged, w_out, x)


def _prep_mixer_weights(w_in, w_pool, w_gla_a2, w_br_att, w_br_pool, w_br_gla, w_out):
    depth = w_in.shape[0]
    w_in_bf = w_in.astype(BF16)
    w_a2 = jnp.pad(w_gla_a2.astype(BF16), ((0, 0), (0, LANES - GLA_LOWRANK), (0, 0)))
    groups = len(POOL_WINDOWS)
    eye = jnp.eye(groups, dtype=BF16)
    w_bd = (w_pool.astype(BF16)[:, :, :, None, :] * eye[None, :, None, :, None]).reshape(
        depth, POOL_WIDTH, POOL_WIDTH)
    return dict(w_in=w_in_bf, w_a2=w_a2, w_bd=w_bd,
                w_br_att=w_br_att.astype(BF16), w_br_pool=w_br_pool.astype(BF16),
                w_br_gla=w_br_gla.astype(BF16), w_out=w_out.astype(BF16))


def _mixer(x, cos_t, sin_t, batch, seq, layer, mw, norm_mix, b_gate, att_sinks, pool_scale,
           b_gla_a, gla_norm):
    proj, gk, h, y_pool, y_att = _inproj(x, norm_mix[None, :], mw["w_in"], mw["w_a2"],
                                         b_gla_a[None, :], cos_t, sin_t, mw["w_bd"],
                                         pool_scale[None, :], att_sinks, seq, layer)
    y_gla = _gla(proj, gk, gla_norm[None, :], batch, seq)
    merged = _merge(h, y_att, y_pool, y_gla, mw["w_in"], b_gate[None, :],
                    mw["w_br_att"], mw["w_br_pool"], mw["w_br_gla"], layer)
    return _outproj(merged, mw["w_out"], x, layer)


def kernel(x, positions, norm_ffn1, ffn1_wi, ffn1_wo, norm_mix, w_in, b_gate, att_sinks, w_pool, pool_scale, w_gla_a2, b_gla_a, gla_norm, w_br_att, w_br_pool, w_br_gla, w_out, norm_ffn2, ffn2_wi, ffn2_wo, norm_final):
    batch, seq, d = x.shape
    depth = norm_ffn1.shape[0]
    assert d == D_MODEL and seq % max(INPROJ_TM, GLA_TS, ATT_BLOCK) == 0
    t = batch * seq
    assert t % max(FFN_TM, MERGE_TM, OUT_TM, ROPE_TM, INPROJ_TM) == 0
    xt = x.reshape(t, d)
    cos_t, sin_t = _rope_tables(positions)
    gf = norm_final[None, :]
    mw = _prep_mixer_weights(w_in, w_pool, w_gla_a2, w_br_att, w_br_pool, w_br_gla, w_out)
    for l in range(depth):
        xt = _ffn(xt, norm_ffn1[l][None, :], ffn1_wi, ffn1_wo, gf, l, False)
        xt = _mixer(xt, cos_t, sin_t, batch, seq, l, mw, norm_mix[l], b_gate[l], att_sinks[l],
                    pool_scale[l], b_gla_a[l], gla_norm[l])
        xt = _ffn(xt, norm_ffn2[l][None, :], ffn2_wi, ffn2_wo, gf, l, l == depth - 1)
    return xt.reshape(batch, seq, d)
```
